```python
import jax, jax.numpy as jnp
from jax import lax
import numpy as np

D_MODEL = 1024
BATCH = 4
SEQ = 8192
DEPTH = 2
DEC_BATCH = 32
DEC_SEQ = 64
PAST_LEN = 1024

CHUNK = 64
Q_BLOCK = 128
EPS = 1e-6
N_MIXERS = 2
N_GLA_LAYERS = (DEPTH + 1) // 2
N_MLA_LAYERS = DEPTH // 2

D_FF = 2816

GLA_HEADS = 4
GLA_DK = D_MODEL // 2 // GLA_HEADS
GLA_DV = D_MODEL // GLA_HEADS
GLA_GATE_RANK = 16
GLA_GATE_NORM = 16.0
GLA_QK = GLA_HEADS * GLA_DK
GLA_VW = GLA_HEADS * GLA_DV
GLA_IN = 2 * GLA_QK + 2 * GLA_VW + GLA_GATE_RANK

MLA_HEADS = 8
MLA_NOPE = 128
MLA_ROPE = 64
MLA_V = 128
MLA_Q_LORA = 384
MLA_KV_LORA = 256
MLA_DOWN = MLA_Q_LORA + MLA_KV_LORA + MLA_ROPE
MLA_SCALE = (MLA_NOPE + MLA_ROPE) ** -0.5
ROPE_THETA = 10000.0

kernel_name = 'hybrid_gla_mla_macaron_stream_step'


def rmsnorm(x, g):
    xf = x.astype(jnp.float32)
    y = xf * lax.rsqrt(jnp.mean(xf * xf, axis=-1, keepdims=True) + EPS)
    return (y * g.astype(jnp.float32)).astype(x.dtype)


def swiglu(x, w_gate, w_up, w_down):
    return (jax.nn.silu(x @ w_gate) * (x @ w_up)) @ w_down


def rope(x, pos):
    half = MLA_ROPE // 2
    inv = ROPE_THETA ** (-jnp.arange(half, dtype=jnp.float32) / half)
    ang = pos.astype(jnp.float32)[:, None] * inv[None, :]
    ang = ang.reshape((ang.shape[0],) + (1,) * (x.ndim - 3) + (half,))
    cos, sin = jnp.cos(ang), jnp.sin(ang)
    x1, x2 = jnp.split(x.astype(jnp.float32), 2, axis=-1)
    return jnp.concatenate([x1 * cos - x2 * sin, x1 * sin + x2 * cos], axis=-1).astype(x.dtype)


def gla_recurrence(q, k, v, log_a, s0):
    B, L, H, _ = q.shape
    C = min(CHUNK, L)
    nc = L // C

    def to_chunks(t):
        return t.astype(jnp.float32).reshape(B, nc, C, H, t.shape[-1]).transpose(1, 0, 3, 2, 4)

    qc, kc, vc, ac = to_chunks(q), to_chunks(k), to_chunks(v), to_chunks(log_a)
    causal = jnp.tril(jnp.ones((C, C), dtype=bool))

    def step(s, inp):
        qi, ki, vi, ai = inp
        b = jnp.cumsum(ai, axis=-2)
        q_dec = qi * jnp.exp(b)
        k_inv = ki * jnp.exp(-b)
        scores = jnp.where(causal, jnp.einsum('bhtk,bhsk->bhts', q_dec, k_inv), 0.0)
        o = jnp.einsum('bhts,bhsv->bhtv', scores, vi) + jnp.einsum('bhtk,bhkv->bhtv', q_dec, s)
        b_last = b[:, :, -1:, :]
        k_tail = ki * jnp.exp(b_last - b)
        s_new = s * jnp.exp(b_last[:, :, 0, :, None]) + jnp.einsum('bhsk,bhsv->bhkv', k_tail, vi)
        return s_new, o

    s_fin, o = lax.scan(step, s0.astype(jnp.float32), (qc, kc, vc, ac))
    o = o.transpose(1, 0, 3, 2, 4).reshape(B, L, H, -1)
    return o, s_fin


def gla_mixer(h, s0, w_in, w_gk_up, b_gk, g_norm, w_out):
    B, L, _ = h.shape
    proj = h @ w_in
    q, k, v, g, gk_low = jnp.split(proj, [GLA_QK, 2 * GLA_QK, 2 * GLA_QK + GLA_VW, 2 * GLA_QK + 2 * GLA_VW], axis=-1)
    log_a = jax.nn.log_sigmoid((gk_low @ w_gk_up + b_gk).astype(jnp.float32)) / GLA_GATE_NORM
    q = q.reshape(B, L, GLA_HEADS, GLA_DK) * (GLA_DK ** -0.5)
    k = k.reshape(B, L, GLA_HEADS, GLA_DK)
    v = v.reshape(B, L, GLA_HEADS, GLA_DV)
    log_a = log_a.reshape(B, L, GLA_HEADS, GLA_DK)
    o, s_fin = gla_recurrence(q, k, v, log_a, s0)
    o = rmsnorm(o.astype(h.dtype), g_norm).reshape(B, L, GLA_VW) * jax.nn.silu(g)
    return o @ w_out, s_fin.astype(s0.dtype)


def mla_attend(q_lat, q_pe, ckv, kpe, q_pos, k_pos):
    s = jnp.einsum('bqhc,bkc->bhqk', q_lat, ckv) + jnp.einsum('bqhr,bkr->bhqk', q_pe, kpe)
    s = s.astype(jnp.float32) * MLA_SCALE
    visible = (k_pos[None, :] // CHUNK) <= (q_pos[:, None] // CHUNK)
    s = jnp.where(visible, s, -jnp.inf)
    p = jax.nn.softmax(s, axis=-1).astype(ckv.dtype)
    return jnp.einsum('bhqk,bkc->bqhc', p, ckv)


def mla_mixer(h, pos, past_ckv, past_kpe, past_pos, w_down, q_norm, w_uq, kv_norm, w_uk, w_uv, w_out):
    B, L, _ = h.shape
    cq, ckv_raw, kpe_raw = jnp.split(h @ w_down, [MLA_Q_LORA, MLA_Q_LORA + MLA_KV_LORA], axis=-1)
    q = (rmsnorm(cq, q_norm) @ w_uq).reshape(B, L, MLA_HEADS, MLA_NOPE + MLA_ROPE)
    q_nope, q_pe = jnp.split(q, [MLA_NOPE], axis=-1)
    q_pe = rope(q_pe, pos)
    ckv_new = rmsnorm(ckv_raw, kv_norm)
    kpe_new = rope(kpe_raw, pos)
    q_lat = jnp.einsum('bqhd,chd->bqhc', q_nope, w_uk)
    if past_ckv is None:
        ckv, kpe, k_pos = ckv_new, kpe_new, pos
    else:
        ckv = jnp.concatenate([past_ckv, ckv_new], axis=1)
        kpe = jnp.concatenate([past_kpe, kpe_new], axis=1)
        k_pos = jnp.concatenate([past_pos, pos])
    if L > Q_BLOCK:
        nb = L // Q_BLOCK
        qb = q_lat.reshape(B, nb, Q_BLOCK, MLA_HEADS, MLA_KV_LORA).swapaxes(0, 1)
        pb = q_pe.reshape(B, nb, Q_BLOCK, MLA_HEADS, MLA_ROPE).swapaxes(0, 1)
        posb = pos.reshape(nb, Q_BLOCK)
        o_lat = lax.map(lambda a: mla_attend(a[0], a[1], ckv, kpe, a[2], k_pos), (qb, pb, posb))
        o_lat = o_lat.swapaxes(0, 1).reshape(B, L, MLA_HEADS, MLA_KV_LORA)
    else:
        o_lat = mla_attend(q_lat, q_pe, ckv, kpe, pos, k_pos)
    o = jnp.einsum('bqhc,chv->bqhv', o_lat, w_uv).reshape(B, L, MLA_HEADS * MLA_V)
    return o @ w_out, ckv_new, kpe_new


def setup_inputs(seed: int = 0) -> dict:
    key = jax.random.key(seed)
    ks = iter(jax.random.split(key, 32))

    def w(shape, fan_in):
        return jax.random.normal(next(ks), shape, jnp.float32) * (fan_in ** -0.5)

    def gain(shape):
        return 1.0 + 0.02 * jax.random.normal(next(ks), shape, jnp.float32)

    D, F = D_MODEL, D_FF
    return {
        'x_prompt': jax.random.normal(next(ks), (BATCH, SEQ, D), jnp.float32),
        'x_sample': jax.random.normal(next(ks), (DEC_BATCH, DEC_SEQ, D), jnp.float32),
        'state_gla': w((N_GLA_LAYERS, DEC_BATCH, GLA_HEADS, GLA_DK, GLA_DV), GLA_DK),
        'cache_ckv': jax.random.normal(next(ks), (N_MLA_LAYERS, DEC_BATCH, PAST_LEN, MLA_KV_LORA), jnp.float32),
        'cache_kpe': jax.random.normal(next(ks), (N_MLA_LAYERS, DEC_BATCH, PAST_LEN, MLA_ROPE), jnp.float32),
        'norm_ffn1': gain((DEPTH, D)),
        'w_ffn1_gate': w((DEPTH, D, F), D),
        'w_ffn1_up': w((DEPTH, D, F), D),
        'w_ffn1_down': w((DEPTH, F, D), F),
        'norm_mix': gain((DEPTH, D)),
        'norm_ffn2': gain((DEPTH, D)),
        'w_ffn2_gate': w((DEPTH, D, F), D),
        'w_ffn2_up': w((DEPTH, D, F), D),
        'w_ffn2_down': w((DEPTH, F, D), F),
        'w_gla_in': w((N_GLA_LAYERS, D, GLA_IN), D),
        'w_gla_gk_up': w((N_GLA_LAYERS, GLA_GATE_RANK, GLA_QK), GLA_GATE_RANK),
        'b_gla_gk': 0.1 * jax.random.normal(next(ks), (N_GLA_LAYERS, GLA_QK), jnp.float32),
        'gla_out_norm': gain((N_GLA_LAYERS, GLA_DV)),
        'w_gla_out': w((N_GLA_LAYERS, GLA_VW, D), GLA_VW),
        'w_mla_down': w((N_MLA_LAYERS, D, MLA_DOWN), D),
        'mla_q_norm': gain((N_MLA_LAYERS, MLA_Q_LORA)),
        'w_mla_uq': w((N_MLA_LAYERS, MLA_Q_LORA, MLA_HEADS * (MLA_NOPE + MLA_ROPE)), MLA_Q_LORA),
        'mla_kv_norm': gain((N_MLA_LAYERS, MLA_KV_LORA)),
        'w_mla_uk': w((N_MLA_LAYERS, MLA_KV_LORA, MLA_HEADS, MLA_NOPE), MLA_KV_LORA),
        'w_mla_uv': w((N_MLA_LAYERS, MLA_KV_LORA, MLA_HEADS, MLA_V), MLA_KV_LORA),
        'w_mla_out': w((N_MLA_LAYERS, MLA_HEADS * MLA_V, D), MLA_HEADS * MLA_V),
        'norm_final': gain((D,)),
    }


def reference(x_prompt, x_sample, state_gla, cache_ckv, cache_kpe,
              norm_ffn1, w_ffn1_gate, w_ffn1_up, w_ffn1_down, norm_mix,
              norm_ffn2, w_ffn2_gate, w_ffn2_up, w_ffn2_down,
              w_gla_in, w_gla_gk_up, b_gla_gk, gla_out_norm, w_gla_out,
              w_mla_down, mla_q_norm, w_mla_uq, mla_kv_norm, w_mla_uk, w_mla_uv, w_mla_out,
              norm_final):

    def run(x, pos, gla_s0, past_ckv, past_kpe, past_pos):
        new_gla, new_ckv, new_kpe = [], [], []
        for i in range(DEPTH):
            x = x + 0.5 * swiglu(rmsnorm(x, norm_ffn1[i]), w_ffn1_gate[i], w_ffn1_up[i], w_ffn1_down[i])
            h = rmsnorm(x, norm_mix[i])
            j = i // N_MIXERS
            if i % N_MIXERS == 0:
                y, s = gla_mixer(h, gla_s0[j], w_gla_in[j], w_gla_gk_up[j], b_gla_gk[j],
                                 gla_out_norm[j], w_gla_out[j])
                new_gla.append(s)
            else:
                pc = None if past_ckv is None else past_ckv[j]
                pk = None if past_kpe is None else past_kpe[j]
                y, c, r = mla_mixer(h, pos, pc, pk, past_pos, w_mla_down[j], mla_q_norm[j], w_mla_uq[j],
                                    mla_kv_norm[j], w_mla_uk[j], w_mla_uv[j], w_mla_out[j])
                new_ckv.append(c)
                new_kpe.append(r)
            x = x + y
            x = x + 0.5 * swiglu(rmsnorm(x, norm_ffn2[i]), w_ffn2_gate[i], w_ffn2_up[i], w_ffn2_down[i])
        return rmsnorm(x, norm_final), jnp.stack(new_gla), jnp.stack(new_ckv), jnp.stack(new_kpe)

    bp, lp = x_prompt.shape[0], x_prompt.shape[1]
    pos_p = jnp.arange(lp)
    s0_p = jnp.zeros((N_GLA_LAYERS, bp, GLA_HEADS, GLA_DK, GLA_DV), x_prompt.dtype)
    y_prompt, gla_p, ckv_p, kpe_p = run(x_prompt, pos_p, s0_p, None, None, None)

    past_len = cache_ckv.shape[2]
    pos_s = past_len + jnp.arange(x_sample.shape[1])
    past_pos = jnp.arange(past_len)
    y_sample, gla_s, ckv_s, kpe_s = run(x_sample, pos_s, state_gla, cache_ckv, cache_kpe, past_pos)

    return (y_prompt, y_sample, gla_p, ckv_p, kpe_p, gla_s, ckv_s, kpe_s)
```

```python
import functools

import jax
import jax.numpy as jnp
from jax import lax
from jax.experimental import pallas as pl
from jax.experimental.pallas import tpu as pltpu

F32 = jnp.float32
BF16 = jnp.bfloat16

EPS = 1e-6
CHUNK = 64
GLA_HEADS = 4
GLA_GATE_NORM = 16.0
MLA_HEADS = 8
MLA_NOPE = 128
MLA_ROPE = 64
MLA_Q_LORA = 384
MLA_KV_LORA = 256
ROPE_THETA = 10000.0
LANES = 128
KCAT = MLA_KV_LORA + LANES

VMEM_LIMIT = 56 * 1024 * 1024


def _rms(xf, g):
    return xf * lax.rsqrt(jnp.mean(xf * xf, axis=-1, keepdims=True) + EPS) * g


def _silu(a):
    return a * (1.0 / (1.0 + jnp.exp(-a)))


def _dot(a, b):
    return jnp.dot(a, b, preferred_element_type=F32)


def _dot_nt(a, b):
    return lax.dot_general(a, b, (((1,), (1,)), ((), ())), preferred_element_type=F32)


def _dot_tn(a, b):
    return lax.dot_general(a, b, (((0,), (0,)), ((), ())), preferred_element_type=F32)


def _resident(shape):
    zeros = (0,) * len(shape)
    return pl.BlockSpec(shape, lambda *_: zeros, pipeline_mode=pl.Buffered(1))


def _ffn_kernel(x_ref, g_ref, wg_ref, wu_ref, wd_ref, gf_ref, o_ref, hid_ref, *, f_chunk, final_norm):
    x = x_ref[...]
    h = _rms(x, g_ref[...]).astype(BF16)
    d_ff = wg_ref.shape[1]
    for c in range(d_ff // f_chunk):
        sl = slice(c * f_chunk, (c + 1) * f_chunk)
        a = _dot(h, wg_ref[:, sl])
        b = _dot(h, wu_ref[:, sl])
        hid_ref[:, sl] = (_silu(a) * b).astype(BF16)
    y = x + 0.5 * _dot(hid_ref[...], wd_ref[...])
    if final_norm:
        y = _rms(y, gf_ref[...])
    o_ref[...] = y


def _ffn_call(x, g, wg, wu, wd, gf, *, final_norm, tm=512, f_chunk=256):
    t, d = x.shape
    d_ff = wg.shape[1]
    assert t % tm == 0 and d_ff % f_chunk == 0
    return pl.pallas_call(
        functools.partial(_ffn_kernel, f_chunk=f_chunk, final_norm=final_norm),
        out_shape=jax.ShapeDtypeStruct((t, d), F32),
        grid=(t // tm,),
        in_specs=[
            pl.BlockSpec((tm, d), lambda i: (i, 0)),
            _resident((1, d)),
            _resident((d, d_ff)),
            _resident((d, d_ff)),
            _resident((d_ff, d)),
            _resident((1, d)),
        ],
        out_specs=pl.BlockSpec((tm, d), lambda i: (i, 0)),
        scratch_shapes=[pltpu.VMEM((tm, d_ff), BF16)],
        compiler_params=pltpu.CompilerParams(
            dimension_semantics=("parallel",), vmem_limit_bytes=VMEM_LIMIT),
        name="ffn",
    )(x, g, wg, wu, wd, gf)


def _split3(a):
    hi = a.astype(BF16)
    r = a - hi.astype(F32)
    mid = r.astype(BF16)
    lo = (r - mid.astype(F32)).astype(BF16)
    return hi, mid, lo


def _gla_kernel(x_ref, s0_ref, gmix_ref, wq_ref, wk_ref, wv_ref, wg_ref, wlow_ref, wgk_ref, bgk_ref,
                gout_ref, wout_ref, y_ref, s_ref,
                st_s, q_s, k_s, v_s, g_s, la_s, o_s, *, nb, tl):
    l = pl.program_id(1)
    heads = GLA_HEADS
    dk = wq_ref.shape[1] // heads
    dv = wv_ref.shape[1] // heads
    rows = nb * tl
    d = x_ref.shape[-1]

    @pl.when(l == 0)
    def _():
        for b in range(nb):
            for hd in range(heads):
                st_s[b, hd] = s0_ref[b, hd].T

    x = x_ref[...].reshape(rows, d)
    h = _rms(x, gmix_ref[...]).astype(BF16)
    q_s[...] = _dot(h, wq_ref[...]) * (dk ** -0.5)
    k_s[...] = _dot(h, wk_ref[...])
    v_s[...] = _dot(h, wv_ref[...]).astype(BF16)
    g_s[...] = _dot(h, wg_ref[...])
    low = _dot(h, wlow_ref[...]).astype(BF16)
    z = _dot(low, wgk_ref[...]) + bgk_ref[...]
    la_s[...] = -(jnp.maximum(-z, 0.0) + jnp.log1p(jnp.exp(-jnp.abs(z)))) / GLA_GATE_NORM

    ti = lax.broadcasted_iota(jnp.int32, (CHUNK, CHUNK), 0)
    si = lax.broadcasted_iota(jnp.int32, (CHUNK, CHUNK), 1)
    causal = si <= ti
    tri = jnp.where(causal, 1.0, 0.0).astype(BF16)

    for b in range(nb):
        for c in range(tl // CHUNK):
            r0 = b * tl + c * CHUNK
            rs = slice(r0, r0 + CHUNK)
            a_hi, a_mid, a_lo = _split3(la_s[rs, :])
            cum = _dot(tri, a_hi) + _dot(tri, a_mid) + _dot(tri, a_lo)
            cum_last = cum[CHUNK - 1:CHUNK, :]
            qc = q_s[rs, :]
            kc = k_s[rs, :]
            q_dec = (qc * jnp.exp(cum)).astype(BF16)
            k_inv = (kc * jnp.exp(-cum)).astype(BF16)
            k_tail = (kc * jnp.exp(cum_last - cum)).astype(BF16)
            dec = jnp.exp(cum_last)
            for hd in range(heads):
                ks = slice(hd * dk, (hd + 1) * dk)
                vs = slice(hd * dv, (hd + 1) * dv)
                vv = v_s[rs, vs]
                st = st_s[b, hd]
                scores = jnp.where(causal, _dot_nt(q_dec[:, ks], k_inv[:, ks]), 0.0).astype(BF16)
                o_s[rs, vs] = _dot(scores, vv) + _dot_nt(q_dec[:, ks], st.astype(BF16))
                st_s[b, hd] = st * dec[:, ks] + _dot_tn(vv, k_tail[:, ks])

    gated = []
    for hd in range(heads):
        vs = slice(hd * dv, (hd + 1) * dv)
        gated.append((_rms(o_s[:, vs], gout_ref[...]) * _silu(g_s[:, vs])).astype(BF16))
    y = x + _dot(jnp.concatenate(gated, axis=-1), wout_ref[...])
    y_ref[...] = y.reshape(nb, tl, d)

    @pl.when(l == pl.num_programs(1) - 1)
    def _():
        for b in range(nb):
            for hd in range(heads):
                s_ref[b, hd] = st_s[b, hd].T


def _gla_call(x, s0, gmix, wq, wk, wv, wg, wlow, wgk, bgk, gout, wout, *, nb, tl):
    bsz, seq, d = x.shape
    heads = GLA_HEADS
    dk = wq.shape[1] // heads
    dv = wv.shape[1] // heads
    assert bsz % nb == 0 and seq % tl == 0 and tl % CHUNK == 0
    rows = nb * tl
    return pl.pallas_call(
        functools.partial(_gla_kernel, nb=nb, tl=tl),
        out_shape=(jax.ShapeDtypeStruct((bsz, seq, d), F32),
                   jax.ShapeDtypeStruct((bsz, heads, dk, dv), F32)),
        grid=(bsz // nb, seq // tl),
        in_specs=[
            pl.BlockSpec((nb, tl, d), lambda b, l: (b, l, 0)),
            pl.BlockSpec((nb, heads, dk, dv), lambda b, l: (b, 0, 0, 0)),
            _resident(gmix.shape), _resident(wq.shape), _resident(wk.shape), _resident(wv.shape),
            _resident(wg.shape), _resident(wlow.shape), _resident(wgk.shape), _resident(bgk.shape),
            _resident(gout.shape), _resident(wout.shape),
        ],
        out_specs=(pl.BlockSpec((nb, tl, d), lambda b, l: (b, l, 0)),
                   pl.BlockSpec((nb, heads, dk, dv), lambda b, l: (b, 0, 0, 0))),
        scratch_shapes=[
            pltpu.VMEM((nb, heads, dv, dk), F32),
            pltpu.VMEM((rows, heads * dk), F32),
            pltpu.VMEM((rows, heads * dk), F32),
            pltpu.VMEM((rows, heads * dv), BF16),
            pltpu.VMEM((rows, heads * dv), F32),
            pltpu.VMEM((rows, heads * dk), F32),
            pltpu.VMEM((rows, heads * dv), F32),
        ],
        compiler_params=pltpu.CompilerParams(
            dimension_semantics=("parallel", "arbitrary"), vmem_limit_bytes=VMEM_LIMIT),
        name="gla",
    )(x, s0, gmix, wq, wk, wv, wg, wlow, wgk, bgk, gout, wout)


def _mla_proj_kernel(x_ref, cos_ref, sin_ref, gmix_ref, wdq_ref, wdkv_ref, wdk_ref, wdkr_ref, qn_ref, kvn_ref,
                     wnope_ref, wr_ref, wrr_ref, wuk_ref,
                     qcat_ref, kcat_ref, ckv_ref, kpe_ref):
    x = x_ref[0]
    h = _rms(x, gmix_ref[...]).astype(BF16)
    cos = cos_ref[...]
    sin = sin_ref[...]

    ckv = _rms(_dot(h, wdkv_ref[...]), kvn_ref[...])
    kpe = _dot(h, wdk_ref[...]) * cos + _dot(h, wdkr_ref[...]) * sin
    ckv_ref[0] = ckv
    kpe_ref[0] = kpe[:, :MLA_ROPE]
    kcat_ref[0, :, :MLA_KV_LORA] = ckv.astype(BF16)
    kcat_ref[0, :, MLA_KV_LORA:] = kpe.astype(BF16)

    hq = _rms(_dot(h, wdq_ref[...]), qn_ref[...]).astype(BF16)
    q_nope = _dot(hq, wnope_ref[...]).astype(BF16)
    q_r = _dot(hq, wr_ref[...])
    q_rr = _dot(hq, wrr_ref[...])
    for hd in range(MLA_HEADS):
        hs = slice(hd * LANES, (hd + 1) * LANES)
        qcat_ref[0, hd, :, :MLA_KV_LORA] = _dot(q_nope[:, hs], wuk_ref[hd]).astype(BF16)
        qcat_ref[0, hd, :, MLA_KV_LORA:] = (q_r[:, hs] * cos + q_rr[:, hs] * sin).astype(BF16)


def _mla_proj_call(x, cos, sin, gmix, wdq, wdkv, wdk, wdkr, qn, kvn, wnope, wr, wrr, wuk, *, tm):
    bsz, seq, d = x.shape
    assert seq % tm == 0
    weights = (gmix, wdq, wdkv, wdk, wdkr, qn, kvn, wnope, wr, wrr, wuk)
    return pl.pallas_call(
        _mla_proj_kernel,
        out_shape=(jax.ShapeDtypeStruct((bsz, MLA_HEADS, seq, KCAT), BF16),
                   jax.ShapeDtypeStruct((bsz, seq, KCAT), BF16),
                   jax.ShapeDtypeStruct((bsz, seq, MLA_KV_LORA), F32),
                   jax.ShapeDtypeStruct((bsz, seq, MLA_ROPE), F32)),
        grid=(bsz, seq // tm),
        in_specs=[
            pl.BlockSpec((1, tm, d), lambda b, l: (b, l, 0)),
            pl.BlockSpec((tm, LANES), lambda b, l: (l, 0)),
            pl.BlockSpec((tm, LANES), lambda b, l: (l, 0)),
        ] + [_resident(w.shape) for w in weights],
        out_specs=(pl.BlockSpec((1, MLA_HEADS, tm, KCAT), lambda b, l: (b, 0, l, 0)),
                   pl.BlockSpec((1, tm, KCAT), lambda b, l: (b, l, 0)),
                   pl.BlockSpec((1, tm, MLA_KV_LORA), lambda b, l: (b, l, 0)),
                   pl.BlockSpec((1, tm, MLA_ROPE), lambda b, l: (b, l, 0))),
        compiler_params=pltpu.CompilerParams(
            dimension_semantics=("parallel", "parallel"), vmem_limit_bytes=VMEM_LIMIT),
        name="mla_proj",
    )(x, cos, sin, *weights)


def _mla_attn_kernel(q_ref, kfull_ref, kdiag_ref, x_ref, wuv_ref, wout_ref, y_ref,
                     o_s, m_s, l_s, acc_s, *, hg_heads, tq, tk, past_tiles, scale):
    qb = pl.program_id(1)
    hg = pl.program_id(2)
    n_hg = pl.num_programs(2)
    m_rows = hg_heads * tq
    q = q_ref[0].reshape(m_rows, KCAT)

    kd = kdiag_ref[0]
    qi = lax.broadcasted_iota(jnp.int32, (tq, tq), 0) // CHUNK
    ki = lax.broadcasted_iota(jnp.int32, (tq, tq), 1) // CHUNK
    visible = ki <= qi
    s = _dot_nt(q, kd).reshape(hg_heads, tq, tq)
    s = jnp.where(visible[None], s, -jnp.inf).reshape(m_rows, tq)
    m0 = jnp.max(s, axis=-1, keepdims=True)
    p = jnp.exp((s - m0) * scale)
    m_s[...] = jnp.broadcast_to(m0, (m_rows, LANES))
    l_s[...] = jnp.broadcast_to(jnp.sum(p, axis=-1, keepdims=True), (m_rows, LANES))
    acc_s[...] = _dot(p.astype(BF16), kd[:, :MLA_KV_LORA])

    n_full = past_tiles + qb * (tq // tk)

    def body(j, carry):
        kt = kfull_ref[0, pl.ds(pl.multiple_of(j * tk, tk), tk), :]
        s = _dot_nt(q, kt)
        m_prev = m_s[...]
        m_new = jnp.maximum(m_prev, jnp.max(s, axis=-1, keepdims=True))
        alpha = jnp.exp((m_prev - m_new) * scale)
        p = jnp.exp((s - pltpu.repeat(m_new, tk // LANES, axis=1)) * scale)
        l_s[...] = alpha * l_s[...] + jnp.sum(p, axis=-1, keepdims=True)
        acc_s[...] = (acc_s[...] * pltpu.repeat(alpha, MLA_KV_LORA // LANES, axis=1)
                      + _dot(p.astype(BF16), kt[:, :MLA_KV_LORA]))
        m_s[...] = m_new
        return carry

    lax.fori_loop(0, n_full, body, 0)

    o_lat = acc_s[...] * pltpu.repeat(1.0 / l_s[...], MLA_KV_LORA // LANES, axis=1)
    outs = []
    for hh in range(hg_heads):
        oh = o_lat[hh * tq:(hh + 1) * tq].astype(BF16)
        outs.append(_dot(oh, wuv_ref[hg * hg_heads + hh]).astype(BF16))
    o_s[hg] = jnp.concatenate(outs, axis=-1)

    @pl.when(hg == n_hg - 1)
    def _():
        o_all = jnp.concatenate([o_s[g] for g in range(o_s.shape[0])], axis=-1)
        y_ref[0] = x_ref[0] + _dot(o_all, wout_ref[...])


def _mla_attn_call(q, kfull, kdiag, x, wuv, wout, *, hg_heads, tq, tk, scale):
    bsz, heads, seq, _ = q.shape
    d = x.shape[-1]
    lfull = kfull.shape[1]
    nq = seq // tq
    n_hg = heads // hg_heads
    assert seq % tq == 0 and heads % hg_heads == 0 and lfull % tk == 0
    if nq == 1:
        past_tiles = lfull // tk
    else:
        assert tq % tk == 0 and lfull == seq
        past_tiles = 0
    v_dim = wuv.shape[-1]
    return pl.pallas_call(
        functools.partial(_mla_attn_kernel, hg_heads=hg_heads, tq=tq, tk=tk, past_tiles=past_tiles, scale=scale),
        out_shape=jax.ShapeDtypeStruct((bsz, seq, d), F32),
        grid=(bsz, nq, n_hg),
        in_specs=[
            pl.BlockSpec((1, hg_heads, tq, KCAT), lambda b, i, g: (b, g, i, 0)),
            pl.BlockSpec((1, lfull, KCAT), lambda b, i, g: (b, 0, 0)),
            pl.BlockSpec((1, tq, KCAT), lambda b, i, g: (b, i, 0)),
            pl.BlockSpec((1, tq, d), lambda b, i, g: (b, i, 0)),
            _resident(wuv.shape),
            _resident(wout.shape),
        ],
        out_specs=pl.BlockSpec((1, tq, d), lambda b, i, g: (b, i, 0)),
        scratch_shapes=[
            pltpu.VMEM((n_hg, tq, hg_heads * v_dim), BF16),
            pltpu.VMEM((hg_heads * tq, LANES), F32),
            pltpu.VMEM((hg_heads * tq, LANES), F32),
            pltpu.VMEM((hg_heads * tq, MLA_KV_LORA), F32),
        ],
        compiler_params=pltpu.CompilerParams(
            dimension_semantics=("parallel", "parallel", "arbitrary"), vmem_limit_bytes=VMEM_LIMIT),
        name="mla_attn",
    )(q, kfull, kdiag, x, wuv, wout)


def _rope_tables(pos):
    half = MLA_ROPE // 2
    inv = ROPE_THETA ** (-jnp.arange(half, dtype=F32) / half)
    ang = pos.astype(F32)[:, None] * inv[None, :]
    pad = jnp.zeros((pos.shape[0], LANES - MLA_ROPE), F32)
    cos = jnp.concatenate([jnp.cos(ang), jnp.cos(ang), pad], axis=-1)
    sin = jnp.concatenate([jnp.sin(ang), jnp.sin(ang), pad], axis=-1)
    return cos, sin


def _rotate_half_cols(w):
    half = w.shape[-1] // 2
    return jnp.concatenate([-w[..., half:], w[..., :half]], axis=-1)


def _pad_last(w, n):
    return jnp.concatenate([w, jnp.zeros(w.shape[:-1] + (n - w.shape[-1],), w.dtype)], axis=-1)


def _prep_gla(w_in, w_gk_up, b_gk, g_norm, w_out):
    qk = w_gk_up.shape[1]
    vw = w_out.shape[0]
    wq = w_in[:, :qk]
    wk = w_in[:, qk:2 * qk]
    wv = w_in[:, 2 * qk:2 * qk + vw]
    wg = w_in[:, 2 * qk + vw:2 * qk + 2 * vw]
    wlow = _pad_last(w_in[:, 2 * qk + 2 * vw:], LANES)
    rank = w_gk_up.shape[0]
    wgk = jnp.concatenate([w_gk_up, jnp.zeros((LANES - rank, qk), w_gk_up.dtype)], axis=0)
    return (wq.astype(BF16), wk.astype(BF16), wv.astype(BF16), wg.astype(BF16), wlow.astype(BF16),
            wgk.astype(BF16), b_gk[None, :], g_norm[None, :], w_out.astype(BF16))


def _prep_mla(w_down, q_norm, w_uq, kv_norm, w_uk, w_uv, w_out):
    wdq = w_down[:, :MLA_Q_LORA]
    wdkv = w_down[:, MLA_Q_LORA:MLA_Q_LORA + MLA_KV_LORA]
    wdk = w_down[:, MLA_Q_LORA + MLA_KV_LORA:]
    wdkr = _rotate_half_cols(wdk)
    wuq = w_uq.reshape(MLA_Q_LORA, MLA_HEADS, MLA_NOPE + MLA_ROPE)
    wnope = wuq[..., :MLA_NOPE].reshape(MLA_Q_LORA, MLA_HEADS * MLA_NOPE)
    wrope = wuq[..., MLA_NOPE:]
    wr = _pad_last(wrope, LANES).reshape(MLA_Q_LORA, MLA_HEADS * LANES)
    wrr = _pad_last(_rotate_half_cols(wrope), LANES).reshape(MLA_Q_LORA, MLA_HEADS * LANES)
    wuk = jnp.transpose(w_uk, (1, 2, 0))
    wuv = jnp.transpose(w_uv, (1, 0, 2))
    return dict(
        wdq=wdq.astype(BF16), wdkv=wdkv.astype(BF16), wdk=_pad_last(wdk, LANES).astype(BF16),
        wdkr=_pad_last(wdkr, LANES).astype(BF16), qn=q_norm[None, :], kvn=kv_norm[None, :],
        wnope=wnope.astype(BF16), wr=wr.astype(BF16), wrr=wrr.astype(BF16), wuk=wuk.astype(BF16),
        wuv=wuv.astype(BF16), wout=w_out.astype(BF16))


def kernel(x_prompt, x_sample, state_gla, cache_ckv, cache_kpe, norm_ffn1, w_ffn1_gate, w_ffn1_up, w_ffn1_down, norm_mix, norm_ffn2, w_ffn2_gate, w_ffn2_up, w_ffn2_down, w_gla_in, w_gla_gk_up, b_gla_gk, gla_out_norm, w_gla_out, w_mla_down, mla_q_norm, w_mla_uq, mla_kv_norm, w_mla_uk, w_mla_uv, w_mla_out, norm_final):
    depth = norm_ffn1.shape[0]
    n_mixers = 2
    d = x_prompt.shape[-1]
    mla_scale = (MLA_NOPE + MLA_ROPE) ** -0.5
    gf = norm_final[None, :]

    ffn1 = [(norm_ffn1[i][None, :], w_ffn1_gate[i].astype(BF16), w_ffn1_up[i].astype(BF16),
             w_ffn1_down[i].astype(BF16)) for i in range(depth)]
    ffn2 = [(norm_ffn2[i][None, :], w_ffn2_gate[i].astype(BF16), w_ffn2_up[i].astype(BF16),
             w_ffn2_down[i].astype(BF16)) for i in range(depth)]
    gla_w = [_prep_gla(w_gla_in[j], w_gla_gk_up[j], b_gla_gk[j], gla_out_norm[j], w_gla_out[j])
             for j in range(w_gla_in.shape[0])]
    mla_w = [_prep_mla(w_mla_down[j], mla_q_norm[j], w_mla_uq[j], mla_kv_norm[j], w_mla_uk[j], w_mla_uv[j],
                       w_mla_out[j]) for j in range(w_mla_down.shape[0])]

    def run(x, pos, gla_s0, past_ckv, past_kpe, *, gla_nb, gla_tl, proj_tm, attn_heads, attn_tq, attn_tk):
        bsz, seq, _ = x.shape
        cos, sin = _rope_tables(pos)
        new_gla, new_ckv, new_kpe = [], [], []

        def ffn(x, w, final_norm=False):
            return _ffn_call(x.reshape(bsz * seq, d), w[0], w[1], w[2], w[3], gf,
                             final_norm=final_norm).reshape(bsz, seq, d)

        for i in range(depth):
            x = ffn(x, ffn1[i])
            j = i // n_mixers
            gm = norm_mix[i][None, :]
            if i % n_mixers == 0:
                x, s_fin = _gla_call(x, gla_s0[j], gm, *gla_w[j], nb=gla_nb, tl=gla_tl)
                new_gla.append(s_fin)
            else:
                w = mla_w[j]
                qcat, kcat, ckv, kpe = _mla_proj_call(
                    x, cos, sin, gm, w["wdq"], w["wdkv"], w["wdk"], w["wdkr"], w["qn"], w["kvn"],
                    w["wnope"], w["wr"], w["wrr"], w["wuk"], tm=proj_tm)
                if past_ckv is None:
                    kfull = kcat
                else:
                    pad = jnp.zeros(past_kpe[j].shape[:-1] + (KCAT - MLA_KV_LORA - MLA_ROPE,), BF16)
                    kfull = jnp.concatenate([past_ckv[j].astype(BF16), past_kpe[j].astype(BF16), pad], axis=-1)
                x = _mla_attn_call(qcat, kfull, kcat, x, w["wuv"], w["wout"], hg_heads=attn_heads,
                                   tq=attn_tq, tk=attn_tk, scale=mla_scale)
                new_ckv.append(ckv)
                new_kpe.append(kpe)
            x = ffn(x, ffn2[i], final_norm=(i == depth - 1))
        return x, jnp.stack(new_gla), jnp.stack(new_ckv), jnp.stack(new_kpe)

    bp, lp = x_prompt.shape[0], x_prompt.shape[1]
    n_gla = w_gla_in.shape[0]
    s0_p = jnp.zeros((n_gla, bp) + state_gla.shape[2:], x_prompt.dtype)
    y_p, gla_p, ckv_p, kpe_p = run(
        x_prompt, jnp.arange(lp), s0_p, None, None,
        gla_nb=1, gla_tl=512, proj_tm=512, attn_heads=4, attn_tq=512, attn_tk=512)

    past_len = cache_ckv.shape[2]
    ls = x_sample.shape[1]
    y_s, gla_s, ckv_s, kpe_s = run(
        x_sample, past_len + jnp.arange(ls), state_gla, cache_ckv, cache_kpe,
        gla_nb=8, gla_tl=ls, proj_tm=ls, attn_heads=MLA_HEADS, attn_tq=ls, attn_tk=512)

    return (y_p, y_s, gla_p, ckv_p, kpe_p, gla_s, ckv_s, kpe_s)
```

```python
import functools

import jax
import jax.numpy as jnp
from jax import lax
from jax.experimental import pallas as pl
from jax.experimental.pallas import tpu as pltpu

F32 = jnp.float32
BF16 = jnp.bfloat16

EPS = 1e-6
CHUNK = 64
GLA_HEADS = 4
GLA_GATE_NORM = 16.0
MLA_HEADS = 8
MLA_NOPE = 128
MLA_ROPE = 64
MLA_Q_LORA = 384
MLA_KV_LORA = 256
ROPE_THETA = 10000.0
LANES = 128
LOG2E = 1.4426950408889634
ATTN_COL_BLOCK = 512
KCAT = MLA_KV_LORA + LANES

VMEM_LIMIT = 56 * 1024 * 1024


def _rms(xf, g):
    return xf * lax.rsqrt(jnp.mean(xf * xf, axis=-1, keepdims=True) + EPS) * g


def _silu(a):
    return a * (1.0 / (1.0 + jnp.exp(-a)))


def _dot(a, b):
    return jnp.dot(a, b, preferred_element_type=F32)


def _dot_nt(a, b):
    return lax.dot_general(a, b, (((1,), (1,)), ((), ())), preferred_element_type=F32)


def _dot_tn(a, b):
    return lax.dot_general(a, b, (((0,), (0,)), ((), ())), preferred_element_type=F32)


def _resident(shape):
    zeros = (0,) * len(shape)
    return pl.BlockSpec(shape, lambda *_: zeros, pipeline_mode=pl.Buffered(1))


def _ffn_kernel(x_ref, g_ref, wg_ref, wu_ref, wd_ref, gf_ref, o_ref, hid_ref, *, f_chunk, final_norm):
    x = x_ref[...]
    h = _rms(x, g_ref[...]).astype(BF16)
    d_ff = wg_ref.shape[1]
    for c in range(d_ff // f_chunk):
        sl = slice(c * f_chunk, (c + 1) * f_chunk)
        a = _dot(h, wg_ref[:, sl])
        b = _dot(h, wu_ref[:, sl])
        hid_ref[:, sl] = (_silu(a) * b).astype(BF16)
    y = x + 0.5 * _dot(hid_ref[...], wd_ref[...])
    if final_norm:
        y = _rms(y, gf_ref[...])
    o_ref[...] = y


def _ffn_call(x, g, wg, wu, wd, gf, *, final_norm, tm=512, f_chunk=256):
    t, d = x.shape
    d_ff = wg.shape[1]
    assert t % tm == 0 and d_ff % f_chunk == 0
    return pl.pallas_call(
        functools.partial(_ffn_kernel, f_chunk=f_chunk, final_norm=final_norm),
        out_shape=jax.ShapeDtypeStruct((t, d), F32),
        grid=(t // tm,),
        in_specs=[
            pl.BlockSpec((tm, d), lambda i: (i, 0)),
            _resident((1, d)),
            _resident((d, d_ff)),
            _resident((d, d_ff)),
            _resident((d_ff, d)),
            _resident((1, d)),
        ],
        out_specs=pl.BlockSpec((tm, d), lambda i: (i, 0)),
        scratch_shapes=[pltpu.VMEM((tm, d_ff), BF16)],
        compiler_params=pltpu.CompilerParams(
            dimension_semantics=("parallel",), vmem_limit_bytes=VMEM_LIMIT),
        name="ffn",
    )(x, g, wg, wu, wd, gf)


def _split3(a):
    hi = a.astype(BF16)
    r = a - hi.astype(F32)
    mid = r.astype(BF16)
    lo = (r - mid.astype(F32)).astype(BF16)
    return hi, mid, lo


def _gla_kernel(x_ref, s0_ref, gmix_ref, wq_ref, wk_ref, wv_ref, wg_ref, wlow_ref, wgk_ref, bgk_ref,
                gout_ref, wout_ref, y_ref, s_ref,
                st_s, q_s, k_s, v_s, g_s, la_s, o_s, *, nb, tl):
    l = pl.program_id(1)
    heads = GLA_HEADS
    dk = wq_ref.shape[1] // heads
    dv = wv_ref.shape[1] // heads
    rows = nb * tl
    d = x_ref.shape[-1]

    @pl.when(l == 0)
    def _():
        for b in range(nb):
            for hd in range(heads):
                st_s[b, hd] = s0_ref[b, hd].T

    x = x_ref[...].reshape(rows, d)
    h = _rms(x, gmix_ref[...]).astype(BF16)
    q_s[...] = _dot(h, wq_ref[...]) * (dk ** -0.5)
    k_s[...] = _dot(h, wk_ref[...])
    v_s[...] = _dot(h, wv_ref[...]).astype(BF16)
    g_s[...] = _dot(h, wg_ref[...])
    low = _dot(h, wlow_ref[...]).astype(BF16)
    z = _dot(low, wgk_ref[...]) + bgk_ref[...]
    la_s[...] = -(jnp.maximum(-z, 0.0) + jnp.log1p(jnp.exp(-jnp.abs(z)))) / GLA_GATE_NORM

    ti = lax.broadcasted_iota(jnp.int32, (CHUNK, CHUNK), 0)
    si = lax.broadcasted_iota(jnp.int32, (CHUNK, CHUNK), 1)
    causal = si <= ti
    tri = jnp.where(causal, 1.0, 0.0).astype(BF16)

    for b in range(nb):
        for c in range(tl // CHUNK):
            r0 = b * tl + c * CHUNK
            rs = slice(r0, r0 + CHUNK)
            a_hi, a_mid, a_lo = _split3(la_s[rs, :])
            cum = _dot(tri, a_hi) + _dot(tri, a_mid) + _dot(tri, a_lo)
            cum_last = cum[CHUNK - 1:CHUNK, :]
            qc = q_s[rs, :]
            kc = k_s[rs, :]
            q_dec = (qc * jnp.exp(cum)).astype(BF16)
            k_inv = (kc * jnp.exp(-cum)).astype(BF16)
            k_tail = (kc * jnp.exp(cum_last - cum)).astype(BF16)
            dec = jnp.exp(cum_last)
            for hd in range(heads):
                ks = slice(hd * dk, (hd + 1) * dk)
                vs = slice(hd * dv, (hd + 1) * dv)
                vv = v_s[rs, vs]
                st = st_s[b, hd]
                scores = jnp.where(causal, _dot_nt(q_dec[:, ks], k_inv[:, ks]), 0.0).astype(BF16)
                o_s[rs, vs] = _dot(scores, vv) + _dot_nt(q_dec[:, ks], st.astype(BF16))
                st_s[b, hd] = st * dec[:, ks] + _dot_tn(vv, k_tail[:, ks])

    gated = []
    for hd in range(heads):
        vs = slice(hd * dv, (hd + 1) * dv)
        gated.append((_rms(o_s[:, vs], gout_ref[...]) * _silu(g_s[:, vs])).astype(BF16))
    y = x + _dot(jnp.concatenate(gated, axis=-1), wout_ref[...])
    y_ref[...] = y.reshape(nb, tl, d)

    @pl.when(l == pl.num_programs(1) - 1)
    def _():
        for b in range(nb):
            for hd in range(heads):
                s_ref[b, hd] = st_s[b, hd].T


def _gla_call(x, s0, gmix, wq, wk, wv, wg, wlow, wgk, bgk, gout, wout, *, nb, tl):
    bsz, seq, d = x.shape
    heads = GLA_HEADS
    dk = wq.shape[1] // heads
    dv = wv.shape[1] // heads
    assert bsz % nb == 0 and seq % tl == 0 and tl % CHUNK == 0
    rows = nb * tl
    return pl.pallas_call(
        functools.partial(_gla_kernel, nb=nb, tl=tl),
        out_shape=(jax.ShapeDtypeStruct((bsz, seq, d), F32),
                   jax.ShapeDtypeStruct((bsz, heads, dk, dv), F32)),
        grid=(bsz // nb, seq // tl),
        in_specs=[
            pl.BlockSpec((nb, tl, d), lambda b, l: (b, l, 0)),
            pl.BlockSpec((nb, heads, dk, dv), lambda b, l: (b, 0, 0, 0)),
            _resident(gmix.shape), _resident(wq.shape), _resident(wk.shape), _resident(wv.shape),
            _resident(wg.shape), _resident(wlow.shape), _resident(wgk.shape), _resident(bgk.shape),
            _resident(gout.shape), _resident(wout.shape),
        ],
        out_specs=(pl.BlockSpec((nb, tl, d), lambda b, l: (b, l, 0)),
                   pl.BlockSpec((nb, heads, dk, dv), lambda b, l: (b, 0, 0, 0))),
        scratch_shapes=[
            pltpu.VMEM((nb, heads, dv, dk), F32),
            pltpu.VMEM((rows, heads * dk), F32),
            pltpu.VMEM((rows, heads * dk), F32),
            pltpu.VMEM((rows, heads * dv), BF16),
            pltpu.VMEM((rows, heads * dv), F32),
            pltpu.VMEM((rows, heads * dk), F32),
            pltpu.VMEM((rows, heads * dv), F32),
        ],
        compiler_params=pltpu.CompilerParams(
            dimension_semantics=("parallel", "arbitrary"), vmem_limit_bytes=VMEM_LIMIT),
        name="gla",
    )(x, s0, gmix, wq, wk, wv, wg, wlow, wgk, bgk, gout, wout)


def _mla_proj_kernel(x_ref, cos_ref, sin_ref, gmix_ref, wdq_ref, wdkv_ref, wdk_ref, wdkr_ref, qn_ref, kvn_ref,
                     wnope_ref, wr_ref, wrr_ref, wuk_ref,
                     qcat_ref, kcat_ref, vt_ref, ckv_ref, kpe_ref):
    x = x_ref[0]
    h = _rms(x, gmix_ref[...]).astype(BF16)
    cos = cos_ref[...]
    sin = sin_ref[...]

    ckv = _rms(_dot(h, wdkv_ref[...]), kvn_ref[...])
    kpe = _dot(h, wdk_ref[...]) * cos + _dot(h, wdkr_ref[...]) * sin
    ckv_ref[0] = ckv
    kpe_ref[0] = kpe[:, :MLA_ROPE]
    kcat_ref[0, :, :MLA_KV_LORA] = ckv.astype(BF16)
    kcat_ref[0, :, MLA_KV_LORA:] = kpe.astype(BF16)
    vt_ref[0, 0] = ckv.T.astype(BF16)

    hq = _rms(_dot(h, wdq_ref[...]), qn_ref[...]).astype(BF16)
    q_nope = _dot(hq, wnope_ref[...]).astype(BF16)
    q_r = _dot(hq, wr_ref[...])
    q_rr = _dot(hq, wrr_ref[...])
    for hd in range(MLA_HEADS):
        hs = slice(hd * LANES, (hd + 1) * LANES)
        qcat_ref[0, hd, :, :MLA_KV_LORA] = _dot(q_nope[:, hs], wuk_ref[hd]).astype(BF16)
        qcat_ref[0, hd, :, MLA_KV_LORA:] = (q_r[:, hs] * cos + q_rr[:, hs] * sin).astype(BF16)


def _mla_proj_call(x, cos, sin, gmix, wdq, wdkv, wdk, wdkr, qn, kvn, wnope, wr, wrr, wuk, *, tm):
    bsz, seq, d = x.shape
    assert seq % tm == 0
    weights = (gmix, wdq, wdkv, wdk, wdkr, qn, kvn, wnope, wr, wrr, wuk)
    return pl.pallas_call(
        _mla_proj_kernel,
        out_shape=(jax.ShapeDtypeStruct((bsz, MLA_HEADS, seq, KCAT), BF16),
                   jax.ShapeDtypeStruct((bsz, seq, KCAT), BF16),
                   jax.ShapeDtypeStruct((bsz, seq // tm, MLA_KV_LORA, tm), BF16),
                   jax.ShapeDtypeStruct((bsz, seq, MLA_KV_LORA), F32),
                   jax.ShapeDtypeStruct((bsz, seq, MLA_ROPE), F32)),
        grid=(bsz, seq // tm),
        in_specs=[
            pl.BlockSpec((1, tm, d), lambda b, l: (b, l, 0)),
            pl.BlockSpec((tm, LANES), lambda b, l: (l, 0)),
            pl.BlockSpec((tm, LANES), lambda b, l: (l, 0)),
        ] + [_resident(w.shape) for w in weights],
        out_specs=(pl.BlockSpec((1, MLA_HEADS, tm, KCAT), lambda b, l: (b, 0, l, 0)),
                   pl.BlockSpec((1, tm, KCAT), lambda b, l: (b, l, 0)),
                   pl.BlockSpec((1, 1, MLA_KV_LORA, tm), lambda b, l: (b, l, 0, 0)),
                   pl.BlockSpec((1, tm, MLA_KV_LORA), lambda b, l: (b, l, 0)),
                   pl.BlockSpec((1, tm, MLA_ROPE), lambda b, l: (b, l, 0))),
        compiler_params=pltpu.CompilerParams(
            dimension_semantics=("parallel", "parallel"), vmem_limit_bytes=VMEM_LIMIT),
        name="mla_proj",
    )(x, cos, sin, *weights)


def _mla_attn_kernel(q_ref, kfull_ref, vtfull_ref, kdiag_ref, vtdiag_ref, x_ref, wuvt_ref, wout_ref, y_ref,
                     o_s, m_s, l_s, acc_s, *, hg_heads, tq, tk, cb, past_tiles, scale):
    qb = pl.program_id(1)
    hg = pl.program_id(2)
    n_hg = pl.num_programs(2)
    c = scale * LOG2E
    n_blk = hg_heads * tq // cb

    def q_blk(i):
        if cb >= tq:
            heads_per_blk = cb // tq
            return q_ref[0, i * heads_per_blk:(i + 1) * heads_per_blk].reshape(cb, KCAT)
        head, row0 = divmod(i * cb, tq)
        return q_ref[0, head, row0:row0 + cb]

    kd = kdiag_ref[0]
    vtd = vtdiag_ref[0, 0]
    ki = lax.broadcasted_iota(jnp.int32, (tq, cb), 0) // CHUNK
    col = lax.broadcasted_iota(jnp.int32, (tq, cb), 1)
    for i in range(n_blk):
        cs = slice(i * cb, (i + 1) * cb)
        visible = ki <= ((col + i * cb) % tq) // CHUNK
        s = jnp.where(visible, _dot_nt(kd, q_blk(i)), -jnp.inf)
        m0 = jnp.max(s, axis=0, keepdims=True)
        p = jnp.exp2((s - m0) * c)
        m_s[:, cs] = m0
        l_s[:, cs] = jnp.sum(p, axis=0, keepdims=True)
        acc_s[:, cs] = _dot(vtd, p.astype(BF16))

    n_full = past_tiles + qb * (tq // tk)

    def body(j, carry):
        kt = kfull_ref[0, pl.ds(pl.multiple_of(j * tk, tk), tk), :]
        vt = vtfull_ref[0, j]
        s_next = _dot_nt(kt, q_blk(0))
        for i in range(n_blk):
            cs = slice(i * cb, (i + 1) * cb)
            s = s_next
            if i + 1 < n_blk:
                s_next = _dot_nt(kt, q_blk(i + 1))
            m_prev = m_s[:, cs]
            m_new = jnp.maximum(m_prev, jnp.max(s, axis=0, keepdims=True))
            alpha = jnp.exp2((m_prev - m_new) * c)
            p = jnp.exp2((s - m_new) * c)
            l_s[:, cs] = alpha * l_s[:, cs] + jnp.sum(p, axis=0, keepdims=True)
            acc_s[:, cs] = acc_s[:, cs] * alpha + _dot(vt, p.astype(BF16))
            m_s[:, cs] = m_new
        return carry

    lax.fori_loop(0, n_full, body, 0)

    for hh in range(hg_heads):
        cs = slice(hh * tq, (hh + 1) * tq)
        head = hg * hg_heads + hh
        o_lat_t = (acc_s[:, cs] * (1.0 / l_s[:, cs])).astype(BF16)
        o_s[head] = _dot(wuvt_ref[head], o_lat_t).astype(BF16)

    @pl.when(hg == n_hg - 1)
    def _():
        o_t = o_s[...].reshape(o_s.shape[0] * o_s.shape[1], tq)
        y_ref[0] = x_ref[0] + _dot_tn(o_t, wout_ref[...])


def _mla_attn_call(q, kfull, vtfull, kdiag, vtdiag, x, wuvt, wout, *, hg_heads, tq, tk, scale):
    bsz, heads, seq, _ = q.shape
    d = x.shape[-1]
    lfull = kfull.shape[1]
    nq = seq // tq
    n_hg = heads // hg_heads
    assert seq % tq == 0 and heads % hg_heads == 0 and lfull % tk == 0
    assert vtfull.shape == (bsz, lfull // tk, MLA_KV_LORA, tk) and vtdiag.shape == (bsz, nq, MLA_KV_LORA, tq)
    if nq == 1:
        past_tiles = lfull // tk
    else:
        assert tq == tk and lfull == seq
        past_tiles = 0
    cb = min(hg_heads * tq, ATTN_COL_BLOCK)
    assert (cb % tq == 0 or tq % cb == 0) and (hg_heads * tq) % cb == 0
    v_dim = wuvt.shape[1]
    return pl.pallas_call(
        functools.partial(_mla_attn_kernel, hg_heads=hg_heads, tq=tq, tk=tk, cb=cb, past_tiles=past_tiles,
                          scale=scale),
        out_shape=jax.ShapeDtypeStruct((bsz, seq, d), F32),
        grid=(bsz, nq, n_hg),
        in_specs=[
            pl.BlockSpec((1, hg_heads, tq, KCAT), lambda b, i, g: (b, g, i, 0)),
            pl.BlockSpec((1, lfull, KCAT), lambda b, i, g: (b, 0, 0)),
            pl.BlockSpec((1, lfull // tk, MLA_KV_LORA, tk), lambda b, i, g: (b, 0, 0, 0)),
            pl.BlockSpec((1, tq, KCAT), lambda b, i, g: (b, i, 0)),
            pl.BlockSpec((1, 1, MLA_KV_LORA, tq), lambda b, i, g: (b, i, 0, 0)),
            pl.BlockSpec((1, tq, d), lambda b, i, g: (b, i, 0)),
            _resident(wuvt.shape),
            _resident(wout.shape),
        ],
        out_specs=pl.BlockSpec((1, tq, d), lambda b, i, g: (b, i, 0)),
        scratch_shapes=[
            pltpu.VMEM((heads, v_dim, tq), BF16),
            pltpu.VMEM((1, hg_heads * tq), F32),
            pltpu.VMEM((1, hg_heads * tq), F32),
            pltpu.VMEM((MLA_KV_LORA, hg_heads * tq), F32),
        ],
        compiler_params=pltpu.CompilerParams(
            dimension_semantics=("parallel", "parallel", "arbitrary"), vmem_limit_bytes=VMEM_LIMIT),
        name="mla_attn",
    )(q, kfull, vtfull, kdiag, vtdiag, x, wuvt, wout)


def _rope_tables(pos):
    half = MLA_ROPE // 2
    inv = ROPE_THETA ** (-jnp.arange(half, dtype=F32) / half)
    ang = pos.astype(F32)[:, None] * inv[None, :]
    pad = jnp.zeros((pos.shape[0], LANES - MLA_ROPE), F32)
    cos = jnp.concatenate([jnp.cos(ang), jnp.cos(ang), pad], axis=-1)
    sin = jnp.concatenate([jnp.sin(ang), jnp.sin(ang), pad], axis=-1)
    return cos, sin


def _rotate_half_cols(w):
    half = w.shape[-1] // 2
    return jnp.concatenate([-w[..., half:], w[..., :half]], axis=-1)


def _pad_last(w, n):
    return jnp.concatenate([w, jnp.zeros(w.shape[:-1] + (n - w.shape[-1],), w.dtype)], axis=-1)


def _prep_gla(w_in, w_gk_up, b_gk, g_norm, w_out):
    qk = w_gk_up.shape[1]
    vw = w_out.shape[0]
    wq = w_in[:, :qk]
    wk = w_in[:, qk:2 * qk]
    wv = w_in[:, 2 * qk:2 * qk + vw]
    wg = w_in[:, 2 * qk + vw:2 * qk + 2 * vw]
    wlow = _pad_last(w_in[:, 2 * qk + 2 * vw:], LANES)
    rank = w_gk_up.shape[0]
    wgk = jnp.concatenate([w_gk_up, jnp.zeros((LANES - rank, qk), w_gk_up.dtype)], axis=0)
    return (wq.astype(BF16), wk.astype(BF16), wv.astype(BF16), wg.astype(BF16), wlow.astype(BF16),
            wgk.astype(BF16), b_gk[None, :], g_norm[None, :], w_out.astype(BF16))


def _prep_mla(w_down, q_norm, w_uq, kv_norm, w_uk, w_uv, w_out):
    wdq = w_down[:, :MLA_Q_LORA]
    wdkv = w_down[:, MLA_Q_LORA:MLA_Q_LORA + MLA_KV_LORA]
    wdk = w_down[:, MLA_Q_LORA + MLA_KV_LORA:]
    wdkr = _rotate_half_cols(wdk)
    wuq = w_uq.reshape(MLA_Q_LORA, MLA_HEADS, MLA_NOPE + MLA_ROPE)
    wnope = wuq[..., :MLA_NOPE].reshape(MLA_Q_LORA, MLA_HEADS * MLA_NOPE)
    wrope = wuq[..., MLA_NOPE:]
    wr = _pad_last(wrope, LANES).reshape(MLA_Q_LORA, MLA_HEADS * LANES)
    wrr = _pad_last(_rotate_half_cols(wrope), LANES).reshape(MLA_Q_LORA, MLA_HEADS * LANES)
    wuk = jnp.transpose(w_uk, (1, 2, 0))
    wuvt = jnp.transpose(w_uv, (1, 2, 0))
    return dict(
        wdq=wdq.astype(BF16), wdkv=wdkv.astype(BF16), wdk=_pad_last(wdk, LANES).astype(BF16),
        wdkr=_pad_last(wdkr, LANES).astype(BF16), qn=q_norm[None, :], kvn=kv_norm[None, :],
        wnope=wnope.astype(BF16), wr=wr.astype(BF16), wrr=wrr.astype(BF16), wuk=wuk.astype(BF16),
        wuvt=wuvt.astype(BF16), wout=w_out.astype(BF16))


def kernel(x_prompt, x_sample, state_gla, cache_ckv, cache_kpe, norm_ffn1, w_ffn1_gate, w_ffn1_up, w_ffn1_down, norm_mix, norm_ffn2, w_ffn2_gate, w_ffn2_up, w_ffn2_down, w_gla_in, w_gla_gk_up, b_gla_gk, gla_out_norm, w_gla_out, w_mla_down, mla_q_norm, w_mla_uq, mla_kv_norm, w_mla_uk, w_mla_uv, w_mla_out, norm_final):
    depth = norm_ffn1.shape[0]
    n_mixers = 2
    d = x_prompt.shape[-1]
    mla_scale = (MLA_NOPE + MLA_ROPE) ** -0.5
    gf = norm_final[None, :]

    ffn1 = [(norm_ffn1[i][None, :], w_ffn1_gate[i].astype(BF16), w_ffn1_up[i].astype(BF16),
             w_ffn1_down[i].astype(BF16)) for i in range(depth)]
    ffn2 = [(norm_ffn2[i][None, :], w_ffn2_gate[i].astype(BF16), w_ffn2_up[i].astype(BF16),
             w_ffn2_down[i].astype(BF16)) for i in range(depth)]
    gla_w = [_prep_gla(w_gla_in[j], w_gla_gk_up[j], b_gla_gk[j], gla_out_norm[j], w_gla_out[j])
             for j in range(w_gla_in.shape[0])]
    mla_w = [_prep_mla(w_mla_down[j], mla_q_norm[j], w_mla_uq[j], mla_kv_norm[j], w_mla_uk[j], w_mla_uv[j],
                       w_mla_out[j]) for j in range(w_mla_down.shape[0])]

    def run(x, pos, gla_s0, past_ckv, past_kpe, *, gla_nb, gla_tl, proj_tm, attn_heads, attn_tq, attn_tk):
        bsz, seq, _ = x.shape
        cos, sin = _rope_tables(pos)
        new_gla, new_ckv, new_kpe = [], [], []

        def ffn(x, w, final_norm=False):
            return _ffn_call(x.reshape(bsz * seq, d), w[0], w[1], w[2], w[3], gf,
                             final_norm=final_norm).reshape(bsz, seq, d)

        for i in range(depth):
            x = ffn(x, ffn1[i])
            j = i // n_mixers
            gm = norm_mix[i][None, :]
            if i % n_mixers == 0:
                x, s_fin = _gla_call(x, gla_s0[j], gm, *gla_w[j], nb=gla_nb, tl=gla_tl)
                new_gla.append(s_fin)
            else:
                w = mla_w[j]
                qcat, kcat, vt, ckv, kpe = _mla_proj_call(
                    x, cos, sin, gm, w["wdq"], w["wdkv"], w["wdk"], w["wdkr"], w["qn"], w["kvn"],
                    w["wnope"], w["wr"], w["wrr"], w["wuk"], tm=proj_tm)
                if past_ckv is None:
                    kfull, vtfull = kcat, vt
                else:
                    pc = past_ckv[j].astype(BF16)
                    pad = jnp.zeros(past_kpe[j].shape[:-1] + (KCAT - MLA_KV_LORA - MLA_ROPE,), BF16)
                    kfull = jnp.concatenate([pc, past_kpe[j].astype(BF16), pad], axis=-1)
                    vtfull = jnp.swapaxes(pc.reshape(bsz, -1, attn_tk, MLA_KV_LORA), 2, 3)
                x = _mla_attn_call(qcat, kfull, vtfull, kcat, vt, x, w["wuvt"], w["wout"], hg_heads=attn_heads,
                                   tq=attn_tq, tk=attn_tk, scale=mla_scale)
                new_ckv.append(ckv)
                new_kpe.append(kpe)
            x = ffn(x, ffn2[i], final_norm=(i == depth - 1))
        return x, jnp.stack(new_gla), jnp.stack(new_ckv), jnp.stack(new_kpe)

    bp, lp = x_prompt.shape[0], x_prompt.shape[1]
    n_gla = w_gla_in.shape[0]
    s0_p = jnp.zeros((n_gla, bp) + state_gla.shape[2:], x_prompt.dtype)
    y_p, gla_p, ckv_p, kpe_p = run(
        x_prompt, jnp.arange(lp), s0_p, None, None,
        gla_nb=1, gla_tl=512, proj_tm=512, attn_heads=MLA_HEADS, attn_tq=512, attn_tk=512)

    past_len = cache_ckv.shape[2]
    ls = x_sample.shape[1]
    y_s, gla_s, ckv_s, kpe_s = run(
        x_sample, past_len + jnp.arange(ls), state_gla, cache_ckv, cache_kpe,
        gla_nb=8, gla_tl=ls, proj_tm=ls, attn_heads=MLA_HEADS, attn_tq=ls, attn_tk=512)

    return (y_p, y_s, gla_p, ckv_p, kpe_p, gla_s, ckv_s, kpe_s)
```

```python
import functools

import jax
import jax.numpy as jnp
from jax import lax
from jax.experimental import pallas as pl
from jax.experimental.pallas import tpu as pltpu

F32 = jnp.float32
BF16 = jnp.bfloat16

EPS = 1e-6
CHUNK = 64
GLA_HEADS = 4
GLA_GATE_NORM = 16.0
MLA_HEADS = 8
MLA_NOPE = 128
MLA_ROPE = 64
MLA_Q_LORA = 384
MLA_KV_LORA = 256
ROPE_THETA = 10000.0
LANES = 128
LOG2E = 1.4426950408889634
ATTN_COL_BLOCK = 512
ATTN_LAG_LIMIT = 64.0
KCAT = MLA_KV_LORA + LANES

VMEM_LIMIT = 56 * 1024 * 1024


def _rms(xf, g):
    return xf * lax.rsqrt(jnp.mean(xf * xf, axis=-1, keepdims=True) + EPS) * g


def _silu(a):
    return a * (1.0 / (1.0 + jnp.exp(-a)))


def _dot(a, b):
    return jnp.dot(a, b, preferred_element_type=F32)


def _dot_nt(a, b):
    return lax.dot_general(a, b, (((1,), (1,)), ((), ())), preferred_element_type=F32)


def _dot_tn(a, b):
    return lax.dot_general(a, b, (((0,), (0,)), ((), ())), preferred_element_type=F32)


def _resident(shape):
    zeros = (0,) * len(shape)
    return pl.BlockSpec(shape, lambda *_: zeros, pipeline_mode=pl.Buffered(1))


def _ffn_kernel(x_ref, g_ref, wg_ref, wu_ref, wd_ref, gf_ref, o_ref, hid_ref, *, f_chunk, final_norm):
    x = x_ref[...]
    h = _rms(x, g_ref[...]).astype(BF16)
    d_ff = wg_ref.shape[1]
    for c in range(d_ff // f_chunk):
        sl = slice(c * f_chunk, (c + 1) * f_chunk)
        a = _dot(h, wg_ref[:, sl])
        b = _dot(h, wu_ref[:, sl])
        hid_ref[:, sl] = (_silu(a) * b).astype(BF16)
    y = x + 0.5 * _dot(hid_ref[...], wd_ref[...])
    if final_norm:
        y = _rms(y, gf_ref[...])
    o_ref[...] = y


def _ffn_call(x, g, wg, wu, wd, gf, *, final_norm, tm=512, f_chunk=256):
    t, d = x.shape
    d_ff = wg.shape[1]
    assert t % tm == 0 and d_ff % f_chunk == 0
    return pl.pallas_call(
        functools.partial(_ffn_kernel, f_chunk=f_chunk, final_norm=final_norm),
        out_shape=jax.ShapeDtypeStruct((t, d), F32),
        grid=(t // tm,),
        in_specs=[
            pl.BlockSpec((tm, d), lambda i: (i, 0)),
            _resident((1, d)),
            _resident((d, d_ff)),
            _resident((d, d_ff)),
            _resident((d_ff, d)),
            _resident((1, d)),
        ],
        out_specs=pl.BlockSpec((tm, d), lambda i: (i, 0)),
        scratch_shapes=[pltpu.VMEM((tm, d_ff), BF16)],
        compiler_params=pltpu.CompilerParams(
            dimension_semantics=("parallel",), vmem_limit_bytes=VMEM_LIMIT),
        name="ffn",
    )(x, g, wg, wu, wd, gf)


def _split3(a):
    hi = a.astype(BF16)
    r = a - hi.astype(F32)
    mid = r.astype(BF16)
    lo = (r - mid.astype(F32)).astype(BF16)
    return hi, mid, lo


def _gla_kernel(x_ref, s0_ref, gmix_ref, wq_ref, wk_ref, wv_ref, wg_ref, wlow_ref, wgk_ref, bgk_ref,
                gout_ref, wout_ref, y_ref, s_ref,
                st_s, q_s, k_s, v_s, g_s, la_s, o_s, *, nb, tl):
    l = pl.program_id(1)
    heads = GLA_HEADS
    dk = wq_ref.shape[1] // heads
    dv = wv_ref.shape[1] // heads
    rows = nb * tl
    d = x_ref.shape[-1]

    @pl.when(l == 0)
    def _():
        for b in range(nb):
            for hd in range(heads):
                st_s[b, hd] = s0_ref[b, hd].T

    x = x_ref[...].reshape(rows, d)
    h = _rms(x, gmix_ref[...]).astype(BF16)
    q_s[...] = _dot(h, wq_ref[...]) * (dk ** -0.5)
    k_s[...] = _dot(h, wk_ref[...])
    v_s[...] = _dot(h, wv_ref[...]).astype(BF16)
    g_s[...] = _dot(h, wg_ref[...])
    low = _dot(h, wlow_ref[...]).astype(BF16)
    z = _dot(low, wgk_ref[...]) + bgk_ref[...]
    la_s[...] = -(jnp.maximum(-z, 0.0) + jnp.log1p(jnp.exp(-jnp.abs(z)))) / GLA_GATE_NORM

    ti = lax.broadcasted_iota(jnp.int32, (CHUNK, CHUNK), 0)
    si = lax.broadcasted_iota(jnp.int32, (CHUNK, CHUNK), 1)
    causal = si <= ti
    tri = jnp.where(causal, 1.0, 0.0).astype(BF16)

    for b in range(nb):
        for c in range(tl // CHUNK):
            r0 = b * tl + c * CHUNK
            rs = slice(r0, r0 + CHUNK)
            a_hi, a_mid, a_lo = _split3(la_s[rs, :])
            cum = _dot(tri, a_hi) + _dot(tri, a_mid) + _dot(tri, a_lo)
            cum_last = cum[CHUNK - 1:CHUNK, :]
            qc = q_s[rs, :]
            kc = k_s[rs, :]
            q_dec = (qc * jnp.exp(cum)).astype(BF16)
            k_inv = (kc * jnp.exp(-cum)).astype(BF16)
            k_tail = (kc * jnp.exp(cum_last - cum)).astype(BF16)
            dec = jnp.exp(cum_last)
            for hd in range(heads):
                ks = slice(hd * dk, (hd + 1) * dk)
                vs = slice(hd * dv, (hd + 1) * dv)
                vv = v_s[rs, vs]
                st = st_s[b, hd]
                scores = jnp.where(causal, _dot_nt(q_dec[:, ks], k_inv[:, ks]), 0.0).astype(BF16)
                o_s[rs, vs] = _dot(scores, vv) + _dot_nt(q_dec[:, ks], st.astype(BF16))
                st_s[b, hd] = st * dec[:, ks] + _dot_tn(vv, k_tail[:, ks])

    gated = []
    for hd in range(heads):
        vs = slice(hd * dv, (hd + 1) * dv)
        gated.append((_rms(o_s[:, vs], gout_ref[...]) * _silu(g_s[:, vs])).astype(BF16))
    y = x + _dot(jnp.concatenate(gated, axis=-1), wout_ref[...])
    y_ref[...] = y.reshape(nb, tl, d)

    @pl.when(l == pl.num_programs(1) - 1)
    def _():
        for b in range(nb):
            for hd in range(heads):
                s_ref[b, hd] = st_s[b, hd].T


def _gla_call(x, s0, gmix, wq, wk, wv, wg, wlow, wgk, bgk, gout, wout, *, nb, tl):
    bsz, seq, d = x.shape
    heads = GLA_HEADS
    dk = wq.shape[1] // heads
    dv = wv.shape[1] // heads
    assert bsz % nb == 0 and seq % tl == 0 and tl % CHUNK == 0
    rows = nb * tl
    return pl.pallas_call(
        functools.partial(_gla_kernel, nb=nb, tl=tl),
        out_shape=(jax.ShapeDtypeStruct((bsz, seq, d), F32),
                   jax.ShapeDtypeStruct((bsz, heads, dk, dv), F32)),
        grid=(bsz // nb, seq // tl),
        in_specs=[
            pl.BlockSpec((nb, tl, d), lambda b, l: (b, l, 0)),
            pl.BlockSpec((nb, heads, dk, dv), lambda b, l: (b, 0, 0, 0)),
            _resident(gmix.shape), _resident(wq.shape), _resident(wk.shape), _resident(wv.shape),
            _resident(wg.shape), _resident(wlow.shape), _resident(wgk.shape), _resident(bgk.shape),
            _resident(gout.shape), _resident(wout.shape),
        ],
        out_specs=(pl.BlockSpec((nb, tl, d), lambda b, l: (b, l, 0)),
                   pl.BlockSpec((nb, heads, dk, dv), lambda b, l: (b, 0, 0, 0))),
        scratch_shapes=[
            pltpu.VMEM((nb, heads, dv, dk), F32),
            pltpu.VMEM((rows, heads * dk), F32),
            pltpu.VMEM((rows, heads * dk), F32),
            pltpu.VMEM((rows, heads * dv), BF16),
            pltpu.VMEM((rows, heads * dv), F32),
            pltpu.VMEM((rows, heads * dk), F32),
            pltpu.VMEM((rows, heads * dv), F32),
        ],
        compiler_params=pltpu.CompilerParams(
            dimension_semantics=("parallel", "arbitrary"), vmem_limit_bytes=VMEM_LIMIT),
        name="gla",
    )(x, s0, gmix, wq, wk, wv, wg, wlow, wgk, bgk, gout, wout)


def _mla_proj_kernel(x_ref, cos_ref, sin_ref, gmix_ref, wdq_ref, wdkv_ref, wdk_ref, wdkr_ref, qn_ref, kvn_ref,
                     wnope_ref, wr_ref, wrr_ref, wuk_ref,
                     qcat_ref, kcat_ref, vt_ref, ckv_ref, kpe_ref):
    x = x_ref[0]
    h = _rms(x, gmix_ref[...]).astype(BF16)
    cos = cos_ref[...]
    sin = sin_ref[...]

    ckv = _rms(_dot(h, wdkv_ref[...]), kvn_ref[...])
    kpe = _dot(h, wdk_ref[...]) * cos + _dot(h, wdkr_ref[...]) * sin
    ckv_ref[0] = ckv
    kpe_ref[0] = kpe[:, :MLA_ROPE]
    kcat_ref[0, :, :MLA_KV_LORA] = ckv.astype(BF16)
    kcat_ref[0, :, MLA_KV_LORA:] = kpe.astype(BF16)
    vt_ref[0, 0] = ckv.T.astype(BF16)

    hq = _rms(_dot(h, wdq_ref[...]), qn_ref[...]).astype(BF16)
    q_nope = _dot(hq, wnope_ref[...]).astype(BF16)
    q_r = _dot(hq, wr_ref[...])
    q_rr = _dot(hq, wrr_ref[...])
    for hd in range(MLA_HEADS):
        hs = slice(hd * LANES, (hd + 1) * LANES)
        qcat_ref[0, hd, :, :MLA_KV_LORA] = _dot(q_nope[:, hs], wuk_ref[hd]).astype(BF16)
        qcat_ref[0, hd, :, MLA_KV_LORA:] = (q_r[:, hs] * cos + q_rr[:, hs] * sin).astype(BF16)


def _mla_proj_call(x, cos, sin, gmix, wdq, wdkv, wdk, wdkr, qn, kvn, wnope, wr, wrr, wuk, *, tm):
    bsz, seq, d = x.shape
    assert seq % tm == 0
    weights = (gmix, wdq, wdkv, wdk, wdkr, qn, kvn, wnope, wr, wrr, wuk)
    return pl.pallas_call(
        _mla_proj_kernel,
        out_shape=(jax.ShapeDtypeStruct((bsz, MLA_HEADS, seq, KCAT), BF16),
                   jax.ShapeDtypeStruct((bsz, seq, KCAT), BF16),
                   jax.ShapeDtypeStruct((bsz, seq // tm, MLA_KV_LORA, tm), BF16),
                   jax.ShapeDtypeStruct((bsz, seq, MLA_KV_LORA), F32),
                   jax.ShapeDtypeStruct((bsz, seq, MLA_ROPE), F32)),
        grid=(bsz, seq // tm),
        in_specs=[
            pl.BlockSpec((1, tm, d), lambda b, l: (b, l, 0)),
            pl.BlockSpec((tm, LANES), lambda b, l: (l, 0)),
            pl.BlockSpec((tm, LANES), lambda b, l: (l, 0)),
        ] + [_resident(w.shape) for w in weights],
        out_specs=(pl.BlockSpec((1, MLA_HEADS, tm, KCAT), lambda b, l: (b, 0, l, 0)),
                   pl.BlockSpec((1, tm, KCAT), lambda b, l: (b, l, 0)),
                   pl.BlockSpec((1, 1, MLA_KV_LORA, tm), lambda b, l: (b, l, 0, 0)),
                   pl.BlockSpec((1, tm, MLA_KV_LORA), lambda b, l: (b, l, 0)),
                   pl.BlockSpec((1, tm, MLA_ROPE), lambda b, l: (b, l, 0))),
        compiler_params=pltpu.CompilerParams(
            dimension_semantics=("parallel", "parallel"), vmem_limit_bytes=VMEM_LIMIT),
        name="mla_proj",
    )(x, cos, sin, *weights)


def _mla_attn_kernel(q_ref, kfull_ref, vtfull_ref, kdiag_ref, vtdiag_ref, x_ref, wuvt_ref, wout_ref, y_ref,
                     o_s, m_s, l_s, acc_s, *, hg_heads, tq, tk, cb, past_tiles, scale):
    qb = pl.program_id(1)
    hg = pl.program_id(2)
    n_hg = pl.num_programs(2)
    c = scale * LOG2E
    n_blk = hg_heads * tq // cb
    n_full = past_tiles + qb * (tq // tk)

    def q_blk(i):
        if cb >= tq:
            heads_per_blk = cb // tq
            return q_ref[0, i * heads_per_blk:(i + 1) * heads_per_blk].reshape(cb, KCAT)
        head, row0 = divmod(i * cb, tq)
        return q_ref[0, head, row0:row0 + cb]

    def diag_tile():
        kd = kdiag_ref[0]
        vtd = vtdiag_ref[0, 0]
        ki = lax.broadcasted_iota(jnp.int32, (tq, cb), 0) // CHUNK
        col = lax.broadcasted_iota(jnp.int32, (tq, cb), 1)
        for i in range(n_blk):
            cs = slice(i * cb, (i + 1) * cb)
            visible = ki <= ((col + i * cb) % tq) // CHUNK
            s = jnp.where(visible, _dot_nt(kd, q_blk(i)), -jnp.inf)
            m0 = jnp.max(s, axis=0, keepdims=True)
            p = jnp.exp2((s - m0) * c)
            m_s[:, cs] = m0
            l_s[:, cs] = jnp.sum(p, axis=0, keepdims=True)
            acc_s[:, cs] = _dot(vtd, p.astype(BF16))

    def tile_loop(one_pass):
        def body(j, lag):
            kt = kfull_ref[0, pl.ds(pl.multiple_of(j * tk, tk), tk), :]
            vt = vtfull_ref[0, j]
            s_next = _dot_nt(kt, q_blk(0))
            for i in range(n_blk):
                cs = slice(i * cb, (i + 1) * cb)
                s = s_next
                if i + 1 < n_blk:
                    s_next = _dot_nt(kt, q_blk(i + 1))
                m_prev = m_s[:, cs]
                mx = jnp.max(s, axis=0, keepdims=True)
                m_new = jnp.maximum(m_prev, mx)
                alpha = jnp.exp2((m_prev - m_new) * c)
                if one_pass:
                    p = jnp.exp2((s - m_prev) * c)
                    l_s[:, cs] = (l_s[:, cs] + jnp.sum(p, axis=0, keepdims=True)) * alpha
                    acc_s[:, cs] = (acc_s[:, cs] + _dot(vt, p.astype(BF16))) * alpha
                    lag = jnp.maximum(lag, (mx - m_prev) * c)
                else:
                    p = jnp.exp2((s - m_new) * c)
                    l_s[:, cs] = alpha * l_s[:, cs] + jnp.sum(p, axis=0, keepdims=True)
                    acc_s[:, cs] = acc_s[:, cs] * alpha + _dot(vt, p.astype(BF16))
                m_s[:, cs] = m_new
            return lag

        return lax.fori_loop(0, n_full, body, jnp.zeros((1, cb), F32))

    diag_tile()
    lag = tile_loop(one_pass=True)

    @pl.when(jnp.max(lag) > ATTN_LAG_LIMIT)
    def _():
        diag_tile()
        tile_loop(one_pass=False)

    for hh in range(hg_heads):
        cs = slice(hh * tq, (hh + 1) * tq)
        head = hg * hg_heads + hh
        o_lat_t = (acc_s[:, cs] * (1.0 / l_s[:, cs])).astype(BF16)
        o_s[head] = _dot(wuvt_ref[head], o_lat_t).astype(BF16)

    @pl.when(hg == n_hg - 1)
    def _():
        o_t = o_s[...].reshape(o_s.shape[0] * o_s.shape[1], tq)
        y_ref[0] = x_ref[0] + _dot_tn(o_t, wout_ref[...])


def _mla_attn_call(q, kfull, vtfull, kdiag, vtdiag, x, wuvt, wout, *, hg_heads, tq, tk, scale):
    bsz, heads, seq, _ = q.shape
    d = x.shape[-1]
    lfull = kfull.shape[1]
    nq = seq // tq
    n_hg = heads // hg_heads
    assert seq % tq == 0 and heads % hg_heads == 0 and lfull % tk == 0
    assert vtfull.shape == (bsz, lfull // tk, MLA_KV_LORA, tk) and vtdiag.shape == (bsz, nq, MLA_KV_LORA, tq)
    if nq == 1:
        past_tiles = lfull // tk
    else:
        assert tq == tk and lfull == seq
        past_tiles = 0
    cb = min(hg_heads * tq, ATTN_COL_BLOCK)
    assert (cb % tq == 0 or tq % cb == 0) and (hg_heads * tq) % cb == 0
    v_dim = wuvt.shape[1]
    return pl.pallas_call(
        functools.partial(_mla_attn_kernel, hg_heads=hg_heads, tq=tq, tk=tk, cb=cb, past_tiles=past_tiles,
                          scale=scale),
        out_shape=jax.ShapeDtypeStruct((bsz, seq, d), F32),
        grid=(bsz, nq, n_hg),
        in_specs=[
            pl.BlockSpec((1, hg_heads, tq, KCAT), lambda b, i, g: (b, g, i, 0)),
            pl.BlockSpec((1, lfull, KCAT), lambda b, i, g: (b, 0, 0)),
            pl.BlockSpec((1, lfull // tk, MLA_KV_LORA, tk), lambda b, i, g: (b, 0, 0, 0)),
            pl.BlockSpec((1, tq, KCAT), lambda b, i, g: (b, i, 0)),
            pl.BlockSpec((1, 1, MLA_KV_LORA, tq), lambda b, i, g: (b, i, 0, 0)),
            pl.BlockSpec((1, tq, d), lambda b, i, g: (b, i, 0)),
            _resident(wuvt.shape),
            _resident(wout.shape),
        ],
        out_specs=pl.BlockSpec((1, tq, d), lambda b, i, g: (b, i, 0)),
        scratch_shapes=[
            pltpu.VMEM((heads, v_dim, tq), BF16),
            pltpu.VMEM((1, hg_heads * tq), F32),
            pltpu.VMEM((1, hg_heads * tq), F32),
            pltpu.VMEM((MLA_KV_LORA, hg_heads * tq), F32),
        ],
        compiler_params=pltpu.CompilerParams(
            dimension_semantics=("parallel", "parallel", "arbitrary"), vmem_limit_bytes=VMEM_LIMIT),
        name="mla_attn",
    )(q, kfull, vtfull, kdiag, vtdiag, x, wuvt, wout)


def _rope_tables(pos):
    half = MLA_ROPE // 2
    inv = ROPE_THETA ** (-jnp.arange(half, dtype=F32) / half)
    ang = pos.astype(F32)[:, None] * inv[None, :]
    pad = jnp.zeros((pos.shape[0], LANES - MLA_ROPE), F32)
    cos = jnp.concatenate([jnp.cos(ang), jnp.cos(ang), pad], axis=-1)
    sin = jnp.concatenate([jnp.sin(ang), jnp.sin(ang), pad], axis=-1)
    return cos, sin


def _rotate_half_cols(w):
    half = w.shape[-1] // 2
    return jnp.concatenate([-w[..., half:], w[..., :half]], axis=-1)


def _pad_last(w, n):
    return jnp.concatenate([w, jnp.zeros(w.shape[:-1] + (n - w.shape[-1],), w.dtype)], axis=-1)


def _prep_gla(w_in, w_gk_up, b_gk, g_norm, w_out):
    qk = w_gk_up.shape[1]
    vw = w_out.shape[0]
    wq = w_in[:, :qk]
    wk = w_in[:, qk:2 * qk]
    wv = w_in[:, 2 * qk:2 * qk + vw]
    wg = w_in[:, 2 * qk + vw:2 * qk + 2 * vw]
    wlow = _pad_last(w_in[:, 2 * qk + 2 * vw:], LANES)
    rank = w_gk_up.shape[0]
    wgk = jnp.concatenate([w_gk_up, jnp.zeros((LANES - rank, qk), w_gk_up.dtype)], axis=0)
    return (wq.astype(BF16), wk.astype(BF16), wv.astype(BF16), wg.astype(BF16), wlow.astype(BF16),
            wgk.astype(BF16), b_gk[None, :], g_norm[None, :], w_out.astype(BF16))


def _prep_mla(w_down, q_norm, w_uq, kv_norm, w_uk, w_uv, w_out):
    wdq = w_down[:, :MLA_Q_LORA]
    wdkv = w_down[:, MLA_Q_LORA:MLA_Q_LORA + MLA_KV_LORA]
    wdk = w_down[:, MLA_Q_LORA + MLA_KV_LORA:]
    wdkr = _rotate_half_cols(wdk)
    wuq = w_uq.reshape(MLA_Q_LORA, MLA_HEADS, MLA_NOPE + MLA_ROPE)
    wnope = wuq[..., :MLA_NOPE].reshape(MLA_Q_LORA, MLA_HEADS * MLA_NOPE)
    wrope = wuq[..., MLA_NOPE:]
    wr = _pad_last(wrope, LANES).reshape(MLA_Q_LORA, MLA_HEADS * LANES)
    wrr = _pad_last(_rotate_half_cols(wrope), LANES).reshape(MLA_Q_LORA, MLA_HEADS * LANES)
    wuk = jnp.transpose(w_uk, (1, 2, 0))
    wuvt = jnp.transpose(w_uv, (1, 2, 0))
    return dict(
        wdq=wdq.astype(BF16), wdkv=wdkv.astype(BF16), wdk=_pad_last(wdk, LANES).astype(BF16),
        wdkr=_pad_last(wdkr, LANES).astype(BF16), qn=q_norm[None, :], kvn=kv_norm[None, :],
        wnope=wnope.astype(BF16), wr=wr.astype(BF16), wrr=wrr.astype(BF16), wuk=wuk.astype(BF16),
        wuvt=wuvt.astype(BF16), wout=w_out.astype(BF16))


def kernel(x_prompt, x_sample, state_gla, cache_ckv, cache_kpe, norm_ffn1, w_ffn1_gate, w_ffn1_up, w_ffn1_down, norm_mix, norm_ffn2, w_ffn2_gate, w_ffn2_up, w_ffn2_down, w_gla_in, w_gla_gk_up, b_gla_gk, gla_out_norm, w_gla_out, w_mla_down, mla_q_norm, w_mla_uq, mla_kv_norm, w_mla_uk, w_mla_uv, w_mla_out, norm_final):
    depth = norm_ffn1.shape[0]
    n_mixers = 2
    d = x_prompt.shape[-1]
    mla_scale = (MLA_NOPE + MLA_ROPE) ** -0.5
    gf = norm_final[None, :]

    ffn1 = [(norm_ffn1[i][None, :], w_ffn1_gate[i].astype(BF16), w_ffn1_up[i].astype(BF16),
             w_ffn1_down[i].astype(BF16)) for i in range(depth)]
    ffn2 = [(norm_ffn2[i][None, :], w_ffn2_gate[i].astype(BF16), w_ffn2_up[i].astype(BF16),
             w_ffn2_down[i].astype(BF16)) for i in range(depth)]
    gla_w = [_prep_gla(w_gla_in[j], w_gla_gk_up[j], b_gla_gk[j], gla_out_norm[j], w_gla_out[j])
             for j in range(w_gla_in.shape[0])]
    mla_w = [_prep_mla(w_mla_down[j], mla_q_norm[j], w_mla_uq[j], mla_kv_norm[j], w_mla_uk[j], w_mla_uv[j],
                       w_mla_out[j]) for j in range(w_mla_down.shape[0])]

    def run(x, pos, gla_s0, past_ckv, past_kpe, *, gla_nb, gla_tl, proj_tm, attn_heads, attn_tq, attn_tk):
        bsz, seq, _ = x.shape
        cos, sin = _rope_tables(pos)
        new_gla, new_ckv, new_kpe = [], [], []

        def ffn(x, w, final_norm=False):
            return _ffn_call(x.reshape(bsz * seq, d), w[0], w[1], w[2], w[3], gf,
                             final_norm=final_norm).reshape(bsz, seq, d)

        for i in range(depth):
            x = ffn(x, ffn1[i])
            j = i // n_mixers
            gm = norm_mix[i][None, :]
            if i % n_mixers == 0:
                x, s_fin = _gla_call(x, gla_s0[j], gm, *gla_w[j], nb=gla_nb, tl=gla_tl)
                new_gla.append(s_fin)
            else:
                w = mla_w[j]
                qcat, kcat, vt, ckv, kpe = _mla_proj_call(
                    x, cos, sin, gm, w["wdq"], w["wdkv"], w["wdk"], w["wdkr"], w["qn"], w["kvn"],
                    w["wnope"], w["wr"], w["wrr"], w["wuk"], tm=proj_tm)
                if past_ckv is None:
                    kfull, vtfull = kcat, vt
                else:
                    pc = past_ckv[j].astype(BF16)
                    pad = jnp.zeros(past_kpe[j].shape[:-1] + (KCAT - MLA_KV_LORA - MLA_ROPE,), BF16)
                    kfull = jnp.concatenate([pc, past_kpe[j].astype(BF16), pad], axis=-1)
                    vtfull = jnp.swapaxes(pc.reshape(bsz, -1, attn_tk, MLA_KV_LORA), 2, 3)
                x = _mla_attn_call(qcat, kfull, vtfull, kcat, vt, x, w["wuvt"], w["wout"], hg_heads=attn_heads,
                                   tq=attn_tq, tk=attn_tk, scale=mla_scale)
                new_ckv.append(ckv)
                new_kpe.append(kpe)
            x = ffn(x, ffn2[i], final_norm=(i == depth - 1))
        return x, jnp.stack(new_gla), jnp.stack(new_ckv), jnp.stack(new_kpe)

    bp, lp = x_prompt.shape[0], x_prompt.shape[1]
    n_gla = w_gla_in.shape[0]
    s0_p = jnp.zeros((n_gla, bp) + state_gla.shape[2:], x_prompt.dtype)
    y_p, gla_p, ckv_p, kpe_p = run(
        x_prompt, jnp.arange(lp), s0_p, None, None,
        gla_nb=1, gla_tl=512, proj_tm=512, attn_heads=MLA_HEADS, attn_tq=512, attn_tk=512)

    past_len = cache_ckv.shape[2]
    ls = x_sample.shape[1]
    y_s, gla_s, ckv_s, kpe_s = run(
        x_sample, past_len + jnp.arange(ls), state_gla, cache_ckv, cache_kpe,
        gla_nb=8, gla_tl=ls, proj_tm=ls, attn_heads=MLA_HEADS, attn_tq=ls, attn_tk=512)

    return (y_p, y_s, gla_p, ckv_p, kpe_p, gla_s, ckv_s, kpe_s)
```

```python
import functools

import jax
import jax.numpy as jnp
from jax import lax
from jax.experimental import pallas as pl
from jax.experimental.pallas import tpu as pltpu

F32 = jnp.float32
BF16 = jnp.bfloat16

EPS = 1e-6
CHUNK = 64
GLA_HEADS = 4
GLA_GATE_NORM = 16.0
MLA_HEADS = 8
MLA_NOPE = 128
MLA_ROPE = 64
MLA_V = 128
MLA_Q_LORA = 384
MLA_KV_LORA = 256
ROPE_THETA = 10000.0
LANES = 128
LOG2E = 1.4426950408889634
ATTN_COL_BLOCK = 512
ATTN_KEY_TILE = 512
ATTN_HEAD_GROUP = 4
V_ONES_ROWS = 16
ATTN_LAG_LIMIT = 64.0
KCAT = MLA_KV_LORA + LANES

VMEM_LIMIT = 56 * 1024 * 1024


def _rms(xf, g):
    return xf * lax.rsqrt(jnp.mean(xf * xf, axis=-1, keepdims=True) + EPS) * g


def _silu(a):
    return a * (1.0 / (1.0 + jnp.exp(-a)))


def _dot(a, b):
    return jnp.dot(a, b, preferred_element_type=F32)


def _dot_nt(a, b):
    return lax.dot_general(a, b, (((1,), (1,)), ((), ())), preferred_element_type=F32)


def _dot_tn(a, b):
    return lax.dot_general(a, b, (((0,), (0,)), ((), ())), preferred_element_type=F32)


def _resident(shape):
    zeros = (0,) * len(shape)
    return pl.BlockSpec(shape, lambda *_: zeros, pipeline_mode=pl.Buffered(1))


def _ffn_kernel(*refs, f_chunk, final_norm, mixer_proj):
    if mixer_proj:
        x_ref, ot_ref, wo_ref, g_ref, wg_ref, wu_ref, wd_ref, gf_ref, o_ref, hid_ref = refs
        x = x_ref[...] + _dot_tn(ot_ref[0], wo_ref[...])
    else:
        x_ref, g_ref, wg_ref, wu_ref, wd_ref, gf_ref, o_ref, hid_ref = refs
        x = x_ref[...]
    h = _rms(x, g_ref[...]).astype(BF16)
    d_ff = wg_ref.shape[1]
    for c in range(d_ff // f_chunk):
        sl = slice(c * f_chunk, (c + 1) * f_chunk)
        a = _dot(h, wg_ref[:, sl])
        b = _dot(h, wu_ref[:, sl])
        hid_ref[:, sl] = (_silu(a) * b).astype(BF16)
    y = x + 0.5 * _dot(hid_ref[...], wd_ref[...])
    if final_norm:
        y = _rms(y, gf_ref[...])
    o_ref[...] = y


def _ffn_call(x, g, wg, wu, wd, gf, *, final_norm, attn_t=None, w_out=None, tm=512, f_chunk=256):
    t, d = x.shape
    d_ff = wg.shape[1]
    assert t % tm == 0 and d_ff % f_chunk == 0
    mixer_proj = attn_t is not None
    x_spec = pl.BlockSpec((tm, d), lambda i: (i, 0))
    weights = (g, wg, wu, wd, gf)
    if mixer_proj:
        tiles_per_seq = attn_t.shape[2] // tm
        assert attn_t.shape[0] * attn_t.shape[2] == t and attn_t.shape[2] % tm == 0
        ot_spec = pl.BlockSpec((1, attn_t.shape[1], tm), lambda i: (i // tiles_per_seq, 0, i % tiles_per_seq))
        operands = (x, attn_t, w_out) + weights
        in_specs = [x_spec, ot_spec, _resident(w_out.shape)]
    else:
        operands = (x,) + weights
        in_specs = [x_spec]
    return pl.pallas_call(
        functools.partial(_ffn_kernel, f_chunk=f_chunk, final_norm=final_norm, mixer_proj=mixer_proj),
        out_shape=jax.ShapeDtypeStruct((t, d), F32),
        grid=(t // tm,),
        in_specs=in_specs + [_resident(w.shape) for w in weights],
        out_specs=pl.BlockSpec((tm, d), lambda i: (i, 0)),
        scratch_shapes=[pltpu.VMEM((tm, d_ff), BF16)],
        compiler_params=pltpu.CompilerParams(
            dimension_semantics=("parallel",), vmem_limit_bytes=VMEM_LIMIT),
        name="ffn",
    )(*operands)


def _split3(a):
    hi = a.astype(BF16)
    r = a - hi.astype(F32)
    mid = r.astype(BF16)
    lo = (r - mid.astype(F32)).astype(BF16)
    return hi, mid, lo


def _gla_kernel(x_ref, s0_ref, gmix_ref, wq_ref, wk_ref, wv_ref, wg_ref, wlow_ref, wgk_ref, bgk_ref,
                gout_ref, wout_ref, y_ref, s_ref,
                st_s, q_s, k_s, v_s, g_s, la_s, o_s, *, nb, tl):
    l = pl.program_id(1)
    heads = GLA_HEADS
    dk = wq_ref.shape[1] // heads
    dv = wv_ref.shape[1] // heads
    rows = nb * tl
    d = x_ref.shape[-1]

    @pl.when(l == 0)
    def _():
        for b in range(nb):
            for hd in range(heads):
                st_s[b, hd] = s0_ref[b, hd].T

    x = x_ref[...].reshape(rows, d)
    h = _rms(x, gmix_ref[...]).astype(BF16)
    q_s[...] = _dot(h, wq_ref[...]) * (dk ** -0.5)
    k_s[...] = _dot(h, wk_ref[...])
    v_s[...] = _dot(h, wv_ref[...]).astype(BF16)
    g_s[...] = _dot(h, wg_ref[...])
    low = _dot(h, wlow_ref[...]).astype(BF16)
    z = _dot(low, wgk_ref[...]) + bgk_ref[...]
    la_s[...] = -(jnp.maximum(-z, 0.0) + jnp.log1p(jnp.exp(-jnp.abs(z)))) / GLA_GATE_NORM

    ti = lax.broadcasted_iota(jnp.int32, (CHUNK, CHUNK), 0)
    si = lax.broadcasted_iota(jnp.int32, (CHUNK, CHUNK), 1)
    causal = si <= ti
    tri = jnp.where(causal, 1.0, 0.0).astype(BF16)

    for b in range(nb):
        for c in range(tl // CHUNK):
            r0 = b * tl + c * CHUNK
            rs = slice(r0, r0 + CHUNK)
            a_hi, a_mid, a_lo = _split3(la_s[rs, :])
            cum = _dot(tri, a_hi) + _dot(tri, a_mid) + _dot(tri, a_lo)
            cum_last = cum[CHUNK - 1:CHUNK, :]
            qc = q_s[rs, :]
            kc = k_s[rs, :]
            q_dec = (qc * jnp.exp(cum)).astype(BF16)
            k_inv = (kc * jnp.exp(-cum)).astype(BF16)
            k_tail = (kc * jnp.exp(cum_last - cum)).astype(BF16)
            dec = jnp.exp(cum_last)
            for hd in range(heads):
                ks = slice(hd * dk, (hd + 1) * dk)
                vs = slice(hd * dv, (hd + 1) * dv)
                vv = v_s[rs, vs]
                st = st_s[b, hd]
                scores = jnp.where(causal, _dot_nt(q_dec[:, ks], k_inv[:, ks]), 0.0).astype(BF16)
                o_s[rs, vs] = _dot(scores, vv) + _dot_nt(q_dec[:, ks], st.astype(BF16))
                st_s[b, hd] = st * dec[:, ks] + _dot_tn(vv, k_tail[:, ks])

    gated = []
    for hd in range(heads):
        vs = slice(hd * dv, (hd + 1) * dv)
        gated.append((_rms(o_s[:, vs], gout_ref[...]) * _silu(g_s[:, vs])).astype(BF16))
    y = x + _dot(jnp.concatenate(gated, axis=-1), wout_ref[...])
    y_ref[...] = y.reshape(nb, tl, d)

    @pl.when(l == pl.num_programs(1) - 1)
    def _():
        for b in range(nb):
            for hd in range(heads):
                s_ref[b, hd] = st_s[b, hd].T


def _gla_call(x, s0, gmix, wq, wk, wv, wg, wlow, wgk, bgk, gout, wout, *, nb, tl):
    bsz, seq, d = x.shape
    heads = GLA_HEADS
    dk = wq.shape[1] // heads
    dv = wv.shape[1] // heads
    assert bsz % nb == 0 and seq % tl == 0 and tl % CHUNK == 0
    rows = nb * tl
    return pl.pallas_call(
        functools.partial(_gla_kernel, nb=nb, tl=tl),
        out_shape=(jax.ShapeDtypeStruct((bsz, seq, d), F32),
                   jax.ShapeDtypeStruct((bsz, heads, dk, dv), F32)),
        grid=(bsz // nb, seq // tl),
        in_specs=[
            pl.BlockSpec((nb, tl, d), lambda b, l: (b, l, 0)),
            pl.BlockSpec((nb, heads, dk, dv), lambda b, l: (b, 0, 0, 0)),
            _resident(gmix.shape), _resident(wq.shape), _resident(wk.shape), _resident(wv.shape),
            _resident(wg.shape), _resident(wlow.shape), _resident(wgk.shape), _resident(bgk.shape),
            _resident(gout.shape), _resident(wout.shape),
        ],
        out_specs=(pl.BlockSpec((nb, tl, d), lambda b, l: (b, l, 0)),
                   pl.BlockSpec((nb, heads, dk, dv), lambda b, l: (b, 0, 0, 0))),
        scratch_shapes=[
            pltpu.VMEM((nb, heads, dv, dk), F32),
            pltpu.VMEM((rows, heads * dk), F32),
            pltpu.VMEM((rows, heads * dk), F32),
            pltpu.VMEM((rows, heads * dv), BF16),
            pltpu.VMEM((rows, heads * dv), F32),
            pltpu.VMEM((rows, heads * dk), F32),
            pltpu.VMEM((rows, heads * dv), F32),
        ],
        compiler_params=pltpu.CompilerParams(
            dimension_semantics=("parallel", "arbitrary"), vmem_limit_bytes=VMEM_LIMIT),
        name="gla",
    )(x, s0, gmix, wq, wk, wv, wg, wlow, wgk, bgk, gout, wout)


def _mla_proj_kernel(x_ref, cos_ref, sin_ref, gmix_ref, wdq_ref, wdkv_ref, wdk_ref, wdkr_ref, qn_ref, kvn_ref,
                     wnope_ref, wr_ref, wrr_ref, wkey_ref, wval_ref,
                     q_ref, k_ref, vt_ref, ckv_ref, kpe_ref, *, absorb, q_scale):
    x = x_ref[0]
    h = _rms(x, gmix_ref[...]).astype(BF16)
    cos = cos_ref[...]
    sin = sin_ref[...]

    ckv = _rms(_dot(h, wdkv_ref[...]), kvn_ref[...])
    kpe = _dot(h, wdk_ref[...]) * cos + _dot(h, wdkr_ref[...]) * sin
    ckv_ref[0] = ckv
    kpe_ref[0] = kpe[:, :MLA_ROPE]
    ckv_b = ckv.astype(BF16)
    kpe_b = kpe.astype(BF16)
    ckv_t = ckv.T.astype(BF16)

    hq = _rms(_dot(h, wdq_ref[...]), qn_ref[...]).astype(BF16)
    q_nope = _dot(hq, wnope_ref[...])
    q_r = _dot(hq, wr_ref[...])
    q_rr = _dot(hq, wrr_ref[...])

    if absorb:
        k_ref[0, :, :MLA_KV_LORA] = ckv_b
        k_ref[0, :, MLA_KV_LORA:] = kpe_b
        vt_ref[0, 0] = ckv_t
    else:
        k_nope = _dot(ckv_b, wkey_ref[...]).astype(BF16)
        v_t = _dot(wval_ref[...], ckv_t).astype(BF16)
        ones = jnp.ones((V_ONES_ROWS, x.shape[0]), BF16)
    for hd in range(MLA_HEADS):
        hs = slice(hd * LANES, (hd + 1) * LANES)
        q_pe = q_r[:, hs] * cos + q_rr[:, hs] * sin
        if absorb:
            q_ref[0, hd, :, :MLA_KV_LORA] = _dot(q_nope[:, hs].astype(BF16), wkey_ref[hd]).astype(BF16)
            q_ref[0, hd, :, MLA_KV_LORA:] = q_pe.astype(BF16)
        else:
            q_ref[0, hd, :, :MLA_NOPE] = (q_nope[:, hs] * q_scale).astype(BF16)
            q_ref[0, hd, :, MLA_NOPE:] = (q_pe * q_scale).astype(BF16)
            k_ref[0, hd, :, :MLA_NOPE] = k_nope[:, hs]
            k_ref[0, hd, :, MLA_NOPE:] = kpe_b
            vt_ref[0, hd, 0, :MLA_V] = v_t[hd * MLA_V:(hd + 1) * MLA_V, :]
            vt_ref[0, hd, 0, MLA_V:] = ones


def _mla_proj_call(x, cos, sin, gmix, wdq, wdkv, wdk, wdkr, qn, kvn, wnope, wr, wrr, wkey, wval, *, tm, absorb,
                   q_scale=1.0):
    bsz, seq, d = x.shape
    assert seq % tm == 0
    weights = (gmix, wdq, wdkv, wdk, wdkr, qn, kvn, wnope, wr, wrr, wkey, wval)
    nt = seq // tm
    if absorb:
        qk_dim = KCAT
        q_k_vt = (jax.ShapeDtypeStruct((bsz, MLA_HEADS, seq, qk_dim), BF16),
                  jax.ShapeDtypeStruct((bsz, seq, qk_dim), BF16),
                  jax.ShapeDtypeStruct((bsz, nt, MLA_KV_LORA, tm), BF16))
        q_k_vt_specs = (pl.BlockSpec((1, MLA_HEADS, tm, qk_dim), lambda b, l: (b, 0, l, 0)),
                        pl.BlockSpec((1, tm, qk_dim), lambda b, l: (b, l, 0)),
                        pl.BlockSpec((1, 1, MLA_KV_LORA, tm), lambda b, l: (b, l, 0, 0)))
    else:
        qk_dim = MLA_NOPE + LANES
        q_k_vt = (jax.ShapeDtypeStruct((bsz, MLA_HEADS, seq, qk_dim), BF16),
                  jax.ShapeDtypeStruct((bsz, MLA_HEADS, seq, qk_dim), BF16),
                  jax.ShapeDtypeStruct((bsz, MLA_HEADS, nt, MLA_V + V_ONES_ROWS, tm), BF16))
        q_k_vt_specs = (pl.BlockSpec((1, MLA_HEADS, tm, qk_dim), lambda b, l: (b, 0, l, 0)),
                        pl.BlockSpec((1, MLA_HEADS, tm, qk_dim), lambda b, l: (b, 0, l, 0)),
                        pl.BlockSpec((1, MLA_HEADS, 1, MLA_V + V_ONES_ROWS, tm), lambda b, l: (b, 0, l, 0, 0)))
    return pl.pallas_call(
        functools.partial(_mla_proj_kernel, absorb=absorb, q_scale=q_scale),
        out_shape=q_k_vt + (jax.ShapeDtypeStruct((bsz, seq, MLA_KV_LORA), F32),
                            jax.ShapeDtypeStruct((bsz, seq, MLA_ROPE), F32)),
        grid=(bsz, nt),
        in_specs=[
            pl.BlockSpec((1, tm, d), lambda b, l: (b, l, 0)),
            pl.BlockSpec((tm, LANES), lambda b, l: (l, 0)),
            pl.BlockSpec((tm, LANES), lambda b, l: (l, 0)),
        ] + [_resident(w.shape) for w in weights],
        out_specs=q_k_vt_specs + (pl.BlockSpec((1, tm, MLA_KV_LORA), lambda b, l: (b, l, 0)),
                                  pl.BlockSpec((1, tm, MLA_ROPE), lambda b, l: (b, l, 0))),
        compiler_params=pltpu.CompilerParams(
            dimension_semantics=("parallel", "parallel"), vmem_limit_bytes=VMEM_LIMIT),
        name="mla_proj",
    )(x, cos, sin, *weights)


def _exp2_scaled(t, c):
    return jnp.exp2(t if c == 1.0 else t * c)


def _diag_update(s, vtd, cs, m_s, l_s, acc_s, c):
    m0 = jnp.max(s, axis=0, keepdims=True)
    p = _exp2_scaled(s - m0, c)
    m_s[:, cs] = m0
    if l_s is not None:
        l_s[:, cs] = jnp.sum(p, axis=0, keepdims=True)
    acc_s[:, cs] = _dot(vtd, p.astype(BF16))


def _tile_update(s, vt, cs, m_s, l_s, acc_s, c, lag, one_pass):
    m_prev = m_s[:, cs]
    mx = jnp.max(s, axis=0, keepdims=True)
    m_new = jnp.maximum(m_prev, mx)
    alpha = _exp2_scaled(m_prev - m_new, c)
    if one_pass:
        p = _exp2_scaled(s - m_prev, c)
        if l_s is not None:
            l_s[:, cs] = (l_s[:, cs] + jnp.sum(p, axis=0, keepdims=True)) * alpha
        acc_s[:, cs] = (acc_s[:, cs] + _dot(vt, p.astype(BF16))) * alpha
        lag = jnp.maximum(lag, (mx - m_prev) * c)
    else:
        p = _exp2_scaled(s - m_new, c)
        if l_s is not None:
            l_s[:, cs] = alpha * l_s[:, cs] + jnp.sum(p, axis=0, keepdims=True)
        acc_s[:, cs] = acc_s[:, cs] * alpha + _dot(vt, p.astype(BF16))
    m_s[:, cs] = m_new
    return lag


def _attend(n_full, n_blk, cb, diag_scores, diag_values, tile_scores, tile_values, m_s, l_s, acc_s, c):
    def diag_tile():
        for i in range(n_blk):
            _diag_update(diag_scores(i), diag_values(i), slice(i * cb, (i + 1) * cb), m_s, l_s, acc_s, c)

    def tile_loop(one_pass):
        def body(j, lag):
            s_next = tile_scores(j, 0)
            for i in range(n_blk):
                s = s_next
                if i + 1 < n_blk:
                    s_next = tile_scores(j, i + 1)
                lag = _tile_update(s, tile_values(j, i), slice(i * cb, (i + 1) * cb), m_s, l_s, acc_s, c, lag,
                                   one_pass)
            return lag

        return lax.fori_loop(0, n_full, body, jnp.zeros((1, cb), F32))

    diag_tile()
    lag = tile_loop(one_pass=True)

    @pl.when(jnp.max(lag) > ATTN_LAG_LIMIT)
    def _():
        diag_tile()
        tile_loop(one_pass=False)


def _block_causal(tq, cb, col0):
    ki = lax.broadcasted_iota(jnp.int32, (tq, cb), 0) // CHUNK
    qi = ((lax.broadcasted_iota(jnp.int32, (tq, cb), 1) + col0) % tq) // CHUNK
    return ki <= qi


def _mla_attn_kernel(q_ref, kfull_ref, vtfull_ref, kdiag_ref, vtdiag_ref, x_ref, wuvt_ref, wout_ref, y_ref,
                     o_s, m_s, l_s, acc_s, *, tq, tk, cb, past_tiles, scale):
    qb = pl.program_id(1)
    heads = q_ref.shape[1]
    n_blk = heads * tq // cb
    heads_per_blk = cb // tq

    def q_blk(i):
        return q_ref[0, i * heads_per_blk:(i + 1) * heads_per_blk].reshape(cb, KCAT)

    def diag_scores(i):
        return jnp.where(_block_causal(tq, cb, i * cb), _dot_nt(kdiag_ref[0], q_blk(i)), -jnp.inf)

    def tile_scores(j, i):
        return _dot_nt(kfull_ref[0, pl.ds(pl.multiple_of(j * tk, tk), tk), :], q_blk(i))

    _attend(past_tiles + qb * (tq // tk), n_blk, cb, diag_scores, lambda i: vtdiag_ref[0, 0],
            tile_scores, lambda j, i: vtfull_ref[0, j], m_s, l_s, acc_s, scale * LOG2E)

    for hd in range(heads):
        cs = slice(hd * tq, (hd + 1) * tq)
        o_lat_t = (acc_s[:, cs] * (1.0 / l_s[:, cs])).astype(BF16)
        o_s[hd] = _dot(wuvt_ref[hd], o_lat_t).astype(BF16)
    o_t = o_s[...].reshape(o_s.shape[0] * o_s.shape[1], tq)
    y_ref[0] = x_ref[0] + _dot_tn(o_t, wout_ref[...])


def _mla_attn_call(q, kfull, vtfull, kdiag, vtdiag, x, wuvt, wout, *, tq, tk, scale):
    bsz, heads, seq, _ = q.shape
    d = x.shape[-1]
    lfull = kfull.shape[1]
    nq = seq // tq
    assert seq % tq == 0 and lfull % tk == 0
    assert vtfull.shape == (bsz, lfull // tk, MLA_KV_LORA, tk) and vtdiag.shape == (bsz, nq, MLA_KV_LORA, tq)
    if nq == 1:
        past_tiles = lfull // tk
    else:
        assert tq == tk and lfull == seq
        past_tiles = 0
    cb = max(tq, min(heads * tq, ATTN_COL_BLOCK))
    assert cb % tq == 0 and (heads * tq) % cb == 0
    v_dim = wuvt.shape[1]
    return pl.pallas_call(
        functools.partial(_mla_attn_kernel, tq=tq, tk=tk, cb=cb, past_tiles=past_tiles, scale=scale),
        out_shape=jax.ShapeDtypeStruct((bsz, seq, d), F32),
        grid=(bsz, nq),
        in_specs=[
            pl.BlockSpec((1, heads, tq, KCAT), lambda b, i: (b, 0, i, 0)),
            pl.BlockSpec((1, lfull, KCAT), lambda b, i: (b, 0, 0)),
            pl.BlockSpec((1, lfull // tk, MLA_KV_LORA, tk), lambda b, i: (b, 0, 0, 0)),
            pl.BlockSpec((1, tq, KCAT), lambda b, i: (b, i, 0)),
            pl.BlockSpec((1, 1, MLA_KV_LORA, tq), lambda b, i: (b, i, 0, 0)),
            pl.BlockSpec((1, tq, d), lambda b, i: (b, i, 0)),
            _resident(wuvt.shape),
            _resident(wout.shape),
        ],
        out_specs=pl.BlockSpec((1, tq, d), lambda b, i: (b, i, 0)),
        scratch_shapes=[
            pltpu.VMEM((heads, v_dim, tq), BF16),
            pltpu.VMEM((1, heads * tq), F32),
            pltpu.VMEM((1, heads * tq), F32),
            pltpu.VMEM((MLA_KV_LORA, heads * tq), F32),
        ],
        compiler_params=pltpu.CompilerParams(
            dimension_semantics=("parallel", "parallel"), vmem_limit_bytes=VMEM_LIMIT),
        name="mla_attn",
    )(q, kfull, vtfull, kdiag, vtdiag, x, wuvt, wout)


def _mha_attn_kernel(q_ref, k_ref, vt_ref, kdiag_ref, vtdiag_ref, ot_ref, m_s, acc_s, *, tq):
    qb = pl.program_id(2)
    hp = q_ref.shape[1]
    visible = _block_causal(tq, tq, 0)

    def diag_scores(i):
        return jnp.where(visible, _dot_nt(kdiag_ref[0, i], q_ref[0, i]), -jnp.inf)

    def tile_scores(j, i):
        return _dot_nt(k_ref[0, i, pl.ds(pl.multiple_of(j * tq, tq), tq), :], q_ref[0, i])

    _attend(qb, hp, tq, diag_scores, lambda i: vtdiag_ref[0, i, 0],
            tile_scores, lambda j, i: vt_ref[0, i, j], m_s, None, acc_s, 1.0)

    for i in range(hp):
        cs = slice(i * tq, (i + 1) * tq)
        inv_l = 1.0 / acc_s[MLA_V:MLA_V + 1, cs]
        ot_ref[0, i * MLA_V:(i + 1) * MLA_V, :] = (acc_s[:MLA_V, cs] * inv_l).astype(BF16)


def _mha_attn_call(q, k, vt, *, hp, tq):
    bsz, heads, seq, qk_dim = q.shape
    nq = seq // tq
    v_rows = MLA_V + V_ONES_ROWS
    assert seq % tq == 0 and heads % hp == 0 and vt.shape == (bsz, heads, nq, v_rows, tq)
    once = pl.Buffered(1)
    return pl.pallas_call(
        functools.partial(_mha_attn_kernel, tq=tq),
        out_shape=jax.ShapeDtypeStruct((bsz, heads * MLA_V, seq), BF16),
        grid=(bsz, heads // hp, nq),
        in_specs=[
            pl.BlockSpec((1, hp, tq, qk_dim), lambda b, g, i: (b, g, i, 0)),
            pl.BlockSpec((1, hp, seq, qk_dim), lambda b, g, i: (b, g, 0, 0), pipeline_mode=once),
            pl.BlockSpec((1, hp, nq, v_rows, tq), lambda b, g, i: (b, g, 0, 0, 0), pipeline_mode=once),
            pl.BlockSpec((1, hp, tq, qk_dim), lambda b, g, i: (b, g, i, 0)),
            pl.BlockSpec((1, hp, 1, v_rows, tq), lambda b, g, i: (b, g, i, 0, 0)),
        ],
        out_specs=pl.BlockSpec((1, hp * MLA_V, tq), lambda b, g, i: (b, g, i)),
        scratch_shapes=[
            pltpu.VMEM((1, hp * tq), F32),
            pltpu.VMEM((v_rows, hp * tq), F32),
        ],
        compiler_params=pltpu.CompilerParams(
            dimension_semantics=("parallel", "parallel", "arbitrary"), vmem_limit_bytes=VMEM_LIMIT),
        name="mha_attn",
    )(q, k, vt, k, vt)


def _rope_tables(pos):
    half = MLA_ROPE // 2
    inv = ROPE_THETA ** (-jnp.arange(half, dtype=F32) / half)
    ang = pos.astype(F32)[:, None] * inv[None, :]
    pad = jnp.zeros((pos.shape[0], LANES - MLA_ROPE), F32)
    cos = jnp.concatenate([jnp.cos(ang), jnp.cos(ang), pad], axis=-1)
    sin = jnp.concatenate([jnp.sin(ang), jnp.sin(ang), pad], axis=-1)
    return cos, sin


def _rotate_half_cols(w):
    half = w.shape[-1] // 2
    return jnp.concatenate([-w[..., half:], w[..., :half]], axis=-1)


def _pad_last(w, n):
    return jnp.concatenate([w, jnp.zeros(w.shape[:-1] + (n - w.shape[-1],), w.dtype)], axis=-1)


def _prep_gla(w_in, w_gk_up, b_gk, g_norm, w_out):
    qk = w_gk_up.shape[1]
    vw = w_out.shape[0]
    wq = w_in[:, :qk]
    wk = w_in[:, qk:2 * qk]
    wv = w_in[:, 2 * qk:2 * qk + vw]
    wg = w_in[:, 2 * qk + vw:2 * qk + 2 * vw]
    wlow = _pad_last(w_in[:, 2 * qk + 2 * vw:], LANES)
    rank = w_gk_up.shape[0]
    wgk = jnp.concatenate([w_gk_up, jnp.zeros((LANES - rank, qk), w_gk_up.dtype)], axis=0)
    return (wq.astype(BF16), wk.astype(BF16), wv.astype(BF16), wg.astype(BF16), wlow.astype(BF16),
            wgk.astype(BF16), b_gk[None, :], g_norm[None, :], w_out.astype(BF16))


def _prep_mla(w_down, q_norm, w_uq, kv_norm, w_uk, w_uv, w_out):
    wdq = w_down[:, :MLA_Q_LORA]
    wdkv = w_down[:, MLA_Q_LORA:MLA_Q_LORA + MLA_KV_LORA]
    wdk = w_down[:, MLA_Q_LORA + MLA_KV_LORA:]
    wdkr = _rotate_half_cols(wdk)
    wuq = w_uq.reshape(MLA_Q_LORA, MLA_HEADS, MLA_NOPE + MLA_ROPE)
    wnope = wuq[..., :MLA_NOPE].reshape(MLA_Q_LORA, MLA_HEADS * MLA_NOPE)
    wrope = wuq[..., MLA_NOPE:]
    wr = _pad_last(wrope, LANES).reshape(MLA_Q_LORA, MLA_HEADS * LANES)
    wrr = _pad_last(_rotate_half_cols(wrope), LANES).reshape(MLA_Q_LORA, MLA_HEADS * LANES)
    wuk = jnp.transpose(w_uk, (1, 2, 0))
    wuvt = jnp.transpose(w_uv, (1, 2, 0))
    wuk_flat = w_uk.reshape(MLA_KV_LORA, MLA_HEADS * MLA_NOPE)
    wuv_flat_t = w_uv.reshape(MLA_KV_LORA, MLA_HEADS * MLA_V).T
    return dict(
        wdq=wdq.astype(BF16), wdkv=wdkv.astype(BF16), wdk=_pad_last(wdk, LANES).astype(BF16),
        wdkr=_pad_last(wdkr, LANES).astype(BF16), qn=q_norm[None, :], kvn=kv_norm[None, :],
        wnope=wnope.astype(BF16), wr=wr.astype(BF16), wrr=wrr.astype(BF16), wuk=wuk.astype(BF16),
        wuvt=wuvt.astype(BF16), wuk_flat=wuk_flat.astype(BF16), wuv_flat_t=wuv_flat_t.astype(BF16),
        wout=w_out.astype(BF16))


def kernel(x_prompt, x_sample, state_gla, cache_ckv, cache_kpe, norm_ffn1, w_ffn1_gate, w_ffn1_up, w_ffn1_down, norm_mix, norm_ffn2, w_ffn2_gate, w_ffn2_up, w_ffn2_down, w_gla_in, w_gla_gk_up, b_gla_gk, gla_out_norm, w_gla_out, w_mla_down, mla_q_norm, w_mla_uq, mla_kv_norm, w_mla_uk, w_mla_uv, w_mla_out, norm_final):
    depth = norm_ffn1.shape[0]
    n_mixers = 2
    d = x_prompt.shape[-1]
    mla_scale = (MLA_NOPE + MLA_ROPE) ** -0.5
    gf = norm_final[None, :]

    ffn1 = [(norm_ffn1[i][None, :], w_ffn1_gate[i].astype(BF16), w_ffn1_up[i].astype(BF16),
             w_ffn1_down[i].astype(BF16)) for i in range(depth)]
    ffn2 = [(norm_ffn2[i][None, :], w_ffn2_gate[i].astype(BF16), w_ffn2_up[i].astype(BF16),
             w_ffn2_down[i].astype(BF16)) for i in range(depth)]
    gla_w = [_prep_gla(w_gla_in[j], w_gla_gk_up[j], b_gla_gk[j], gla_out_norm[j], w_gla_out[j])
             for j in range(w_gla_in.shape[0])]
    mla_w = [_prep_mla(w_mla_down[j], mla_q_norm[j], w_mla_uq[j], mla_kv_norm[j], w_mla_uk[j], w_mla_uv[j],
                       w_mla_out[j]) for j in range(w_mla_down.shape[0])]

    def run(x, pos, gla_s0, past_ckv, past_kpe, *, gla_nb, gla_tl, attn_tq):
        bsz, seq, _ = x.shape
        cos, sin = _rope_tables(pos)
        new_gla, new_ckv, new_kpe = [], [], []
        pending = {}

        def ffn(x, w, final_norm=False):
            return _ffn_call(x.reshape(bsz * seq, d), w[0], w[1], w[2], w[3], gf,
                             final_norm=final_norm, **pending).reshape(bsz, seq, d)

        for i in range(depth):
            x = ffn(x, ffn1[i])
            j = i // n_mixers
            gm = norm_mix[i][None, :]
            if i % n_mixers == 0:
                x, s_fin = _gla_call(x, gla_s0[j], gm, *gla_w[j], nb=gla_nb, tl=gla_tl)
                new_gla.append(s_fin)
            else:
                w = mla_w[j]
                absorb = past_ckv is not None
                q, k, vt, ckv, kpe = _mla_proj_call(
                    x, cos, sin, gm, w["wdq"], w["wdkv"], w["wdk"], w["wdkr"], w["qn"], w["kvn"],
                    w["wnope"], w["wr"], w["wrr"], w["wuk"] if absorb else w["wuk_flat"], w["wuv_flat_t"],
                    tm=attn_tq, absorb=absorb, q_scale=1.0 if absorb else mla_scale * LOG2E)
                if absorb:
                    pc = past_ckv[j].astype(BF16)
                    pad = jnp.zeros(past_kpe[j].shape[:-1] + (KCAT - MLA_KV_LORA - MLA_ROPE,), BF16)
                    kfull = jnp.concatenate([pc, past_kpe[j].astype(BF16), pad], axis=-1)
                    vtfull = jnp.swapaxes(pc.reshape(bsz, -1, ATTN_KEY_TILE, MLA_KV_LORA), 2, 3)
                    x = _mla_attn_call(q, kfull, vtfull, k, vt, x, w["wuvt"], w["wout"],
                                       tq=attn_tq, tk=ATTN_KEY_TILE, scale=mla_scale)
                else:
                    pending = dict(attn_t=_mha_attn_call(q, k, vt, hp=ATTN_HEAD_GROUP, tq=attn_tq), w_out=w["wout"])
                new_ckv.append(ckv)
                new_kpe.append(kpe)
            x = ffn(x, ffn2[i], final_norm=(i == depth - 1))
            pending = {}
        return x, jnp.stack(new_gla), jnp.stack(new_ckv), jnp.stack(new_kpe)

    bp, lp = x_prompt.shape[0], x_prompt.shape[1]
    n_gla = w_gla_in.shape[0]
    s0_p = jnp.zeros((n_gla, bp) + state_gla.shape[2:], x_prompt.dtype)
    y_p, gla_p, ckv_p, kpe_p = run(
        x_prompt, jnp.arange(lp), s0_p, None, None, gla_nb=1, gla_tl=512, attn_tq=ATTN_KEY_TILE)

    past_len = cache_ckv.shape[2]
    ls = x_sample.shape[1]
    y_s, gla_s, ckv_s, kpe_s = run(
        x_sample, past_len + jnp.arange(ls), state_gla, cache_ckv, cache_kpe, gla_nb=8, gla_tl=ls, attn_tq=ls)

    return (y_p, y_s, gla_p, ckv_p, kpe_p, gla_s, ckv_s, kpe_s)
```

```python
import functools

import jax
import jax.numpy as jnp
from jax import lax
from jax.experimental import pallas as pl
from jax.experimental.pallas import tpu as pltpu

F32 = jnp.float32
BF16 = jnp.bfloat16

EPS = 1e-6
CHUNK = 64
GLA_HEADS = 4
GLA_GATE_NORM = 16.0
GLA_ROW_GROUPS = 2
MLA_HEADS = 8
MLA_NOPE = 128
MLA_ROPE = 64
MLA_V = 128
MLA_Q_LORA = 384
MLA_KV_LORA = 256
ROPE_THETA = 10000.0
LANES = 128
BF16_SUBLANES = 16
LOG2E = 1.4426950408889634
ATTN_COL_BLOCK = 512
ATTN_KEY_TILE = 512
ATTN_HEAD_GROUP = 4
V_ONES_ROWS = 16
ATTN_LAG_LIMIT = 64.0
KCAT = MLA_KV_LORA + LANES

VMEM_LIMIT = 56 * 1024 * 1024


def _rms(xf, g):
    return xf * lax.rsqrt(jnp.mean(xf * xf, axis=-1, keepdims=True) + EPS) * g


def _silu(a):
    return a * (1.0 / (1.0 + jnp.exp(-a)))


def _dot(a, b):
    return jnp.dot(a, b, preferred_element_type=F32)


def _dot_nt(a, b):
    return lax.dot_general(a, b, (((1,), (1,)), ((), ())), preferred_element_type=F32)


def _dot_tn(a, b):
    return lax.dot_general(a, b, (((0,), (0,)), ((), ())), preferred_element_type=F32)


def _resident(shape):
    zeros = (0,) * len(shape)
    return pl.BlockSpec(shape, lambda *_: zeros, pipeline_mode=pl.Buffered(1))


def _ffn_kernel(*refs, f_chunk, final_norm, mixer_proj):
    if mixer_proj:
        x_ref, ot_ref, wo_ref, g_ref, wg_ref, wu_ref, wd_ref, gf_ref, o_ref, hid_ref = refs
        x = x_ref[...] + _dot_tn(ot_ref[0], wo_ref[...])
    else:
        x_ref, g_ref, wg_ref, wu_ref, wd_ref, gf_ref, o_ref, hid_ref = refs
        x = x_ref[...]
    h = _rms(x, g_ref[...]).astype(BF16)
    d_ff = wg_ref.shape[1]
    for c in range(d_ff // f_chunk):
        sl = slice(c * f_chunk, (c + 1) * f_chunk)
        a = _dot(h, wg_ref[:, sl].astype(BF16))
        b = _dot(h, wu_ref[:, sl].astype(BF16))
        hid_ref[:, sl] = (_silu(a) * b).astype(BF16)
    y = x + 0.5 * _dot(hid_ref[...], wd_ref[...].astype(BF16))
    if final_norm:
        y = _rms(y, gf_ref[...])
    o_ref[...] = y


def _ffn_call(x, g, wg, wu, wd, gf, *, final_norm, attn_t=None, w_out=None, tm=512, f_chunk=256):
    t, d = x.shape
    d_ff = wg.shape[1]
    assert t % tm == 0 and d_ff % f_chunk == 0
    mixer_proj = attn_t is not None
    x_spec = pl.BlockSpec((tm, d), lambda i: (i, 0))
    weights = (g, wg, wu, wd, gf)
    if mixer_proj:
        tiles_per_seq = attn_t.shape[2] // tm
        assert attn_t.shape[0] * attn_t.shape[2] == t and attn_t.shape[2] % tm == 0
        ot_spec = pl.BlockSpec((1, attn_t.shape[1], tm), lambda i: (i // tiles_per_seq, 0, i % tiles_per_seq))
        operands = (x, attn_t, w_out) + weights
        in_specs = [x_spec, ot_spec, _resident(w_out.shape)]
    else:
        operands = (x,) + weights
        in_specs = [x_spec]
    return pl.pallas_call(
        functools.partial(_ffn_kernel, f_chunk=f_chunk, final_norm=final_norm, mixer_proj=mixer_proj),
        out_shape=jax.ShapeDtypeStruct((t, d), F32),
        grid=(t // tm,),
        in_specs=in_specs + [_resident(w.shape) for w in weights],
        out_specs=pl.BlockSpec((tm, d), lambda i: (i, 0)),
        scratch_shapes=[pltpu.VMEM((tm, d_ff), BF16)],
        compiler_params=pltpu.CompilerParams(
            dimension_semantics=("parallel",), vmem_limit_bytes=VMEM_LIMIT),
        name="ffn",
    )(*operands)


def _split3(a):
    hi = a.astype(BF16)
    r = a - hi.astype(F32)
    mid = r.astype(BF16)
    lo = (r - mid.astype(F32)).astype(BF16)
    return hi, mid, lo


def _gla_kernel(x_ref, s0_ref, gmix_ref, wq_ref, wk_ref, wv_ref, wg_ref, wlow_ref, wgk_ref, bgk_ref,
                gout_ref, wout_ref, y_ref, s_ref, st_s, *, nb, tl, n_sub):
    l = pl.program_id(1)
    heads = GLA_HEADS
    dk = wq_ref.shape[1] // heads
    dv = wv_ref.shape[1] // heads
    rows = nb * tl
    rows_sub = rows // n_sub
    d = x_ref.shape[-1]

    @pl.when(l == 0)
    def _():
        for b in range(nb):
            for hd in range(heads):
                st_s[b, hd] = s0_ref[b, hd].T

    ti = lax.broadcasted_iota(jnp.int32, (CHUNK, CHUNK), 0)
    si = lax.broadcasted_iota(jnp.int32, (CHUNK, CHUNK), 1)
    causal = si <= ti
    tri = jnp.where(causal, 1.0, 0.0).astype(BF16)

    for sub in range(n_sub):
        row0 = sub * rows_sub
        if rows_sub >= tl:
            seqs = slice(row0 // tl, (row0 + rows_sub) // tl)
            x = x_ref[seqs].reshape(rows_sub, d)
        else:
            x = x_ref[row0 // tl, row0 % tl:row0 % tl + rows_sub, :]
        h = _rms(x, gmix_ref[...]).astype(BF16)
        q = _dot(h, wq_ref[...]) * (dk ** -0.5)
        k = _dot(h, wk_ref[...])
        v = _dot(h, wv_ref[...]).astype(BF16)
        g = _dot(h, wg_ref[...])
        low = _dot(h, wlow_ref[...]).astype(BF16)
        z = _dot(low, wgk_ref[...]) + bgk_ref[...]
        log_a = -(jnp.maximum(-z, 0.0) + jnp.log1p(jnp.exp(-jnp.abs(z)))) / GLA_GATE_NORM

        n_chunk = rows_sub // CHUNK
        q_decs, decs, o_intra, kv_t = [], [], [], []
        for c in range(n_chunk):
            rs = slice(c * CHUNK, (c + 1) * CHUNK)
            a_hi, a_mid, a_lo = _split3(log_a[rs, :])
            cum = _dot(tri, a_hi) + _dot(tri, a_mid) + _dot(tri, a_lo)
            cum_last = cum[CHUNK - 1:CHUNK, :]
            q_dec = (q[rs, :] * jnp.exp(cum)).astype(BF16)
            k_inv = (k[rs, :] * jnp.exp(-cum)).astype(BF16)
            k_tail = (k[rs, :] * jnp.exp(cum_last - cum)).astype(BF16)
            q_decs.append(q_dec)
            decs.append(jnp.exp(cum_last))
            oi, kv = [], []
            for hd in range(heads):
                ks = slice(hd * dk, (hd + 1) * dk)
                vv = v[rs, hd * dv:(hd + 1) * dv]
                scores = jnp.where(causal, _dot_nt(q_dec[:, ks], k_inv[:, ks]), 0.0).astype(BF16)
                oi.append(_dot(scores, vv))
                kv.append(_dot_tn(vv, k_tail[:, ks]))
            o_intra.append(oi)
            kv_t.append(kv)

        o_chunks = []
        state = {}
        for c in range(n_chunk):
            b = (row0 + c * CHUNK) // tl
            o_heads = []
            for hd in range(heads):
                ks = slice(hd * dk, (hd + 1) * dk)
                st = state[(b, hd)] if (b, hd) in state else st_s[b, hd]
                o_heads.append(o_intra[c][hd] + _dot_nt(q_decs[c][:, ks], st.astype(BF16)))
                state[(b, hd)] = st * decs[c][:, ks] + kv_t[c][hd]
            o_chunks.append(o_heads)
        for (b, hd), st in state.items():
            st_s[b, hd] = st

        gated = []
        for hd in range(heads):
            o_h = jnp.concatenate([oc[hd] for oc in o_chunks], axis=0)
            gated.append((_rms(o_h, gout_ref[...]) * _silu(g[:, hd * dv:(hd + 1) * dv])).astype(BF16))
        y = x + _dot(jnp.concatenate(gated, axis=-1), wout_ref[...])
        if rows_sub >= tl:
            y_ref[seqs] = y.reshape(rows_sub // tl, tl, d)
        else:
            y_ref[row0 // tl, row0 % tl:row0 % tl + rows_sub, :] = y

    @pl.when(l == pl.num_programs(1) - 1)
    def _():
        for b in range(nb):
            for hd in range(heads):
                s_ref[b, hd] = st_s[b, hd].T


def _gla_call(x, s0, gmix, wq, wk, wv, wg, wlow, wgk, bgk, gout, wout, *, nb, tl, n_sub=GLA_ROW_GROUPS):
    bsz, seq, d = x.shape
    heads = GLA_HEADS
    dk = wq.shape[1] // heads
    dv = wv.shape[1] // heads
    rows_sub = nb * tl // n_sub
    assert bsz % nb == 0 and seq % tl == 0 and tl % CHUNK == 0 and (nb * tl) % n_sub == 0
    assert rows_sub % CHUNK == 0 and (rows_sub % tl == 0 or tl % rows_sub == 0)
    return pl.pallas_call(
        functools.partial(_gla_kernel, nb=nb, tl=tl, n_sub=n_sub),
        out_shape=(jax.ShapeDtypeStruct((bsz, seq, d), F32),
                   jax.ShapeDtypeStruct((bsz, heads, dk, dv), F32)),
        grid=(bsz // nb, seq // tl),
        in_specs=[
            pl.BlockSpec((nb, tl, d), lambda b, l: (b, l, 0)),
            pl.BlockSpec((nb, heads, dk, dv), lambda b, l: (b, 0, 0, 0)),
            _resident(gmix.shape), _resident(wq.shape), _resident(wk.shape), _resident(wv.shape),
            _resident(wg.shape), _resident(wlow.shape), _resident(wgk.shape), _resident(bgk.shape),
            _resident(gout.shape), _resident(wout.shape),
        ],
        out_specs=(pl.BlockSpec((nb, tl, d), lambda b, l: (b, l, 0)),
                   pl.BlockSpec((nb, heads, dk, dv), lambda b, l: (b, 0, 0, 0))),
        scratch_shapes=[pltpu.VMEM((nb, heads, dv, dk), F32)],
        compiler_params=pltpu.CompilerParams(
            dimension_semantics=("parallel", "arbitrary"), vmem_limit_bytes=VMEM_LIMIT),
        name="gla",
    )(x, s0, gmix, wq, wk, wv, wg, wlow, wgk, bgk, gout, wout)


def _mla_proj_kernel(x_ref, cos_ref, sin_ref, gmix_ref, wdq_ref, wdkv_ref, wdk_ref, wdkr_ref, qn_ref, kvn_ref,
                     wnope_ref, wr_ref, wrr_ref, wkey_ref, wval_ref,
                     q_ref, k_ref, vt_ref, ckv_ref, kpe_ref, *, absorb, q_scale):
    x = x_ref[0]
    h = _rms(x, gmix_ref[...]).astype(BF16)
    cos = cos_ref[...]
    sin = sin_ref[...]

    ckv = _rms(_dot(h, wdkv_ref[...]), kvn_ref[...])
    kpe = _dot(h, wdk_ref[...]) * cos + _dot(h, wdkr_ref[...]) * sin
    ckv_ref[0] = ckv
    kpe_ref[0] = kpe[:, :MLA_ROPE]
    ckv_b = ckv.astype(BF16)
    kpe_b = kpe.astype(BF16)
    ckv_t = ckv.T.astype(BF16)

    hq = _rms(_dot(h, wdq_ref[...]), qn_ref[...]).astype(BF16)
    q_nope = _dot(hq, wnope_ref[...])
    q_r = _dot(hq, wr_ref[...])
    q_rr = _dot(hq, wrr_ref[...])

    if absorb:
        k_ref[0, :, :MLA_KV_LORA] = ckv_b
        k_ref[0, :, MLA_KV_LORA:] = kpe_b
        vt_ref[0, 0] = ckv_t
    else:
        k_nope = _dot(ckv_b, wkey_ref[...]).astype(BF16)
        v_t = _dot(wval_ref[...], ckv_t).astype(BF16)
        ones = jnp.ones((V_ONES_ROWS, x.shape[0]), BF16)
    for hd in range(MLA_HEADS):
        hs = slice(hd * LANES, (hd + 1) * LANES)
        q_pe = q_r[:, hs] * cos + q_rr[:, hs] * sin
        if absorb:
            q_ref[0, hd, :, :MLA_KV_LORA] = _dot(q_nope[:, hs].astype(BF16), wkey_ref[hd]).astype(BF16)
            q_ref[0, hd, :, MLA_KV_LORA:] = q_pe.astype(BF16)
        else:
            q_ref[0, hd, :, :MLA_NOPE] = (q_nope[:, hs] * q_scale).astype(BF16)
            q_ref[0, hd, :, MLA_NOPE:] = (q_pe * q_scale).astype(BF16)
            k_ref[0, hd, :, :MLA_NOPE] = k_nope[:, hs]
            k_ref[0, hd, :, MLA_NOPE:] = kpe_b
            vt_ref[0, hd, 0, :MLA_V] = v_t[hd * MLA_V:(hd + 1) * MLA_V, :]
            vt_ref[0, hd, 0, MLA_V:] = ones


def _mla_proj_call(x, cos, sin, gmix, wdq, wdkv, wdk, wdkr, qn, kvn, wnope, wr, wrr, wkey, wval, *, tm, absorb,
                   q_scale=1.0):
    bsz, seq, d = x.shape
    assert seq % tm == 0
    weights = (gmix, wdq, wdkv, wdk, wdkr, qn, kvn, wnope, wr, wrr, wkey, wval)
    nt = seq // tm
    if absorb:
        qk_dim = KCAT
        q_k_vt = (jax.ShapeDtypeStruct((bsz, MLA_HEADS, seq, qk_dim), BF16),
                  jax.ShapeDtypeStruct((bsz, seq, qk_dim), BF16),
                  jax.ShapeDtypeStruct((bsz, nt, MLA_KV_LORA, tm), BF16))
        q_k_vt_specs = (pl.BlockSpec((1, MLA_HEADS, tm, qk_dim), lambda b, l: (b, 0, l, 0)),
                        pl.BlockSpec((1, tm, qk_dim), lambda b, l: (b, l, 0)),
                        pl.BlockSpec((1, 1, MLA_KV_LORA, tm), lambda b, l: (b, l, 0, 0)))
    else:
        qk_dim = MLA_NOPE + LANES
        q_k_vt = (jax.ShapeDtypeStruct((bsz, MLA_HEADS, seq, qk_dim), BF16),
                  jax.ShapeDtypeStruct((bsz, MLA_HEADS, seq, qk_dim), BF16),
                  jax.ShapeDtypeStruct((bsz, MLA_HEADS, nt, MLA_V + V_ONES_ROWS, tm), BF16))
        q_k_vt_specs = (pl.BlockSpec((1, MLA_HEADS, tm, qk_dim), lambda b, l: (b, 0, l, 0)),
                        pl.BlockSpec((1, MLA_HEADS, tm, qk_dim), lambda b, l: (b, 0, l, 0)),
                        pl.BlockSpec((1, MLA_HEADS, 1, MLA_V + V_ONES_ROWS, tm), lambda b, l: (b, 0, l, 0, 0)))
    return pl.pallas_call(
        functools.partial(_mla_proj_kernel, absorb=absorb, q_scale=q_scale),
        out_shape=q_k_vt + (jax.ShapeDtypeStruct((bsz, seq, MLA_KV_LORA), F32),
                            jax.ShapeDtypeStruct((bsz, seq, MLA_ROPE), F32)),
        grid=(bsz, nt),
        in_specs=[
            pl.BlockSpec((1, tm, d), lambda b, l: (b, l, 0)),
            pl.BlockSpec((tm, LANES), lambda b, l: (l, 0)),
            pl.BlockSpec((tm, LANES), lambda b, l: (l, 0)),
        ] + [_resident(w.shape) for w in weights],
        out_specs=q_k_vt_specs + (pl.BlockSpec((1, tm, MLA_KV_LORA), lambda b, l: (b, l, 0)),
                                  pl.BlockSpec((1, tm, MLA_ROPE), lambda b, l: (b, l, 0))),
        compiler_params=pltpu.CompilerParams(
            dimension_semantics=("parallel", "parallel"), vmem_limit_bytes=VMEM_LIMIT),
        name="mla_proj",
    )(x, cos, sin, *weights)


def _exp2_scaled(t, c):
    return jnp.exp2(t if c == 1.0 else t * c)


def _tile_update(s, vt, cs, m_s, l_s, acc_s, lag_s, c, one_pass):
    m_prev = m_s[:, cs]
    mx = jnp.max(s, axis=0, keepdims=True)
    m_new = jnp.maximum(m_prev, mx)
    alpha = _exp2_scaled(m_prev - m_new, c)
    if one_pass:
        p = _exp2_scaled(s - m_prev, c)
        if l_s is not None:
            l_s[:, cs] = (l_s[:, cs] + jnp.sum(p, axis=0, keepdims=True)) * alpha
        acc_s[:, cs] = (acc_s[:, cs] + _dot(vt, p.astype(BF16))) * alpha
        lag_s[...] = jnp.maximum(lag_s[...], (mx - m_prev) * c)
    else:
        p = _exp2_scaled(s - m_new, c)
        if l_s is not None:
            l_s[:, cs] = alpha * l_s[:, cs] + jnp.sum(p, axis=0, keepdims=True)
        acc_s[:, cs] = acc_s[:, cs] * alpha + _dot(vt, p.astype(BF16))
    m_s[:, cs] = m_new


def _attend(n_full, n_blk, cb, first_key_scores, diag_scores, diag_values, tile_scores, tile_values,
            m_s, l_s, acc_s, lag_s, c):
    def cols(i):
        return slice(i * cb, (i + 1) * cb)

    def run(one_pass):
        lag_s[...] = jnp.zeros(lag_s.shape, F32)
        for i in range(n_blk):
            m_s[:, cols(i)] = first_key_scores(i)
            acc_s[:, cols(i)] = jnp.zeros((acc_s.shape[0], cb), F32)
            if l_s is not None:
                l_s[:, cols(i)] = jnp.zeros((1, cb), F32)

        def update(units):
            s_next = units[0][0]()
            for u, (_, values, i) in enumerate(units):
                s = s_next
                if u + 1 < len(units):
                    s_next = units[u + 1][0]()
                _tile_update(s, values(), cols(i), m_s, l_s, acc_s, lag_s, c, one_pass)

        def full_tiles(js):
            return [(functools.partial(tile_scores, j, i), functools.partial(tile_values, j, i), i)
                    for j in js for i in range(n_blk)]

        update([(functools.partial(diag_scores, i), functools.partial(diag_values, i), i) for i in range(n_blk)])

        def pair(jj, carry):
            update(full_tiles((2 * jj, 2 * jj + 1)))
            return carry

        lax.fori_loop(0, n_full // 2, pair, 0)
        if isinstance(n_full, int):
            if n_full % 2:
                update(full_tiles((n_full - 1,)))
        else:
            @pl.when(n_full % 2 == 1)
            def _():
                update(full_tiles((n_full - 1,)))

    run(one_pass=True)

    @pl.when(jnp.max(lag_s[...]) > ATTN_LAG_LIMIT)
    def _():
        run(one_pass=False)


def _block_causal(tq, cb, col0):
    ki = lax.broadcasted_iota(jnp.int32, (tq, cb), 0) // CHUNK
    qi = ((lax.broadcasted_iota(jnp.int32, (tq, cb), 1) + col0) % tq) // CHUNK
    return ki <= qi


def _mla_attn_kernel(q_ref, kfull_ref, vtfull_ref, kdiag_ref, vtdiag_ref, x_ref, wuvt_ref, wout_ref, y_ref,
                     o_s, m_s, l_s, acc_s, lag_s, *, tq, tk, cb, past_tiles, single_block, scale):
    qb = pl.program_id(1)
    heads = q_ref.shape[1]
    n_blk = heads * tq // cb
    heads_per_blk = cb // tq

    def q_blk(i):
        return q_ref[0, i * heads_per_blk:(i + 1) * heads_per_blk].reshape(cb, KCAT)

    def diag_scores(i):
        return jnp.where(_block_causal(tq, cb, i * cb), _dot_nt(kdiag_ref[0], q_blk(i)), -jnp.inf)

    def tile_scores(j, i):
        return _dot_nt(kfull_ref[0, pl.ds(pl.multiple_of(j * tk, tk), tk), :], q_blk(i))

    def first_key_scores(i):
        return _dot_nt(kdiag_ref[0, :BF16_SUBLANES, :], q_blk(i))[:1]

    n_full = past_tiles if single_block else past_tiles + qb * (tq // tk)
    _attend(n_full, n_blk, cb, first_key_scores, diag_scores, lambda i: vtdiag_ref[0, 0],
            tile_scores, lambda j, i: vtfull_ref[0, j], m_s, l_s, acc_s, lag_s, scale * LOG2E)

    for hd in range(heads):
        cs = slice(hd * tq, (hd + 1) * tq)
        o_lat_t = (acc_s[:, cs] * (1.0 / l_s[:, cs])).astype(BF16)
        o_s[hd] = _dot(wuvt_ref[hd], o_lat_t).astype(BF16)
    o_t = o_s[...].reshape(o_s.shape[0] * o_s.shape[1], tq)
    y_ref[0] = x_ref[0] + _dot_tn(o_t, wout_ref[...])


def _mla_attn_call(q, kfull, vtfull, kdiag, vtdiag, x, wuvt, wout, *, tq, tk, scale):
    bsz, heads, seq, _ = q.shape
    d = x.shape[-1]
    lfull = kfull.shape[1]
    nq = seq // tq
    assert seq % tq == 0 and lfull % tk == 0
    assert vtfull.shape == (bsz, lfull // tk, MLA_KV_LORA, tk) and vtdiag.shape == (bsz, nq, MLA_KV_LORA, tq)
    if nq == 1:
        past_tiles = lfull // tk
    else:
        assert tq == tk and lfull == seq
        past_tiles = 0
    cb = max(tq, min(heads * tq, ATTN_COL_BLOCK))
    assert cb % tq == 0 and (heads * tq) % cb == 0
    v_dim = wuvt.shape[1]
    return pl.pallas_call(
        functools.partial(_mla_attn_kernel, tq=tq, tk=tk, cb=cb, past_tiles=past_tiles, single_block=nq == 1,
                          scale=scale),
        out_shape=jax.ShapeDtypeStruct((bsz, seq, d), F32),
        grid=(bsz, nq),
        in_specs=[
            pl.BlockSpec((1, heads, tq, KCAT), lambda b, i: (b, 0, i, 0)),
            pl.BlockSpec((1, lfull, KCAT), lambda b, i: (b, 0, 0)),
            pl.BlockSpec((1, lfull // tk, MLA_KV_LORA, tk), lambda b, i: (b, 0, 0, 0)),
            pl.BlockSpec((1, tq, KCAT), lambda b, i: (b, i, 0)),
            pl.BlockSpec((1, 1, MLA_KV_LORA, tq), lambda b, i: (b, i, 0, 0)),
            pl.BlockSpec((1, tq, d), lambda b, i: (b, i, 0)),
            _resident(wuvt.shape),
            _resident(wout.shape),
        ],
        out_specs=pl.BlockSpec((1, tq, d), lambda b, i: (b, i, 0)),
        scratch_shapes=[
            pltpu.VMEM((heads, v_dim, tq), BF16),
            pltpu.VMEM((1, heads * tq), F32),
            pltpu.VMEM((1, heads * tq), F32),
            pltpu.VMEM((MLA_KV_LORA, heads * tq), F32),
            pltpu.VMEM((1, cb), F32),
        ],
        compiler_params=pltpu.CompilerParams(
            dimension_semantics=("parallel", "parallel"), vmem_limit_bytes=VMEM_LIMIT),
        name="mla_attn",
    )(q, kfull, vtfull, kdiag, vtdiag, x, wuvt, wout)


def _mha_attn_kernel(q_ref, k_ref, vt_ref, kdiag_ref, vtdiag_ref, ot_ref, m_s, acc_s, lag_s, *, tq):
    qb = pl.program_id(2)
    hp = q_ref.shape[1]
    visible = _block_causal(tq, tq, 0)

    def diag_scores(i):
        return jnp.where(visible, _dot_nt(kdiag_ref[0, i], q_ref[0, i]), -jnp.inf)

    def tile_scores(j, i):
        return _dot_nt(k_ref[0, i, pl.ds(pl.multiple_of(j * tq, tq), tq), :], q_ref[0, i])

    def first_key_scores(i):
        return _dot_nt(kdiag_ref[0, i, :BF16_SUBLANES, :], q_ref[0, i])[:1]

    _attend(qb, hp, tq, first_key_scores, diag_scores, lambda i: vtdiag_ref[0, i, 0],
            tile_scores, lambda j, i: vt_ref[0, i, j], m_s, None, acc_s, lag_s, 1.0)

    for i in range(hp):
        cs = slice(i * tq, (i + 1) * tq)
        inv_l = 1.0 / acc_s[MLA_V:MLA_V + 1, cs]
        ot_ref[0, i * MLA_V:(i + 1) * MLA_V, :] = (acc_s[:MLA_V, cs] * inv_l).astype(BF16)


def _mha_attn_call(q, k, vt, *, hp, tq):
    bsz, heads, seq, qk_dim = q.shape
    nq = seq // tq
    v_rows = MLA_V + V_ONES_ROWS
    assert seq % tq == 0 and heads % hp == 0 and vt.shape == (bsz, heads, nq, v_rows, tq)
    once = pl.Buffered(1)
    return pl.pallas_call(
        functools.partial(_mha_attn_kernel, tq=tq),
        out_shape=jax.ShapeDtypeStruct((bsz, heads * MLA_V, seq), BF16),
        grid=(bsz, heads // hp, nq),
        in_specs=[
            pl.BlockSpec((1, hp, tq, qk_dim), lambda b, g, i: (b, g, i, 0)),
            pl.BlockSpec((1, hp, seq, qk_dim), lambda b, g, i: (b, g, 0, 0), pipeline_mode=once),
            pl.BlockSpec((1, hp, nq, v_rows, tq), lambda b, g, i: (b, g, 0, 0, 0), pipeline_mode=once),
            pl.BlockSpec((1, hp, tq, qk_dim), lambda b, g, i: (b, g, i, 0)),
            pl.BlockSpec((1, hp, 1, v_rows, tq), lambda b, g, i: (b, g, i, 0, 0)),
        ],
        out_specs=pl.BlockSpec((1, hp * MLA_V, tq), lambda b, g, i: (b, g, i)),
        scratch_shapes=[
            pltpu.VMEM((1, hp * tq), F32),
            pltpu.VMEM((v_rows, hp * tq), F32),
            pltpu.VMEM((1, tq), F32),
        ],
        compiler_params=pltpu.CompilerParams(
            dimension_semantics=("parallel", "parallel", "arbitrary"), vmem_limit_bytes=VMEM_LIMIT),
        name="mha_attn",
    )(q, k, vt, k, vt)


def _rope_tables(pos):
    half = MLA_ROPE // 2
    inv = ROPE_THETA ** (-jnp.arange(half, dtype=F32) / half)
    ang = pos.astype(F32)[:, None] * inv[None, :]
    pad = jnp.zeros((pos.shape[0], LANES - MLA_ROPE), F32)
    cos = jnp.concatenate([jnp.cos(ang), jnp.cos(ang), pad], axis=-1)
    sin = jnp.concatenate([jnp.sin(ang), jnp.sin(ang), pad], axis=-1)
    return cos, sin


def _rotate_half_cols(w):
    half = w.shape[-1] // 2
    return jnp.concatenate([-w[..., half:], w[..., :half]], axis=-1)


def _pad_last(w, n):
    return jnp.concatenate([w, jnp.zeros(w.shape[:-1] + (n - w.shape[-1],), w.dtype)], axis=-1)


def _prep_gla(w_in, w_gk_up, b_gk, g_norm, w_out):
    qk = w_gk_up.shape[1]
    vw = w_out.shape[0]
    wq = w_in[:, :qk]
    wk = w_in[:, qk:2 * qk]
    wv = w_in[:, 2 * qk:2 * qk + vw]
    wg = w_in[:, 2 * qk + vw:2 * qk + 2 * vw]
    wlow = _pad_last(w_in[:, 2 * qk + 2 * vw:], LANES)
    rank = w_gk_up.shape[0]
    wgk = jnp.concatenate([w_gk_up, jnp.zeros((LANES - rank, qk), w_gk_up.dtype)], axis=0)
    return (wq.astype(BF16), wk.astype(BF16), wv.astype(BF16), wg.astype(BF16), wlow.astype(BF16),
            wgk.astype(BF16), b_gk[None, :], g_norm[None, :], w_out.astype(BF16))


def _prep_mla(w_down, q_norm, w_uq, kv_norm, w_uk, w_uv, w_out):
    wdq = w_down[:, :MLA_Q_LORA]
    wdkv = w_down[:, MLA_Q_LORA:MLA_Q_LORA + MLA_KV_LORA]
    wdk = w_down[:, MLA_Q_LORA + MLA_KV_LORA:]
    wdkr = _rotate_half_cols(wdk)
    wuq = w_uq.reshape(MLA_Q_LORA, MLA_HEADS, MLA_NOPE + MLA_ROPE)
    wnope = wuq[..., :MLA_NOPE].reshape(MLA_Q_LORA, MLA_HEADS * MLA_NOPE)
    wrope = wuq[..., MLA_NOPE:]
    wr = _pad_last(wrope, LANES).reshape(MLA_Q_LORA, MLA_HEADS * LANES)
    wrr = _pad_last(_rotate_half_cols(wrope), LANES).reshape(MLA_Q_LORA, MLA_HEADS * LANES)
    wuk = jnp.transpose(w_uk, (1, 2, 0))
    wuvt = jnp.transpose(w_uv, (1, 2, 0))
    wuk_flat = w_uk.reshape(MLA_KV_LORA, MLA_HEADS * MLA_NOPE)
    wuv_flat_t = w_uv.reshape(MLA_KV_LORA, MLA_HEADS * MLA_V).T
    return dict(
        wdq=wdq.astype(BF16), wdkv=wdkv.astype(BF16), wdk=_pad_last(wdk, LANES).astype(BF16),
        wdkr=_pad_last(wdkr, LANES).astype(BF16), qn=q_norm[None, :], kvn=kv_norm[None, :],
        wnope=wnope.astype(BF16), wr=wr.astype(BF16), wrr=wrr.astype(BF16), wuk=wuk.astype(BF16),
        wuvt=wuvt.astype(BF16), wuk_flat=wuk_flat.astype(BF16), wuv_flat_t=wuv_flat_t.astype(BF16),
        wout=w_out.astype(BF16))


def kernel(x_prompt, x_sample, state_gla, cache_ckv, cache_kpe, norm_ffn1, w_ffn1_gate, w_ffn1_up, w_ffn1_down, norm_mix, norm_ffn2, w_ffn2_gate, w_ffn2_up, w_ffn2_down, w_gla_in, w_gla_gk_up, b_gla_gk, gla_out_norm, w_gla_out, w_mla_down, mla_q_norm, w_mla_uq, mla_kv_norm, w_mla_uk, w_mla_uv, w_mla_out, norm_final):
    depth = norm_ffn1.shape[0]
    n_mixers = 2
    d = x_prompt.shape[-1]
    mla_scale = (MLA_NOPE + MLA_ROPE) ** -0.5
    gf = norm_final[None, :]

    ffn1 = [(norm_ffn1[i][None, :], w_ffn1_gate[i], w_ffn1_up[i], w_ffn1_down[i]) for i in range(depth)]
    ffn2 = [(norm_ffn2[i][None, :], w_ffn2_gate[i], w_ffn2_up[i], w_ffn2_down[i]) for i in range(depth)]
    gla_w = [_prep_gla(w_gla_in[j], w_gla_gk_up[j], b_gla_gk[j], gla_out_norm[j], w_gla_out[j])
             for j in range(w_gla_in.shape[0])]
    mla_w = [_prep_mla(w_mla_down[j], mla_q_norm[j], w_mla_uq[j], mla_kv_norm[j], w_mla_uk[j], w_mla_uv[j],
                       w_mla_out[j]) for j in range(w_mla_down.shape[0])]

    def run(x, pos, gla_s0, past_ckv, past_kpe, *, gla_nb, gla_tl, attn_tq):
        bsz, seq, _ = x.shape
        cos, sin = _rope_tables(pos)
        new_gla, new_ckv, new_kpe = [], [], []
        pending = {}

        def ffn(x, w, final_norm=False):
            return _ffn_call(x.reshape(bsz * seq, d), w[0], w[1], w[2], w[3], gf,
                             final_norm=final_norm, **pending).reshape(bsz, seq, d)

        for i in range(depth):
            x = ffn(x, ffn1[i])
            j = i // n_mixers
            gm = norm_mix[i][None, :]
            if i % n_mixers == 0:
                x, s_fin = _gla_call(x, gla_s0[j], gm, *gla_w[j], nb=gla_nb, tl=gla_tl)
                new_gla.append(s_fin)
            else:
                w = mla_w[j]
                absorb = past_ckv is not None
                q, k, vt, ckv, kpe = _mla_proj_call(
                    x, cos, sin, gm, w["wdq"], w["wdkv"], w["wdk"], w["wdkr"], w["qn"], w["kvn"],
                    w["wnope"], w["wr"], w["wrr"], w["wuk"] if absorb else w["wuk_flat"], w["wuv_flat_t"],
                    tm=attn_tq, absorb=absorb, q_scale=1.0 if absorb else mla_scale * LOG2E)
                if absorb:
                    pc = past_ckv[j].astype(BF16)
                    pad = jnp.zeros(past_kpe[j].shape[:-1] + (KCAT - MLA_KV_LORA - MLA_ROPE,), BF16)
                    kfull = jnp.concatenate([pc, past_kpe[j].astype(BF16), pad], axis=-1)
                    vtfull = jnp.swapaxes(pc.reshape(bsz, -1, ATTN_KEY_TILE, MLA_KV_LORA), 2, 3)
                    x = _mla_attn_call(q, kfull, vtfull, k, vt, x, w["wuvt"], w["wout"],
                                       tq=attn_tq, tk=ATTN_KEY_TILE, scale=mla_scale)
                else:
                    pending = dict(attn_t=_mha_attn_call(q, k, vt, hp=ATTN_HEAD_GROUP, tq=attn_tq), w_out=w["wout"])
                new_ckv.append(ckv)
                new_kpe.append(kpe)
            x = ffn(x, ffn2[i], final_norm=(i == depth - 1))
            pending = {}
        return x, jnp.stack(new_gla), jnp.stack(new_ckv), jnp.stack(new_kpe)

    bp, lp = x_prompt.shape[0], x_prompt.shape[1]
    n_gla = w_gla_in.shape[0]
    s0_p = jnp.zeros((n_gla, bp) + state_gla.shape[2:], x_prompt.dtype)
    y_p, gla_p, ckv_p, kpe_p = run(
        x_prompt, jnp.arange(lp), s0_p, None, None, gla_nb=1, gla_tl=512, attn_tq=ATTN_KEY_TILE)

    past_len = cache_ckv.shape[2]
    ls = x_sample.shape[1]
    y_s, gla_s, ckv_s, kpe_s = run(
        x_sample, past_len + jnp.arange(ls), state_gla, cache_ckv, cache_kpe, gla_nb=8, gla_tl=ls, attn_tq=ls)

    return (y_p, y_s, gla_p, ckv_p, kpe_p, gla_s, ckv_s, kpe_s)
```

```python
import functools

import jax
import jax.numpy as jnp
from jax import lax
from jax.experimental import pallas as pl
from jax.experimental.pallas import tpu as pltpu

F32 = jnp.float32
BF16 = jnp.bfloat16

EPS = 1e-6
CHUNK = 64
GLA_HEADS = 4
GLA_GATE_NORM = 16.0
GLA_ROW_GROUPS = 2
MLA_HEADS = 8
MLA_NOPE = 128
MLA_ROPE = 64
MLA_V = 128
MLA_Q_LORA = 384
MLA_KV_LORA = 256
ROPE_THETA = 10000.0
LANES = 128
BF16_SUBLANES = 16
LOG2E = 1.4426950408889634
ATTN_COL_BLOCK = 512
ATTN_KEY_TILE = 512
ATTN_HEAD_GROUP = 2
V_ONES_ROWS = 16
ATTN_LAG_LIMIT = 64.0
KCAT = MLA_KV_LORA + LANES

VMEM_LIMIT = 56 * 1024 * 1024


def _rms(xf, g):
    return xf * lax.rsqrt(jnp.mean(xf * xf, axis=-1, keepdims=True) + EPS) * g


def _silu(a):
    return a * (1.0 / (1.0 + jnp.exp(-a)))


def _dot(a, b):
    return jnp.dot(a, b, preferred_element_type=F32)


def _dot_nt(a, b):
    return lax.dot_general(a, b, (((1,), (1,)), ((), ())), preferred_element_type=F32)


def _dot_tn(a, b):
    return lax.dot_general(a, b, (((0,), (0,)), ((), ())), preferred_element_type=F32)


def _resident(shape):
    zeros = (0,) * len(shape)
    return pl.BlockSpec(shape, lambda *_: zeros, pipeline_mode=pl.Buffered(1))


def _ffn_kernel(*refs, f_chunk, final_norm, mixer_proj):
    if mixer_proj:
        x_ref, ot_ref, wo_ref, g_ref, wg_ref, wu_ref, wd_ref, gf_ref, o_ref, hid_ref = refs
        x = x_ref[...] + _dot_tn(ot_ref[0], wo_ref[...])
    else:
        x_ref, g_ref, wg_ref, wu_ref, wd_ref, gf_ref, o_ref, hid_ref = refs
        x = x_ref[...]
    h = _rms(x, g_ref[0]).astype(BF16)
    d_ff = wg_ref.shape[2]
    for c in range(d_ff // f_chunk):
        sl = slice(c * f_chunk, (c + 1) * f_chunk)
        a = _dot(h, wg_ref[0, :, sl].astype(BF16))
        b = _dot(h, wu_ref[0, :, sl].astype(BF16))
        hid_ref[:, sl] = (_silu(a) * b).astype(BF16)
    y = x + 0.5 * _dot(hid_ref[...], wd_ref[0].astype(BF16))
    if final_norm:
        y = _rms(y, gf_ref[...])
    o_ref[...] = y


def _layer_resident(shape, layer):
    index = (layer,) + (0,) * (len(shape) - 1)
    return pl.BlockSpec((1,) + tuple(shape[1:]), lambda *_: index, pipeline_mode=pl.Buffered(1))


def _ffn_call(x, g, wg, wu, wd, gf, layer, *, final_norm, attn_t=None, w_out=None, tm=512, f_chunk=256):
    t, d = x.shape
    d_ff = wg.shape[2]
    assert t % tm == 0 and d_ff % f_chunk == 0
    mixer_proj = attn_t is not None
    x_spec = pl.BlockSpec((tm, d), lambda i: (i, 0))
    weights = (g, wg, wu, wd)
    if mixer_proj:
        tiles_per_seq = attn_t.shape[2] // tm
        assert attn_t.shape[0] * attn_t.shape[2] == t and attn_t.shape[2] % tm == 0
        ot_spec = pl.BlockSpec((1, attn_t.shape[1], tm), lambda i: (i // tiles_per_seq, 0, i % tiles_per_seq))
        operands = (x, attn_t, w_out) + weights + (gf,)
        in_specs = [x_spec, ot_spec, _resident(w_out.shape)]
    else:
        operands = (x,) + weights + (gf,)
        in_specs = [x_spec]
    return pl.pallas_call(
        functools.partial(_ffn_kernel, f_chunk=f_chunk, final_norm=final_norm, mixer_proj=mixer_proj),
        out_shape=jax.ShapeDtypeStruct((t, d), F32),
        grid=(t // tm,),
        in_specs=in_specs + [_layer_resident(w.shape, layer) for w in weights] + [_resident(gf.shape)],
        out_specs=pl.BlockSpec((tm, d), lambda i: (i, 0)),
        scratch_shapes=[pltpu.VMEM((tm, d_ff), BF16)],
        compiler_params=pltpu.CompilerParams(
            dimension_semantics=("parallel",), vmem_limit_bytes=VMEM_LIMIT),
        name="ffn",
    )(*operands)


def _split3(a):
    hi = a.astype(BF16)
    r = a - hi.astype(F32)
    mid = r.astype(BF16)
    lo = (r - mid.astype(F32)).astype(BF16)
    return hi, mid, lo


def _gla_kernel(x_ref, s0_ref, gmix_ref, wq_ref, wk_ref, wv_ref, wg_ref, wlow_ref, wgk_ref, bgk_ref,
                gout_ref, wout_ref, y_ref, s_ref, st_s, *, nb, tl, n_sub):
    l = pl.program_id(1)
    heads = GLA_HEADS
    dk = wq_ref.shape[1] // heads
    dv = wv_ref.shape[1] // heads
    rows = nb * tl
    rows_sub = rows // n_sub
    d = x_ref.shape[-1]

    @pl.when(l == 0)
    def _():
        for b in range(nb):
            for hd in range(heads):
                st_s[b, hd] = s0_ref[b, hd].T

    ti = lax.broadcasted_iota(jnp.int32, (CHUNK, CHUNK), 0)
    si = lax.broadcasted_iota(jnp.int32, (CHUNK, CHUNK), 1)
    causal = si <= ti
    tri = jnp.where(causal, 1.0, 0.0).astype(BF16)

    for sub in range(n_sub):
        row0 = sub * rows_sub
        if rows_sub >= tl:
            seqs = slice(row0 // tl, (row0 + rows_sub) // tl)
            x = x_ref[seqs].reshape(rows_sub, d)
        else:
            x = x_ref[row0 // tl, row0 % tl:row0 % tl + rows_sub, :]
        h = _rms(x, gmix_ref[...]).astype(BF16)
        q = _dot(h, wq_ref[...]) * (dk ** -0.5)
        k = _dot(h, wk_ref[...])
        v = _dot(h, wv_ref[...]).astype(BF16)
        g = _dot(h, wg_ref[...])
        low = _dot(h, wlow_ref[...]).astype(BF16)
        z = _dot(low, wgk_ref[...]) + bgk_ref[...]
        log_a = -(jnp.maximum(-z, 0.0) + jnp.log1p(jnp.exp(-jnp.abs(z)))) / GLA_GATE_NORM

        n_chunk = rows_sub // CHUNK
        q_decs, decs, o_intra, kv_t = [], [], [], []
        for c in range(n_chunk):
            rs = slice(c * CHUNK, (c + 1) * CHUNK)
            a_hi, a_mid, a_lo = _split3(log_a[rs, :])
            cum = _dot(tri, a_hi) + _dot(tri, a_mid) + _dot(tri, a_lo)
            cum_last = cum[CHUNK - 1:CHUNK, :]
            q_dec = (q[rs, :] * jnp.exp(cum)).astype(BF16)
            k_inv = (k[rs, :] * jnp.exp(-cum)).astype(BF16)
            k_tail = (k[rs, :] * jnp.exp(cum_last - cum)).astype(BF16)
            q_decs.append(q_dec)
            decs.append(jnp.exp(cum_last))
            oi, kv = [], []
            for hd in range(heads):
                ks = slice(hd * dk, (hd + 1) * dk)
                vv = v[rs, hd * dv:(hd + 1) * dv]
                scores = jnp.where(causal, _dot_nt(q_dec[:, ks], k_inv[:, ks]), 0.0).astype(BF16)
                oi.append(_dot(scores, vv))
                kv.append(_dot_tn(vv, k_tail[:, ks]))
            o_intra.append(oi)
            kv_t.append(kv)

        o_chunks = []
        state = {}
        for c in range(n_chunk):
            b = (row0 + c * CHUNK) // tl
            o_heads = []
            for hd in range(heads):
                ks = slice(hd * dk, (hd + 1) * dk)
                st = state[(b, hd)] if (b, hd) in state else st_s[b, hd]
                o_heads.append(o_intra[c][hd] + _dot_nt(q_decs[c][:, ks], st.astype(BF16)))
                state[(b, hd)] = st * decs[c][:, ks] + kv_t[c][hd]
            o_chunks.append(o_heads)
        for (b, hd), st in state.items():
            st_s[b, hd] = st

        gated = []
        for hd in range(heads):
            o_h = jnp.concatenate([oc[hd] for oc in o_chunks], axis=0)
            gated.append((_rms(o_h, gout_ref[...]) * _silu(g[:, hd * dv:(hd + 1) * dv])).astype(BF16))
        y = x + _dot(jnp.concatenate(gated, axis=-1), wout_ref[...])
        if rows_sub >= tl:
            y_ref[seqs] = y.reshape(rows_sub // tl, tl, d)
        else:
            y_ref[row0 // tl, row0 % tl:row0 % tl + rows_sub, :] = y

    @pl.when(l == pl.num_programs(1) - 1)
    def _():
        for b in range(nb):
            for hd in range(heads):
                s_ref[b, hd] = st_s[b, hd].T


def _gla_call(x, s0, gmix, wq, wk, wv, wg, wlow, wgk, bgk, gout, wout, *, nb, tl, n_sub=GLA_ROW_GROUPS):
    bsz, seq, d = x.shape
    heads = GLA_HEADS
    dk = wq.shape[1] // heads
    dv = wv.shape[1] // heads
    rows_sub = nb * tl // n_sub
    assert bsz % nb == 0 and seq % tl == 0 and tl % CHUNK == 0 and (nb * tl) % n_sub == 0
    assert rows_sub % CHUNK == 0 and (rows_sub % tl == 0 or tl % rows_sub == 0)
    return pl.pallas_call(
        functools.partial(_gla_kernel, nb=nb, tl=tl, n_sub=n_sub),
        out_shape=(jax.ShapeDtypeStruct((bsz, seq, d), F32),
                   jax.ShapeDtypeStruct((bsz, heads, dk, dv), F32)),
        grid=(bsz // nb, seq // tl),
        in_specs=[
            pl.BlockSpec((nb, tl, d), lambda b, l: (b, l, 0)),
            pl.BlockSpec((nb, heads, dk, dv), lambda b, l: (b, 0, 0, 0)),
            _resident(gmix.shape), _resident(wq.shape), _resident(wk.shape), _resident(wv.shape),
            _resident(wg.shape), _resident(wlow.shape), _resident(wgk.shape), _resident(bgk.shape),
            _resident(gout.shape), _resident(wout.shape),
        ],
        out_specs=(pl.BlockSpec((nb, tl, d), lambda b, l: (b, l, 0)),
                   pl.BlockSpec((nb, heads, dk, dv), lambda b, l: (b, 0, 0, 0))),
        scratch_shapes=[pltpu.VMEM((nb, heads, dv, dk), F32)],
        compiler_params=pltpu.CompilerParams(
            dimension_semantics=("parallel", "arbitrary"), vmem_limit_bytes=VMEM_LIMIT),
        name="gla",
    )(x, s0, gmix, wq, wk, wv, wg, wlow, wgk, bgk, gout, wout)


def _mla_proj_kernel(x_ref, cos_ref, sin_ref, gmix_ref, wdq_ref, wdkv_ref, wdk_ref, wdkr_ref, qn_ref, kvn_ref,
                     wnope_ref, wr_ref, wrr_ref, wkey_ref, wval_ref,
                     q_ref, k_ref, vt_ref, ckv_ref, kpe_ref, *, absorb, q_scale):
    x = x_ref[0]
    h = _rms(x, gmix_ref[...]).astype(BF16)
    cos = cos_ref[...]
    sin = sin_ref[...]

    ckv = _rms(_dot(h, wdkv_ref[...]), kvn_ref[...])
    kpe = _dot(h, wdk_ref[...]) * cos + _dot(h, wdkr_ref[...]) * sin
    ckv_ref[0] = ckv
    kpe_ref[0] = kpe[:, :MLA_ROPE]
    ckv_b = ckv.astype(BF16)
    kpe_b = kpe.astype(BF16)
    ckv_t = ckv.T.astype(BF16)

    hq = _rms(_dot(h, wdq_ref[...]), qn_ref[...]).astype(BF16)
    q_nope = _dot(hq, wnope_ref[...])
    q_r = _dot(hq, wr_ref[...])
    q_rr = _dot(hq, wrr_ref[...])

    if absorb:
        k_ref[0, :, :MLA_KV_LORA] = ckv_b
        k_ref[0, :, MLA_KV_LORA:] = kpe_b
        vt_ref[0, 0] = ckv_t
    else:
        k_nope = _dot(ckv_b, wkey_ref[...]).astype(BF16)
        v_t = _dot(wval_ref[...], ckv_t).astype(BF16)
        ones = jnp.ones((V_ONES_ROWS, x.shape[0]), BF16)
    for hd in range(MLA_HEADS):
        hs = slice(hd * LANES, (hd + 1) * LANES)
        q_pe = q_r[:, hs] * cos + q_rr[:, hs] * sin
        if absorb:
            q_ref[0, hd, :, :MLA_KV_LORA] = _dot(q_nope[:, hs].astype(BF16), wkey_ref[hd]).astype(BF16)
            q_ref[0, hd, :, MLA_KV_LORA:] = q_pe.astype(BF16)
        else:
            q_ref[0, hd, :, :MLA_NOPE] = (q_nope[:, hs] * q_scale).astype(BF16)
            q_ref[0, hd, :, MLA_NOPE:] = (q_pe * q_scale).astype(BF16)
            k_ref[0, hd, :, :MLA_NOPE] = k_nope[:, hs]
            k_ref[0, hd, :, MLA_NOPE:] = kpe_b
            vt_ref[0, hd, 0, :MLA_V] = v_t[hd * MLA_V:(hd + 1) * MLA_V, :]
            vt_ref[0, hd, 0, MLA_V:] = ones


def _mla_proj_call(x, cos, sin, gmix, wdq, wdkv, wdk, wdkr, qn, kvn, wnope, wr, wrr, wkey, wval, *, tm, absorb,
                   q_scale=1.0):
    bsz, seq, d = x.shape
    assert seq % tm == 0
    weights = (gmix, wdq, wdkv, wdk, wdkr, qn, kvn, wnope, wr, wrr, wkey, wval)
    nt = seq // tm
    if absorb:
        qk_dim = KCAT
        q_k_vt = (jax.ShapeDtypeStruct((bsz, MLA_HEADS, seq, qk_dim), BF16),
                  jax.ShapeDtypeStruct((bsz, seq, qk_dim), BF16),
                  jax.ShapeDtypeStruct((bsz, nt, MLA_KV_LORA, tm), BF16))
        q_k_vt_specs = (pl.BlockSpec((1, MLA_HEADS, tm, qk_dim), lambda b, l: (b, 0, l, 0)),
                        pl.BlockSpec((1, tm, qk_dim), lambda b, l: (b, l, 0)),
                        pl.BlockSpec((1, 1, MLA_KV_LORA, tm), lambda b, l: (b, l, 0, 0)))
    else:
        qk_dim = MLA_NOPE + LANES
        q_k_vt = (jax.ShapeDtypeStruct((bsz, MLA_HEADS, seq, qk_dim), BF16),
                  jax.ShapeDtypeStruct((bsz, MLA_HEADS, seq, qk_dim), BF16),
                  jax.ShapeDtypeStruct((bsz, MLA_HEADS, nt, MLA_V + V_ONES_ROWS, tm), BF16))
        q_k_vt_specs = (pl.BlockSpec((1, MLA_HEADS, tm, qk_dim), lambda b, l: (b, 0, l, 0)),
                        pl.BlockSpec((1, MLA_HEADS, tm, qk_dim), lambda b, l: (b, 0, l, 0)),
                        pl.BlockSpec((1, MLA_HEADS, 1, MLA_V + V_ONES_ROWS, tm), lambda b, l: (b, 0, l, 0, 0)))
    return pl.pallas_call(
        functools.partial(_mla_proj_kernel, absorb=absorb, q_scale=q_scale),
        out_shape=q_k_vt + (jax.ShapeDtypeStruct((bsz, seq, MLA_KV_LORA), F32),
                            jax.ShapeDtypeStruct((bsz, seq, MLA_ROPE), F32)),
        grid=(bsz, nt),
        in_specs=[
            pl.BlockSpec((1, tm, d), lambda b, l: (b, l, 0)),
            pl.BlockSpec((tm, LANES), lambda b, l: (l, 0)),
            pl.BlockSpec((tm, LANES), lambda b, l: (l, 0)),
        ] + [_resident(w.shape) for w in weights],
        out_specs=q_k_vt_specs + (pl.BlockSpec((1, tm, MLA_KV_LORA), lambda b, l: (b, l, 0)),
                                  pl.BlockSpec((1, tm, MLA_ROPE), lambda b, l: (b, l, 0))),
        compiler_params=pltpu.CompilerParams(
            dimension_semantics=("parallel", "parallel"), vmem_limit_bytes=VMEM_LIMIT),
        name="mla_proj",
    )(x, cos, sin, *weights)


def _exp2_scaled(t, c):
    return jnp.exp2(t if c == 1.0 else t * c)


def _tile_update(s, vt, cs, m_s, l_s, acc_s, lag_s, c, one_pass):
    m_prev = m_s[:, cs]
    mx = jnp.max(s, axis=0, keepdims=True)
    m_new = jnp.maximum(m_prev, mx)
    alpha = _exp2_scaled(m_prev - m_new, c)
    if one_pass:
        p = _exp2_scaled(s - m_prev, c)
        if l_s is not None:
            l_s[:, cs] = (l_s[:, cs] + jnp.sum(p, axis=0, keepdims=True)) * alpha
        acc_s[:, cs] = (acc_s[:, cs] + _dot(vt, p.astype(BF16))) * alpha
        lag_s[...] = jnp.maximum(lag_s[...], (mx - m_prev) * c)
    else:
        p = _exp2_scaled(s - m_new, c)
        if l_s is not None:
            l_s[:, cs] = alpha * l_s[:, cs] + jnp.sum(p, axis=0, keepdims=True)
        acc_s[:, cs] = acc_s[:, cs] * alpha + _dot(vt, p.astype(BF16))
    m_s[:, cs] = m_new


def _attend(n_full, n_blk, cb, first_key_scores, diag_scores, diag_values, tile_scores, tile_values,
            m_s, l_s, acc_s, lag_s, c):
    def cols(i):
        return slice(i * cb, (i + 1) * cb)

    def run(one_pass):
        lag_s[...] = jnp.zeros(lag_s.shape, F32)
        for i in range(n_blk):
            m_s[:, cols(i)] = first_key_scores(i)
            acc_s[:, cols(i)] = jnp.zeros((acc_s.shape[0], cb), F32)
            if l_s is not None:
                l_s[:, cols(i)] = jnp.zeros((1, cb), F32)

        def update(units):
            s_next = units[0][0]()
            for u, (_, values, i) in enumerate(units):
                s = s_next
                if u + 1 < len(units):
                    s_next = units[u + 1][0]()
                _tile_update(s, values(), cols(i), m_s, l_s, acc_s, lag_s, c, one_pass)

        def full_tiles(js):
            return [(functools.partial(tile_scores, j, i), functools.partial(tile_values, j, i), i)
                    for j in js for i in range(n_blk)]

        update([(functools.partial(diag_scores, i), functools.partial(diag_values, i), i) for i in range(n_blk)])

        def pair(jj, carry):
            update(full_tiles((2 * jj, 2 * jj + 1)))
            return carry

        lax.fori_loop(0, n_full // 2, pair, 0)
        if isinstance(n_full, int):
            if n_full % 2:
                update(full_tiles((n_full - 1,)))
        else:
            @pl.when(n_full % 2 == 1)
            def _():
                update(full_tiles((n_full - 1,)))

    run(one_pass=True)

    @pl.when(jnp.max(lag_s[...]) > ATTN_LAG_LIMIT)
    def _():
        run(one_pass=False)


def _block_causal(tq, cb, col0):
    ki = lax.broadcasted_iota(jnp.int32, (tq, cb), 0) // CHUNK
    qi = ((lax.broadcasted_iota(jnp.int32, (tq, cb), 1) + col0) % tq) // CHUNK
    return ki <= qi


def _mla_attn_kernel(q_ref, kfull_ref, vtfull_ref, kdiag_ref, vtdiag_ref, x_ref, wuvt_ref, wout_ref, y_ref,
                     o_s, m_s, l_s, acc_s, lag_s, *, tq, tk, cb, past_tiles, single_block, scale):
    qb = pl.program_id(1)
    heads = q_ref.shape[1]
    n_blk = heads * tq // cb
    heads_per_blk = cb // tq

    def q_blk(i):
        return q_ref[0, i * heads_per_blk:(i + 1) * heads_per_blk].reshape(cb, KCAT)

    def diag_scores(i):
        return jnp.where(_block_causal(tq, cb, i * cb), _dot_nt(kdiag_ref[0], q_blk(i)), -jnp.inf)

    def tile_scores(j, i):
        return _dot_nt(kfull_ref[0, pl.ds(pl.multiple_of(j * tk, tk), tk), :], q_blk(i))

    def first_key_scores(i):
        return _dot_nt(kdiag_ref[0, :BF16_SUBLANES, :], q_blk(i))[:1]

    n_full = past_tiles if single_block else past_tiles + qb * (tq // tk)
    _attend(n_full, n_blk, cb, first_key_scores, diag_scores, lambda i: vtdiag_ref[0, 0],
            tile_scores, lambda j, i: vtfull_ref[0, j], m_s, l_s, acc_s, lag_s, scale * LOG2E)

    for hd in range(heads):
        cs = slice(hd * tq, (hd + 1) * tq)
        o_lat_t = (acc_s[:, cs] * (1.0 / l_s[:, cs])).astype(BF16)
        o_s[hd] = _dot(wuvt_ref[hd], o_lat_t).astype(BF16)
    o_t = o_s[...].reshape(o_s.shape[0] * o_s.shape[1], tq)
    y_ref[0] = x_ref[0] + _dot_tn(o_t, wout_ref[...])


def _mla_attn_call(q, kfull, vtfull, kdiag, vtdiag, x, wuvt, wout, *, tq, tk, scale):
    bsz, heads, seq, _ = q.shape
    d = x.shape[-1]
    lfull = kfull.shape[1]
    nq = seq // tq
    assert seq % tq == 0 and lfull % tk == 0
    assert vtfull.shape == (bsz, lfull // tk, MLA_KV_LORA, tk) and vtdiag.shape == (bsz, nq, MLA_KV_LORA, tq)
    if nq == 1:
        past_tiles = lfull // tk
    else:
        assert tq == tk and lfull == seq
        past_tiles = 0
    cb = max(tq, min(heads * tq, ATTN_COL_BLOCK))
    assert cb % tq == 0 and (heads * tq) % cb == 0
    v_dim = wuvt.shape[1]
    return pl.pallas_call(
        functools.partial(_mla_attn_kernel, tq=tq, tk=tk, cb=cb, past_tiles=past_tiles, single_block=nq == 1,
                          scale=scale),
        out_shape=jax.ShapeDtypeStruct((bsz, seq, d), F32),
        grid=(bsz, nq),
        in_specs=[
            pl.BlockSpec((1, heads, tq, KCAT), lambda b, i: (b, 0, i, 0)),
            pl.BlockSpec((1, lfull, KCAT), lambda b, i: (b, 0, 0)),
            pl.BlockSpec((1, lfull // tk, MLA_KV_LORA, tk), lambda b, i: (b, 0, 0, 0)),
            pl.BlockSpec((1, tq, KCAT), lambda b, i: (b, i, 0)),
            pl.BlockSpec((1, 1, MLA_KV_LORA, tq), lambda b, i: (b, i, 0, 0)),
            pl.BlockSpec((1, tq, d), lambda b, i: (b, i, 0)),
            _resident(wuvt.shape),
            _resident(wout.shape),
        ],
        out_specs=pl.BlockSpec((1, tq, d), lambda b, i: (b, i, 0)),
        scratch_shapes=[
            pltpu.VMEM((heads, v_dim, tq), BF16),
            pltpu.VMEM((1, heads * tq), F32),
            pltpu.VMEM((1, heads * tq), F32),
            pltpu.VMEM((MLA_KV_LORA, heads * tq), F32),
            pltpu.VMEM((1, cb), F32),
        ],
        compiler_params=pltpu.CompilerParams(
            dimension_semantics=("parallel", "parallel"), vmem_limit_bytes=VMEM_LIMIT),
        name="mla_attn",
    )(q, kfull, vtfull, kdiag, vtdiag, x, wuvt, wout)


def _mha_attn_kernel(q_ref, k_ref, vt_ref, kdiag_ref, vtdiag_ref, ot_ref, m_s, acc_s, lag_s, *, tq):
    qb = pl.program_id(2)
    hp = q_ref.shape[1]
    visible = _block_causal(tq, tq, 0)

    def diag_scores(i):
        return jnp.where(visible, _dot_nt(kdiag_ref[0, i], q_ref[0, i]), -jnp.inf)

    def tile_scores(j, i):
        return _dot_nt(k_ref[0, i, pl.ds(pl.multiple_of(j * tq, tq), tq), :], q_ref[0, i])

    def first_key_scores(i):
        return _dot_nt(kdiag_ref[0, i, :BF16_SUBLANES, :], q_ref[0, i])[:1]

    _attend(qb, hp, tq, first_key_scores, diag_scores, lambda i: vtdiag_ref[0, i, 0],
            tile_scores, lambda j, i: vt_ref[0, i, j], m_s, None, acc_s, lag_s, 1.0)

    for i in range(hp):
        cs = slice(i * tq, (i + 1) * tq)
        inv_l = 1.0 / acc_s[MLA_V:MLA_V + 1, cs]
        ot_ref[0, i * MLA_V:(i + 1) * MLA_V, :] = (acc_s[:MLA_V, cs] * inv_l).astype(BF16)


def _mha_attn_call(q, k, vt, *, hp, tq):
    bsz, heads, seq, qk_dim = q.shape
    nq = seq // tq
    v_rows = MLA_V + V_ONES_ROWS
    assert seq % tq == 0 and heads % hp == 0 and vt.shape == (bsz, heads, nq, v_rows, tq)
    return pl.pallas_call(
        functools.partial(_mha_attn_kernel, tq=tq),
        out_shape=jax.ShapeDtypeStruct((bsz, heads * MLA_V, seq), BF16),
        grid=(bsz, heads // hp, nq),
        in_specs=[
            pl.BlockSpec((1, hp, tq, qk_dim), lambda b, g, i: (b, g, i, 0)),
            pl.BlockSpec((1, hp, seq, qk_dim), lambda b, g, i: (b, g, 0, 0)),
            pl.BlockSpec((1, hp, nq, v_rows, tq), lambda b, g, i: (b, g, 0, 0, 0)),
            pl.BlockSpec((1, hp, tq, qk_dim), lambda b, g, i: (b, g, i, 0)),
            pl.BlockSpec((1, hp, 1, v_rows, tq), lambda b, g, i: (b, g, i, 0, 0)),
        ],
        out_specs=pl.BlockSpec((1, hp * MLA_V, tq), lambda b, g, i: (b, g, i)),
        scratch_shapes=[
            pltpu.VMEM((1, hp * tq), F32),
            pltpu.VMEM((v_rows, hp * tq), F32),
            pltpu.VMEM((1, tq), F32),
        ],
        compiler_params=pltpu.CompilerParams(
            dimension_semantics=("parallel", "parallel", "arbitrary"), vmem_limit_bytes=VMEM_LIMIT),
        name="mha_attn",
    )(q, k, vt, k, vt)


def _rope_tables(pos):
    half = MLA_ROPE // 2
    inv = ROPE_THETA ** (-jnp.arange(half, dtype=F32) / half)
    ang = pos.astype(F32)[:, None] * inv[None, :]
    pad = jnp.zeros((pos.shape[0], LANES - MLA_ROPE), F32)
    cos = jnp.concatenate([jnp.cos(ang), jnp.cos(ang), pad], axis=-1)
    sin = jnp.concatenate([jnp.sin(ang), jnp.sin(ang), pad], axis=-1)
    return cos, sin


def _rotate_half_cols(w):
    half = w.shape[-1] // 2
    return jnp.concatenate([-w[..., half:], w[..., :half]], axis=-1)


def _pad_last(w, n):
    return jnp.concatenate([w, jnp.zeros(w.shape[:-1] + (n - w.shape[-1],), w.dtype)], axis=-1)


def _prep_gla(w_in, w_gk_up, b_gk, g_norm, w_out):
    qk = w_gk_up.shape[1]
    vw = w_out.shape[0]
    wq = w_in[:, :qk]
    wk = w_in[:, qk:2 * qk]
    wv = w_in[:, 2 * qk:2 * qk + vw]
    wg = w_in[:, 2 * qk + vw:2 * qk + 2 * vw]
    wlow = _pad_last(w_in[:, 2 * qk + 2 * vw:], LANES)
    rank = w_gk_up.shape[0]
    wgk = jnp.concatenate([w_gk_up, jnp.zeros((LANES - rank, qk), w_gk_up.dtype)], axis=0)
    return (wq.astype(BF16), wk.astype(BF16), wv.astype(BF16), wg.astype(BF16), wlow.astype(BF16),
            wgk.astype(BF16), b_gk[None, :], g_norm[None, :], w_out.astype(BF16))


def _prep_mla(w_down, q_norm, w_uq, kv_norm, w_uk, w_uv, w_out):
    wdq = w_down[:, :MLA_Q_LORA]
    wdkv = w_down[:, MLA_Q_LORA:MLA_Q_LORA + MLA_KV_LORA]
    wdk = w_down[:, MLA_Q_LORA + MLA_KV_LORA:]
    wdkr = _rotate_half_cols(wdk)
    wuq = w_uq.reshape(MLA_Q_LORA, MLA_HEADS, MLA_NOPE + MLA_ROPE)
    wnope = wuq[..., :MLA_NOPE].reshape(MLA_Q_LORA, MLA_HEADS * MLA_NOPE)
    wrope = wuq[..., MLA_NOPE:]
    wr = _pad_last(wrope, LANES).reshape(MLA_Q_LORA, MLA_HEADS * LANES)
    wrr = _pad_last(_rotate_half_cols(wrope), LANES).reshape(MLA_Q_LORA, MLA_HEADS * LANES)
    wuk = jnp.transpose(w_uk, (1, 2, 0))
    wuvt = jnp.transpose(w_uv, (1, 2, 0))
    wuk_flat = w_uk.reshape(MLA_KV_LORA, MLA_HEADS * MLA_NOPE)
    wuv_flat_t = w_uv.reshape(MLA_KV_LORA, MLA_HEADS * MLA_V).T
    return dict(
        wdq=wdq.astype(BF16), wdkv=wdkv.astype(BF16), wdk=_pad_last(wdk, LANES).astype(BF16),
        wdkr=_pad_last(wdkr, LANES).astype(BF16), qn=q_norm[None, :], kvn=kv_norm[None, :],
        wnope=wnope.astype(BF16), wr=wr.astype(BF16), wrr=wrr.astype(BF16), wuk=wuk.astype(BF16),
        wuvt=wuvt.astype(BF16), wuk_flat=wuk_flat.astype(BF16), wuv_flat_t=wuv_flat_t.astype(BF16),
        wout=w_out.astype(BF16))


def kernel(x_prompt, x_sample, state_gla, cache_ckv, cache_kpe, norm_ffn1, w_ffn1_gate, w_ffn1_up, w_ffn1_down, norm_mix, norm_ffn2, w_ffn2_gate, w_ffn2_up, w_ffn2_down, w_gla_in, w_gla_gk_up, b_gla_gk, gla_out_norm, w_gla_out, w_mla_down, mla_q_norm, w_mla_uq, mla_kv_norm, w_mla_uk, w_mla_uv, w_mla_out, norm_final):
    depth = norm_ffn1.shape[0]
    n_mixers = 2
    d = x_prompt.shape[-1]
    mla_scale = (MLA_NOPE + MLA_ROPE) ** -0.5
    gf = norm_final[None, :]

    ffn1 = (norm_ffn1[:, None, :], w_ffn1_gate, w_ffn1_up, w_ffn1_down)
    ffn2 = (norm_ffn2[:, None, :], w_ffn2_gate, w_ffn2_up, w_ffn2_down)
    gla_w = [_prep_gla(w_gla_in[j], w_gla_gk_up[j], b_gla_gk[j], gla_out_norm[j], w_gla_out[j])
             for j in range(w_gla_in.shape[0])]
    mla_w = [_prep_mla(w_mla_down[j], mla_q_norm[j], w_mla_uq[j], mla_kv_norm[j], w_mla_uk[j], w_mla_uv[j],
                       w_mla_out[j]) for j in range(w_mla_down.shape[0])]

    def run(x, pos, gla_s0, past_ckv, past_kpe, *, gla_nb, gla_tl, attn_tq):
        bsz, seq, _ = x.shape
        cos, sin = _rope_tables(pos)
        new_gla, new_ckv, new_kpe = [], [], []
        pending = {}

        def ffn(x, w, layer, final_norm=False):
            return _ffn_call(x.reshape(bsz * seq, d), w[0], w[1], w[2], w[3], gf, layer,
                             final_norm=final_norm, **pending).reshape(bsz, seq, d)

        for i in range(depth):
            x = ffn(x, ffn1, i)
            j = i // n_mixers
            gm = norm_mix[i][None, :]
            if i % n_mixers == 0:
                x, s_fin = _gla_call(x, gla_s0[j], gm, *gla_w[j], nb=gla_nb, tl=gla_tl)
                new_gla.append(s_fin)
            else:
                w = mla_w[j]
                absorb = past_ckv is not None
                q, k, vt, ckv, kpe = _mla_proj_call(
                    x, cos, sin, gm, w["wdq"], w["wdkv"], w["wdk"], w["wdkr"], w["qn"], w["kvn"],
                    w["wnope"], w["wr"], w["wrr"], w["wuk"] if absorb else w["wuk_flat"], w["wuv_flat_t"],
                    tm=attn_tq, absorb=absorb, q_scale=1.0 if absorb else mla_scale * LOG2E)
                if absorb:
                    pc = past_ckv[j].astype(BF16)
                    pad = jnp.zeros(past_kpe[j].shape[:-1] + (KCAT - MLA_KV_LORA - MLA_ROPE,), BF16)
                    kfull = jnp.concatenate([pc, past_kpe[j].astype(BF16), pad], axis=-1)
                    vtfull = jnp.swapaxes(pc.reshape(bsz, -1, ATTN_KEY_TILE, MLA_KV_LORA), 2, 3)
                    x = _mla_attn_call(q, kfull, vtfull, k, vt, x, w["wuvt"], w["wout"],
                                       tq=attn_tq, tk=ATTN_KEY_TILE, scale=mla_scale)
                else:
                    pending = dict(attn_t=_mha_attn_call(q, k, vt, hp=ATTN_HEAD_GROUP, tq=attn_tq), w_out=w["wout"])
                new_ckv.append(ckv)
                new_kpe.append(kpe)
            x = ffn(x, ffn2, i, final_norm=(i == depth - 1))
            pending = {}
        return x, jnp.stack(new_gla), jnp.stack(new_ckv), jnp.stack(new_kpe)

    bp, lp = x_prompt.shape[0], x_prompt.shape[1]
    n_gla = w_gla_in.shape[0]
    s0_p = jnp.zeros((n_gla, bp) + state_gla.shape[2:], x_prompt.dtype)
    y_p, gla_p, ckv_p, kpe_p = run(
        x_prompt, jnp.arange(lp), s0_p, None, None, gla_nb=1, gla_tl=512, attn_tq=ATTN_KEY_TILE)

    past_len = cache_ckv.shape[2]
    ls = x_sample.shape[1]
    y_s, gla_s, ckv_s, kpe_s = run(
        x_sample, past_len + jnp.arange(ls), state_gla, cache_ckv, cache_kpe, gla_nb=8, gla_tl=ls, attn_tq=ls)

    return (y_p, y_s, gla_p, ckv_p, kpe_p, gla_s, ckv_s, kpe_s)
```

```python
import functools

import jax
import jax.numpy as jnp
from jax import lax
from jax.experimental import pallas as pl
from jax.experimental.pallas import tpu as pltpu

F32 = jnp.float32
BF16 = jnp.bfloat16

EPS = 1e-6
CHUNK = 64
GLA_HEADS = 4
GLA_GATE_NORM = 16.0
GLA_ROW_GROUPS = 2
MLA_HEADS = 8
MLA_NOPE = 128
MLA_ROPE = 64
MLA_V = 128
MLA_Q_LORA = 384
MLA_KV_LORA = 256
ROPE_THETA = 10000.0
LANES = 128
BF16_SUBLANES = 16
MXU_COLS = 256
LOG2E = 1.4426950408889634
ATTN_COL_BLOCK = 512
ATTN_KEY_TILE = 512
ATTN_HEAD_GROUP = 2
V_ONES_ROWS = 16
ATTN_LAG_LIMIT = 64.0
KCAT = MLA_KV_LORA + LANES

VMEM_LIMIT = 56 * 1024 * 1024


def _rms(xf, g):
    return xf * lax.rsqrt(jnp.mean(xf * xf, axis=-1, keepdims=True) + EPS) * g


def _silu(a):
    return a * (1.0 / (1.0 + jnp.exp(-a)))


def _dot(a, b):
    return jnp.dot(a, b, preferred_element_type=F32)


def _dot_nt(a, b):
    return lax.dot_general(a, b, (((1,), (1,)), ((), ())), preferred_element_type=F32)


def _dot_tn(a, b):
    return lax.dot_general(a, b, (((0,), (0,)), ((), ())), preferred_element_type=F32)


def _resident(shape):
    zeros = (0,) * len(shape)
    return pl.BlockSpec(shape, lambda *_: zeros, pipeline_mode=pl.Buffered(1))


def _ffn_kernel(*refs, f_chunk, final_norm, mixer_proj):
    if mixer_proj:
        x_ref, ot_ref, wo_ref, g_ref, wg_ref, wu_ref, wd_ref, gf_ref, o_ref, hid_ref = refs
        x = x_ref[...] + _dot_tn(ot_ref[0], wo_ref[...])
    else:
        x_ref, g_ref, wg_ref, wu_ref, wd_ref, gf_ref, o_ref, hid_ref = refs
        x = x_ref[...]
    h = _rms(x, g_ref[0]).astype(BF16)
    d_ff = wg_ref.shape[2]
    for c in range(d_ff // f_chunk):
        sl = slice(c * f_chunk, (c + 1) * f_chunk)
        a = _dot(h, wg_ref[0, :, sl].astype(BF16))
        b = _dot(h, wu_ref[0, :, sl].astype(BF16))
        hid_ref[:, sl] = (_silu(a) * b).astype(BF16)
    y = x + 0.5 * _dot(hid_ref[...], wd_ref[0].astype(BF16))
    if final_norm:
        y = _rms(y, gf_ref[...])
    o_ref[...] = y


def _layer_resident(shape, layer):
    index = (layer,) + (0,) * (len(shape) - 1)
    return pl.BlockSpec((1,) + tuple(shape[1:]), lambda *_: index, pipeline_mode=pl.Buffered(1))


def _ffn_call(x, g, wg, wu, wd, gf, layer, *, final_norm, attn_t=None, w_out=None, tm=512, f_chunk=256):
    t, d = x.shape
    d_ff = wg.shape[2]
    assert t % tm == 0 and d_ff % f_chunk == 0
    mixer_proj = attn_t is not None
    x_spec = pl.BlockSpec((tm, d), lambda i: (i, 0))
    weights = (g, wg, wu, wd)
    if mixer_proj:
        tiles_per_seq = attn_t.shape[2] // tm
        assert attn_t.shape[0] * attn_t.shape[2] == t and attn_t.shape[2] % tm == 0
        ot_spec = pl.BlockSpec((1, attn_t.shape[1], tm), lambda i: (i // tiles_per_seq, 0, i % tiles_per_seq))
        operands = (x, attn_t, w_out) + weights + (gf,)
        in_specs = [x_spec, ot_spec, _resident(w_out.shape)]
    else:
        operands = (x,) + weights + (gf,)
        in_specs = [x_spec]
    return pl.pallas_call(
        functools.partial(_ffn_kernel, f_chunk=f_chunk, final_norm=final_norm, mixer_proj=mixer_proj),
        out_shape=jax.ShapeDtypeStruct((t, d), F32),
        grid=(t // tm,),
        in_specs=in_specs + [_layer_resident(w.shape, layer) for w in weights] + [_resident(gf.shape)],
        out_specs=pl.BlockSpec((tm, d), lambda i: (i, 0)),
        scratch_shapes=[pltpu.VMEM((tm, d_ff), BF16)],
        compiler_params=pltpu.CompilerParams(
            dimension_semantics=("parallel",), vmem_limit_bytes=VMEM_LIMIT),
        name="ffn",
    )(*operands)


def _split3(a):
    hi = a.astype(BF16)
    r = a - hi.astype(F32)
    mid = r.astype(BF16)
    lo = (r - mid.astype(F32)).astype(BF16)
    return hi, mid, lo


def _interleave(primary, filler):
    for n, step in enumerate(primary):
        step()
        if n < len(filler):
            filler[n]()
    for extra in filler[len(primary):]:
        extra()


def _gla_kernel(x_ref, s0_ref, gmix_ref, wq_ref, wk_ref, wv_ref, wg_ref, wlow_ref, wgk_ref, bgk_ref,
                gout_ref, wout_ref, y_ref, s_ref, st_s, *, nb, tl, n_sub):
    l = pl.program_id(1)
    heads = GLA_HEADS
    dk = wq_ref.shape[1] // heads
    dv = wv_ref.shape[1] // heads
    rows_sub = nb * tl // n_sub
    n_chunk = rows_sub // CHUNK
    d = x_ref.shape[-1]

    @pl.when(l == 0)
    def _():
        for b in range(nb):
            for hd in range(heads):
                st_s[b, hd] = s0_ref[b, hd].T

    ti = lax.broadcasted_iota(jnp.int32, (CHUNK, CHUNK), 0)
    si = lax.broadcasted_iota(jnp.int32, (CHUNK, CHUNK), 1)
    causal = si <= ti
    tri = jnp.where(causal, 1.0, 0.0).astype(BF16)

    def rows_of(ref, sub):
        row0 = sub * rows_sub
        if rows_sub >= tl:
            return ref.at[row0 // tl:(row0 + rows_sub) // tl]
        return ref.at[row0 // tl, row0 % tl:row0 % tl + rows_sub, :]

    def in_projection(sub):
        env = {}

        def start():
            env["x"] = rows_of(x_ref, sub)[...].reshape(rows_sub, d)
            env["h"] = _rms(env["x"], gmix_ref[...]).astype(BF16)

        def piece(name, ref, lo, hi):
            return lambda: env.setdefault(name, []).append(_dot(env["h"], ref[:, lo:hi]))

        thunks = [start]
        for name, ref in (("low", wlow_ref), ("q", wq_ref), ("k", wk_ref), ("v", wv_ref), ("g", wg_ref)):
            width = ref.shape[1]
            thunks += [piece(name, ref, lo, min(lo + MXU_COLS, width)) for lo in range(0, width, MXU_COLS)]
        return thunks, env

    def recurrence(sub, env):
        row0 = sub * rows_sub
        cat = lambda name: jnp.concatenate(env[name], axis=-1)
        loc = {}

        def start():
            loc["q"] = cat("q") * (dk ** -0.5)
            loc["k"] = cat("k")
            loc["v"] = cat("v").astype(BF16)
            z = _dot(cat("low").astype(BF16), wgk_ref[...]) + bgk_ref[...]
            loc["log_a"] = -(jnp.maximum(-z, 0.0) + jnp.log1p(jnp.exp(-jnp.abs(z)))) / GLA_GATE_NORM
            loc["state"] = {}
            loc["o"] = [[None] * n_chunk for _ in range(heads)]

        def cumsum(c):
            rs = slice(c * CHUNK, (c + 1) * CHUNK)
            a_hi, a_mid, a_lo = _split3(loc["log_a"][rs, :])
            loc["cum", c] = _dot(tri, a_hi) + _dot(tri, a_mid) + _dot(tri, a_lo)

        def scores(c):
            rs = slice(c * CHUNK, (c + 1) * CHUNK)
            cum = loc["cum", c]
            cum_last = cum[CHUNK - 1:CHUNK, :]
            q_dec = (loc["q"][rs, :] * jnp.exp(cum)).astype(BF16)
            k_inv = (loc["k"][rs, :] * jnp.exp(-cum)).astype(BF16)
            loc["k_tail", c] = (loc["k"][rs, :] * jnp.exp(cum_last - cum)).astype(BF16)
            loc["q_dec", c] = q_dec
            loc["dec", c] = jnp.exp(cum_last)
            loc["scores", c] = [
                jnp.where(causal, _dot_nt(q_dec[:, hd * dk:(hd + 1) * dk], k_inv[:, hd * dk:(hd + 1) * dk]), 0.0)
                .astype(BF16) for hd in range(heads)]

        def values(c):
            rs = slice(c * CHUNK, (c + 1) * CHUNK)
            vs = [loc["v"][rs, hd * dv:(hd + 1) * dv] for hd in range(heads)]
            loc["o_intra", c] = [_dot(loc["scores", c][hd], vs[hd]) for hd in range(heads)]
            loc["kv_t", c] = [_dot_tn(vs[hd], loc["k_tail", c][:, hd * dk:(hd + 1) * dk]) for hd in range(heads)]

        def carry(c):
            b = (row0 + c * CHUNK) // tl
            for hd in range(heads):
                ks = slice(hd * dk, (hd + 1) * dk)
                st = loc["state"].get((b, hd))
                if st is None:
                    st = st_s[b, hd]
                loc["o"][hd][c] = loc["o_intra", c][hd] + _dot_nt(loc["q_dec", c][:, ks], st.astype(BF16))
                loc["state"][(b, hd)] = st * loc["dec", c][:, ks] + loc["kv_t", c][hd]

        def finish():
            for (b, hd), st in loc["state"].items():
                st_s[b, hd] = st
            env["o"] = [jnp.concatenate(loc["o"][hd], axis=0) for hd in range(heads)]

        thunks = [start]
        for c in range(n_chunk):
            thunks += [functools.partial(cumsum, c), functools.partial(scores, c), functools.partial(values, c)]
        thunks += [functools.partial(carry, c) for c in range(n_chunk)] + [finish]
        return thunks

    def out_projection(sub, env):
        loc = {}

        def gate():
            g = jnp.concatenate(env["g"], axis=-1)
            loc["gated"] = jnp.concatenate(
                [(_rms(env["o"][hd], gout_ref[...]) * _silu(g[:, hd * dv:(hd + 1) * dv])).astype(BF16)
                 for hd in range(heads)], axis=-1)

        def piece(lo):
            return lambda: loc.setdefault("y", []).append(_dot(loc["gated"], wout_ref[:, lo:lo + MXU_COLS]))

        def store():
            y = env["x"] + jnp.concatenate(loc["y"], axis=-1)
            dst = rows_of(y_ref, sub)
            dst[...] = y.reshape(dst.shape)

        return [gate] + [piece(lo) for lo in range(0, d, MXU_COLS)] + [store]

    proj = [in_projection(sub) for sub in range(n_sub)]
    for step in proj[0][0]:
        step()
    for sub in range(n_sub):
        filler = []
        if sub + 1 < n_sub:
            filler += proj[sub + 1][0]
        if sub >= 1:
            filler += out_projection(sub - 1, proj[sub - 1][1])
        _interleave(recurrence(sub, proj[sub][1]), filler)
    for step in out_projection(n_sub - 1, proj[n_sub - 1][1]):
        step()

    @pl.when(l == pl.num_programs(1) - 1)
    def _():
        for b in range(nb):
            for hd in range(heads):
                s_ref[b, hd] = st_s[b, hd].T


def _gla_call(x, s0, gmix, wq, wk, wv, wg, wlow, wgk, bgk, gout, wout, *, nb, tl, n_sub=GLA_ROW_GROUPS):
    bsz, seq, d = x.shape
    heads = GLA_HEADS
    dk = wq.shape[1] // heads
    dv = wv.shape[1] // heads
    rows_sub = nb * tl // n_sub
    assert bsz % nb == 0 and seq % tl == 0 and tl % CHUNK == 0 and (nb * tl) % n_sub == 0
    assert rows_sub % CHUNK == 0 and (rows_sub % tl == 0 or tl % rows_sub == 0)
    return pl.pallas_call(
        functools.partial(_gla_kernel, nb=nb, tl=tl, n_sub=n_sub),
        out_shape=(jax.ShapeDtypeStruct((bsz, seq, d), F32),
                   jax.ShapeDtypeStruct((bsz, heads, dk, dv), F32)),
        grid=(bsz // nb, seq // tl),
        in_specs=[
            pl.BlockSpec((nb, tl, d), lambda b, l: (b, l, 0)),
            pl.BlockSpec((nb, heads, dk, dv), lambda b, l: (b, 0, 0, 0)),
            _resident(gmix.shape), _resident(wq.shape), _resident(wk.shape), _resident(wv.shape),
            _resident(wg.shape), _resident(wlow.shape), _resident(wgk.shape), _resident(bgk.shape),
            _resident(gout.shape), _resident(wout.shape),
        ],
        out_specs=(pl.BlockSpec((nb, tl, d), lambda b, l: (b, l, 0)),
                   pl.BlockSpec((nb, heads, dk, dv), lambda b, l: (b, 0, 0, 0))),
        scratch_shapes=[pltpu.VMEM((nb, heads, dv, dk), F32)],
        compiler_params=pltpu.CompilerParams(
            dimension_semantics=("parallel", "arbitrary"), vmem_limit_bytes=VMEM_LIMIT),
        name="gla",
    )(x, s0, gmix, wq, wk, wv, wg, wlow, wgk, bgk, gout, wout)


def _mla_proj_kernel(x_ref, cos_ref, sin_ref, gmix_ref, wdq_ref, wdkv_ref, wdk_ref, wdkr_ref, qn_ref, kvn_ref,
                     wnope_ref, wr_ref, wrr_ref, wkey_ref, wval_ref,
                     q_ref, k_ref, vt_ref, ckv_ref, kpe_ref, *, absorb, q_scale):
    x = x_ref[0]
    h = _rms(x, gmix_ref[...]).astype(BF16)
    cos = cos_ref[...]
    sin = sin_ref[...]

    ckv = _rms(_dot(h, wdkv_ref[...]), kvn_ref[...])
    kpe = _dot(h, wdk_ref[...]) * cos + _dot(h, wdkr_ref[...]) * sin
    ckv_ref[0] = ckv
    kpe_ref[0] = kpe[:, :MLA_ROPE]
    ckv_b = ckv.astype(BF16)
    kpe_b = kpe.astype(BF16)
    ckv_t = ckv.T.astype(BF16)

    hq = _rms(_dot(h, wdq_ref[...]), qn_ref[...]).astype(BF16)
    q_nope = _dot(hq, wnope_ref[...])
    q_r = _dot(hq, wr_ref[...])
    q_rr = _dot(hq, wrr_ref[...])

    if absorb:
        k_ref[0, :, :MLA_KV_LORA] = ckv_b
        k_ref[0, :, MLA_KV_LORA:] = kpe_b
        vt_ref[0, 0] = ckv_t
    else:
        k_nope = _dot(ckv_b, wkey_ref[...]).astype(BF16)
        v_t = _dot(wval_ref[...], ckv_t).astype(BF16)
        ones = jnp.ones((V_ONES_ROWS, x.shape[0]), BF16)
    for hd in range(MLA_HEADS):
        hs = slice(hd * LANES, (hd + 1) * LANES)
        q_pe = q_r[:, hs] * cos + q_rr[:, hs] * sin
        if absorb:
            q_ref[0, hd, :, :MLA_KV_LORA] = _dot(q_nope[:, hs].astype(BF16), wkey_ref[hd]).astype(BF16)
            q_ref[0, hd, :, MLA_KV_LORA:] = q_pe.astype(BF16)
        else:
            q_ref[0, hd, :, :MLA_NOPE] = (q_nope[:, hs] * q_scale).astype(BF16)
            q_ref[0, hd, :, MLA_NOPE:] = (q_pe * q_scale).astype(BF16)
            k_ref[0, hd, :, :MLA_NOPE] = k_nope[:, hs]
            k_ref[0, hd, :, MLA_NOPE:] = kpe_b
            vt_ref[0, hd, 0, :MLA_V] = v_t[hd * MLA_V:(hd + 1) * MLA_V, :]
            vt_ref[0, hd, 0, MLA_V:] = ones


def _mla_proj_call(x, cos, sin, gmix, wdq, wdkv, wdk, wdkr, qn, kvn, wnope, wr, wrr, wkey, wval, *, tm, absorb,
                   q_scale=1.0):
    bsz, seq, d = x.shape
    assert seq % tm == 0
    weights = (gmix, wdq, wdkv, wdk, wdkr, qn, kvn, wnope, wr, wrr, wkey, wval)
    nt = seq // tm
    if absorb:
        qk_dim = KCAT
        q_k_vt = (jax.ShapeDtypeStruct((bsz, MLA_HEADS, seq, qk_dim), BF16),
                  jax.ShapeDtypeStruct((bsz, seq, qk_dim), BF16),
                  jax.ShapeDtypeStruct((bsz, nt, MLA_KV_LORA, tm), BF16))
        q_k_vt_specs = (pl.BlockSpec((1, MLA_HEADS, tm, qk_dim), lambda b, l: (b, 0, l, 0)),
                        pl.BlockSpec((1, tm, qk_dim), lambda b, l: (b, l, 0)),
                        pl.BlockSpec((1, 1, MLA_KV_LORA, tm), lambda b, l: (b, l, 0, 0)))
    else:
        qk_dim = MLA_NOPE + LANES
        q_k_vt = (jax.ShapeDtypeStruct((bsz, MLA_HEADS, seq, qk_dim), BF16),
                  jax.ShapeDtypeStruct((bsz, MLA_HEADS, seq, qk_dim), BF16),
                  jax.ShapeDtypeStruct((bsz, MLA_HEADS, nt, MLA_V + V_ONES_ROWS, tm), BF16))
        q_k_vt_specs = (pl.BlockSpec((1, MLA_HEADS, tm, qk_dim), lambda b, l: (b, 0, l, 0)),
                        pl.BlockSpec((1, MLA_HEADS, tm, qk_dim), lambda b, l: (b, 0, l, 0)),
                        pl.BlockSpec((1, MLA_HEADS, 1, MLA_V + V_ONES_ROWS, tm), lambda b, l: (b, 0, l, 0, 0)))
    return pl.pallas_call(
        functools.partial(_mla_proj_kernel, absorb=absorb, q_scale=q_scale),
        out_shape=q_k_vt + (jax.ShapeDtypeStruct((bsz, seq, MLA_KV_LORA), F32),
                            jax.ShapeDtypeStruct((bsz, seq, MLA_ROPE), F32)),
        grid=(bsz, nt),
        in_specs=[
            pl.BlockSpec((1, tm, d), lambda b, l: (b, l, 0)),
            pl.BlockSpec((tm, LANES), lambda b, l: (l, 0)),
            pl.BlockSpec((tm, LANES), lambda b, l: (l, 0)),
        ] + [_resident(w.shape) for w in weights],
        out_specs=q_k_vt_specs + (pl.BlockSpec((1, tm, MLA_KV_LORA), lambda b, l: (b, l, 0)),
                                  pl.BlockSpec((1, tm, MLA_ROPE), lambda b, l: (b, l, 0))),
        compiler_params=pltpu.CompilerParams(
            dimension_semantics=("parallel", "parallel"), vmem_limit_bytes=VMEM_LIMIT),
        name="mla_proj",
    )(x, cos, sin, *weights)


def _exp2_scaled(t, c):
    return jnp.exp2(t if c == 1.0 else t * c)


def _tile_update(s, vt, cs, m_s, l_s, acc_s, lag_s, c, one_pass):
    m_prev = m_s[:, cs]
    mx = jnp.max(s, axis=0, keepdims=True)
    m_new = jnp.maximum(m_prev, mx)
    alpha = _exp2_scaled(m_prev - m_new, c)
    if one_pass:
        p = _exp2_scaled(s - m_prev, c)
        if l_s is not None:
            l_s[:, cs] = (l_s[:, cs] + jnp.sum(p, axis=0, keepdims=True)) * alpha
        acc_s[:, cs] = (acc_s[:, cs] + _dot(vt, p.astype(BF16))) * alpha
        lag_s[...] = jnp.maximum(lag_s[...], (mx - m_prev) * c)
    else:
        p = _exp2_scaled(s - m_new, c)
        if l_s is not None:
            l_s[:, cs] = alpha * l_s[:, cs] + jnp.sum(p, axis=0, keepdims=True)
        acc_s[:, cs] = acc_s[:, cs] * alpha + _dot(vt, p.astype(BF16))
    m_s[:, cs] = m_new


def _attend(n_full, n_blk, cb, first_key_scores, diag_scores, diag_values, tile_scores, tile_values,
            m_s, l_s, acc_s, lag_s, c):
    def cols(i):
        return slice(i * cb, (i + 1) * cb)

    def run(one_pass):
        lag_s[...] = jnp.zeros(lag_s.shape, F32)
        for i in range(n_blk):
            m_s[:, cols(i)] = first_key_scores(i)
            acc_s[:, cols(i)] = jnp.zeros((acc_s.shape[0], cb), F32)
            if l_s is not None:
                l_s[:, cols(i)] = jnp.zeros((1, cb), F32)

        def update(units):
            s_next = units[0][0]()
            for u, (_, values, i) in enumerate(units):
                s = s_next
                if u + 1 < len(units):
                    s_next = units[u + 1][0]()
                _tile_update(s, values(), cols(i), m_s, l_s, acc_s, lag_s, c, one_pass)

        def full_tiles(js):
            return [(functools.partial(tile_scores, j, i), functools.partial(tile_values, j, i), i)
                    for j in js for i in range(n_blk)]

        update([(functools.partial(diag_scores, i), functools.partial(diag_values, i), i) for i in range(n_blk)])

        def pair(jj, carry):
            update(full_tiles((2 * jj, 2 * jj + 1)))
            return carry

        lax.fori_loop(0, n_full // 2, pair, 0)
        if isinstance(n_full, int):
            if n_full % 2:
                update(full_tiles((n_full - 1,)))
        else:
            @pl.when(n_full % 2 == 1)
            def _():
                update(full_tiles((n_full - 1,)))

    run(one_pass=True)

    @pl.when(jnp.max(lag_s[...]) > ATTN_LAG_LIMIT)
    def _():
        run(one_pass=False)


def _block_causal(tq, cb, col0):
    ki = lax.broadcasted_iota(jnp.int32, (tq, cb), 0) // CHUNK
    qi = ((lax.broadcasted_iota(jnp.int32, (tq, cb), 1) + col0) % tq) // CHUNK
    return ki <= qi


def _mla_attn_kernel(q_ref, kfull_ref, vtfull_ref, kdiag_ref, vtdiag_ref, x_ref, wuvt_ref, wout_ref, y_ref,
                     o_s, m_s, l_s, acc_s, lag_s, *, tq, tk, cb, past_tiles, single_block, scale):
    qb = pl.program_id(1)
    heads = q_ref.shape[1]
    n_blk = heads * tq // cb
    heads_per_blk = cb // tq

    def q_blk(i):
        return q_ref[0, i * heads_per_blk:(i + 1) * heads_per_blk].reshape(cb, KCAT)

    def diag_scores(i):
        return jnp.where(_block_causal(tq, cb, i * cb), _dot_nt(kdiag_ref[0], q_blk(i)), -jnp.inf)

    def tile_scores(j, i):
        return _dot_nt(kfull_ref[0, pl.ds(pl.multiple_of(j * tk, tk), tk), :], q_blk(i))

    def first_key_scores(i):
        return _dot_nt(kdiag_ref[0, :BF16_SUBLANES, :], q_blk(i))[:1]

    n_full = past_tiles if single_block else past_tiles + qb * (tq // tk)
    _attend(n_full, n_blk, cb, first_key_scores, diag_scores, lambda i: vtdiag_ref[0, 0],
            tile_scores, lambda j, i: vtfull_ref[0, j], m_s, l_s, acc_s, lag_s, scale * LOG2E)

    for hd in range(heads):
        cs = slice(hd * tq, (hd + 1) * tq)
        o_lat_t = (acc_s[:, cs] * (1.0 / l_s[:, cs])).astype(BF16)
        o_s[hd] = _dot(wuvt_ref[hd], o_lat_t).astype(BF16)
    o_t = o_s[...].reshape(o_s.shape[0] * o_s.shape[1], tq)
    y_ref[0] = x_ref[0] + _dot_tn(o_t, wout_ref[...])


def _mla_attn_call(q, kfull, vtfull, kdiag, vtdiag, x, wuvt, wout, *, tq, tk, scale):
    bsz, heads, seq, _ = q.shape
    d = x.shape[-1]
    lfull = kfull.shape[1]
    nq = seq // tq
    assert seq % tq == 0 and lfull % tk == 0
    assert vtfull.shape == (bsz, lfull // tk, MLA_KV_LORA, tk) and vtdiag.shape == (bsz, nq, MLA_KV_LORA, tq)
    if nq == 1:
        past_tiles = lfull // tk
    else:
        assert tq == tk and lfull == seq
        past_tiles = 0
    cb = max(tq, min(heads * tq, ATTN_COL_BLOCK))
    assert cb % tq == 0 and (heads * tq) % cb == 0
    v_dim = wuvt.shape[1]
    return pl.pallas_call(
        functools.partial(_mla_attn_kernel, tq=tq, tk=tk, cb=cb, past_tiles=past_tiles, single_block=nq == 1,
                          scale=scale),
        out_shape=jax.ShapeDtypeStruct((bsz, seq, d), F32),
        grid=(bsz, nq),
        in_specs=[
            pl.BlockSpec((1, heads, tq, KCAT), lambda b, i: (b, 0, i, 0)),
            pl.BlockSpec((1, lfull, KCAT), lambda b, i: (b, 0, 0)),
            pl.BlockSpec((1, lfull // tk, MLA_KV_LORA, tk), lambda b, i: (b, 0, 0, 0)),
            pl.BlockSpec((1, tq, KCAT), lambda b, i: (b, i, 0)),
            pl.BlockSpec((1, 1, MLA_KV_LORA, tq), lambda b, i: (b, i, 0, 0)),
            pl.BlockSpec((1, tq, d), lambda b, i: (b, i, 0)),
            _resident(wuvt.shape),
            _resident(wout.shape),
        ],
        out_specs=pl.BlockSpec((1, tq, d), lambda b, i: (b, i, 0)),
        scratch_shapes=[
            pltpu.VMEM((heads, v_dim, tq), BF16),
            pltpu.VMEM((1, heads * tq), F32),
            pltpu.VMEM((1, heads * tq), F32),
            pltpu.VMEM((MLA_KV_LORA, heads * tq), F32),
            pltpu.VMEM((1, cb), F32),
        ],
        compiler_params=pltpu.CompilerParams(
            dimension_semantics=("parallel", "parallel"), vmem_limit_bytes=VMEM_LIMIT),
        name="mla_attn",
    )(q, kfull, vtfull, kdiag, vtdiag, x, wuvt, wout)


def _mha_attn_kernel(qt_ref, k_ref, vt_ref, kdiag_ref, vtdiag_ref, ot_ref, m_s, acc_s, lag_s, *, tq):
    qb = pl.program_id(2)
    hp = qt_ref.shape[1]
    visible = _block_causal(tq, tq, 0)

    def diag_scores(i):
        return jnp.where(visible, _dot(kdiag_ref[0, i], qt_ref[0, i, 0]), -jnp.inf)

    def tile_scores(j, i):
        return _dot(k_ref[0, i, pl.ds(pl.multiple_of(j * tq, tq), tq), :], qt_ref[0, i, 0])

    def first_key_scores(i):
        return _dot(kdiag_ref[0, i, :BF16_SUBLANES, :], qt_ref[0, i, 0])[:1]

    _attend(qb, hp, tq, first_key_scores, diag_scores, lambda i: vtdiag_ref[0, i, 0],
            tile_scores, lambda j, i: vt_ref[0, i, j], m_s, None, acc_s, lag_s, 1.0)

    for i in range(hp):
        cs = slice(i * tq, (i + 1) * tq)
        inv_l = 1.0 / acc_s[MLA_V:MLA_V + 1, cs]
        ot_ref[0, i * MLA_V:(i + 1) * MLA_V, :] = (acc_s[:MLA_V, cs] * inv_l).astype(BF16)


def _mha_attn_call(qt, k, vt, *, hp, tq):
    bsz, heads, seq, qk_dim = k.shape
    nq = seq // tq
    v_rows = MLA_V + V_ONES_ROWS
    assert qt.shape == (bsz, heads, nq, qk_dim, tq)
    assert seq % tq == 0 and heads % hp == 0 and vt.shape == (bsz, heads, nq, v_rows, tq)
    return pl.pallas_call(
        functools.partial(_mha_attn_kernel, tq=tq),
        out_shape=jax.ShapeDtypeStruct((bsz, heads * MLA_V, seq), BF16),
        grid=(bsz, heads // hp, nq),
        in_specs=[
            pl.BlockSpec((1, hp, 1, qk_dim, tq), lambda b, g, i: (b, g, i, 0, 0)),
            pl.BlockSpec((1, hp, seq, qk_dim), lambda b, g, i: (b, g, 0, 0)),
            pl.BlockSpec((1, hp, nq, v_rows, tq), lambda b, g, i: (b, g, 0, 0, 0)),
            pl.BlockSpec((1, hp, tq, qk_dim), lambda b, g, i: (b, g, i, 0)),
            pl.BlockSpec((1, hp, 1, v_rows, tq), lambda b, g, i: (b, g, i, 0, 0)),
        ],
        out_specs=pl.BlockSpec((1, hp * MLA_V, tq), lambda b, g, i: (b, g, i)),
        scratch_shapes=[
            pltpu.VMEM((1, hp * tq), F32),
            pltpu.VMEM((v_rows, hp * tq), F32),
            pltpu.VMEM((1, tq), F32),
        ],
        compiler_params=pltpu.CompilerParams(
            dimension_semantics=("parallel", "parallel", "arbitrary"), vmem_limit_bytes=VMEM_LIMIT),
        name="mha_attn",
    )(qt, k, vt, k, vt)


def _rope_tables(pos):
    half = MLA_ROPE // 2
    inv = ROPE_THETA ** (-jnp.arange(half, dtype=F32) / half)
    ang = pos.astype(F32)[:, None] * inv[None, :]
    pad = jnp.zeros((pos.shape[0], LANES - MLA_ROPE), F32)
    cos = jnp.concatenate([jnp.cos(ang), jnp.cos(ang), pad], axis=-1)
    sin = jnp.concatenate([jnp.sin(ang), jnp.sin(ang), pad], axis=-1)
    return cos, sin


def _rotate_half_cols(w):
    half = w.shape[-1] // 2
    return jnp.concatenate([-w[..., half:], w[..., :half]], axis=-1)


def _pad_last(w, n):
    return jnp.concatenate([w, jnp.zeros(w.shape[:-1] + (n - w.shape[-1],), w.dtype)], axis=-1)


def _prep_gla(w_in, w_gk_up, b_gk, g_norm, w_out):
    qk = w_gk_up.shape[1]
    vw = w_out.shape[0]
    wq = w_in[:, :qk]
    wk = w_in[:, qk:2 * qk]
    wv = w_in[:, 2 * qk:2 * qk + vw]
    wg = w_in[:, 2 * qk + vw:2 * qk + 2 * vw]
    wlow = _pad_last(w_in[:, 2 * qk + 2 * vw:], LANES)
    rank = w_gk_up.shape[0]
    wgk = jnp.concatenate([w_gk_up, jnp.zeros((LANES - rank, qk), w_gk_up.dtype)], axis=0)
    return (wq.astype(BF16), wk.astype(BF16), wv.astype(BF16), wg.astype(BF16), wlow.astype(BF16),
            wgk.astype(BF16), b_gk[None, :], g_norm[None, :], w_out.astype(BF16))


def _prep_mla(w_down, q_norm, w_uq, kv_norm, w_uk, w_uv, w_out):
    wdq = w_down[:, :MLA_Q_LORA]
    wdkv = w_down[:, MLA_Q_LORA:MLA_Q_LORA + MLA_KV_LORA]
    wdk = w_down[:, MLA_Q_LORA + MLA_KV_LORA:]
    wdkr = _rotate_half_cols(wdk)
    wuq = w_uq.reshape(MLA_Q_LORA, MLA_HEADS, MLA_NOPE + MLA_ROPE)
    wnope = wuq[..., :MLA_NOPE].reshape(MLA_Q_LORA, MLA_HEADS * MLA_NOPE)
    wrope = wuq[..., MLA_NOPE:]
    wr = _pad_last(wrope, LANES).reshape(MLA_Q_LORA, MLA_HEADS * LANES)
    wrr = _pad_last(_rotate_half_cols(wrope), LANES).reshape(MLA_Q_LORA, MLA_HEADS * LANES)
    wuk = jnp.transpose(w_uk, (1, 2, 0))
    wuvt = jnp.transpose(w_uv, (1, 2, 0))
    wuk_flat = w_uk.reshape(MLA_KV_LORA, MLA_HEADS * MLA_NOPE)
    wuv_flat_t = w_uv.reshape(MLA_KV_LORA, MLA_HEADS * MLA_V).T
    return dict(
        wdq=wdq.astype(BF16), wdkv=wdkv.astype(BF16), wdk=_pad_last(wdk, LANES).astype(BF16),
        wdkr=_pad_last(wdkr, LANES).astype(BF16), qn=q_norm[None, :], kvn=kv_norm[None, :],
        wnope=wnope.astype(BF16), wr=wr.astype(BF16), wrr=wrr.astype(BF16), wuk=wuk.astype(BF16),
        wuvt=wuvt.astype(BF16), wuk_flat=wuk_flat.astype(BF16), wuv_flat_t=wuv_flat_t.astype(BF16),
        wout=w_out.astype(BF16))


def kernel(x_prompt, x_sample, state_gla, cache_ckv, cache_kpe, norm_ffn1, w_ffn1_gate, w_ffn1_up, w_ffn1_down, norm_mix, norm_ffn2, w_ffn2_gate, w_ffn2_up, w_ffn2_down, w_gla_in, w_gla_gk_up, b_gla_gk, gla_out_norm, w_gla_out, w_mla_down, mla_q_norm, w_mla_uq, mla_kv_norm, w_mla_uk, w_mla_uv, w_mla_out, norm_final):
    depth = norm_ffn1.shape[0]
    n_mixers = 2
    d = x_prompt.shape[-1]
    mla_scale = (MLA_NOPE + MLA_ROPE) ** -0.5
    gf = norm_final[None, :]

    ffn1 = (norm_ffn1[:, None, :], w_ffn1_gate, w_ffn1_up, w_ffn1_down)
    ffn2 = (norm_ffn2[:, None, :], w_ffn2_gate, w_ffn2_up, w_ffn2_down)
    gla_w = [_prep_gla(w_gla_in[j], w_gla_gk_up[j], b_gla_gk[j], gla_out_norm[j], w_gla_out[j])
             for j in range(w_gla_in.shape[0])]
    mla_w = [_prep_mla(w_mla_down[j], mla_q_norm[j], w_mla_uq[j], mla_kv_norm[j], w_mla_uk[j], w_mla_uv[j],
                       w_mla_out[j]) for j in range(w_mla_down.shape[0])]

    def run(x, pos, gla_s0, past_ckv, past_kpe, *, gla_nb, gla_tl, attn_tq):
        bsz, seq, _ = x.shape
        cos, sin = _rope_tables(pos)
        new_gla, new_ckv, new_kpe = [], [], []
        pending = {}

        def ffn(x, w, layer, final_norm=False):
            return _ffn_call(x.reshape(bsz * seq, d), w[0], w[1], w[2], w[3], gf, layer,
                             final_norm=final_norm, **pending).reshape(bsz, seq, d)

        for i in range(depth):
            x = ffn(x, ffn1, i)
            j = i // n_mixers
            gm = norm_mix[i][None, :]
            if i % n_mixers == 0:
                x, s_fin = _gla_call(x, gla_s0[j], gm, *gla_w[j], nb=gla_nb, tl=gla_tl)
                new_gla.append(s_fin)
            else:
                w = mla_w[j]
                absorb = past_ckv is not None
                q, k, vt, ckv, kpe = _mla_proj_call(
                    x, cos, sin, gm, w["wdq"], w["wdkv"], w["wdk"], w["wdkr"], w["qn"], w["kvn"],
                    w["wnope"], w["wr"], w["wrr"], w["wuk"] if absorb else w["wuk_flat"], w["wuv_flat_t"],
                    tm=attn_tq, absorb=absorb, q_scale=1.0 if absorb else mla_scale * LOG2E)
                if absorb:
                    pc = past_ckv[j].astype(BF16)
                    pad = jnp.zeros(past_kpe[j].shape[:-1] + (KCAT - MLA_KV_LORA - MLA_ROPE,), BF16)
                    kfull = jnp.concatenate([pc, past_kpe[j].astype(BF16), pad], axis=-1)
                    vtfull = jnp.swapaxes(pc.reshape(bsz, -1, ATTN_KEY_TILE, MLA_KV_LORA), 2, 3)
                    x = _mla_attn_call(q, kfull, vtfull, k, vt, x, w["wuvt"], w["wout"],
                                       tq=attn_tq, tk=ATTN_KEY_TILE, scale=mla_scale)
                else:
                    qt = jnp.swapaxes(q.reshape(bsz, MLA_HEADS, seq // attn_tq, attn_tq, q.shape[-1]), 3, 4)
                    pending = dict(attn_t=_mha_attn_call(qt, k, vt, hp=ATTN_HEAD_GROUP, tq=attn_tq), w_out=w["wout"])
                new_ckv.append(ckv)
                new_kpe.append(kpe)
            x = ffn(x, ffn2, i, final_norm=(i == depth - 1))
            pending = {}
        return x, jnp.stack(new_gla), jnp.stack(new_ckv), jnp.stack(new_kpe)

    bp, lp = x_prompt.shape[0], x_prompt.shape[1]
    n_gla = w_gla_in.shape[0]
    s0_p = jnp.zeros((n_gla, bp) + state_gla.shape[2:], x_prompt.dtype)
    y_p, gla_p, ckv_p, kpe_p = run(
        x_prompt, jnp.arange(lp), s0_p, None, None, gla_nb=1, gla_tl=512, attn_tq=ATTN_KEY_TILE)

    past_len = cache_ckv.shape[2]
    ls = x_sample.shape[1]
    y_s, gla_s, ckv_s, kpe_s = run(
        x_sample, past_len + jnp.arange(ls), state_gla, cache_ckv, cache_kpe, gla_nb=8, gla_tl=ls, attn_tq=ls)

    return (y_p, y_s, gla_p, ckv_p, kpe_p, gla_s, ckv_s, kpe_s)
```

```python
import functools

import jax
import jax.numpy as jnp
from jax import lax
from jax.experimental import pallas as pl
from jax.experimental.pallas import tpu as pltpu

F32 = jnp.float32
BF16 = jnp.bfloat16

EPS = 1e-6
CHUNK = 64
GLA_HEADS = 4
GLA_GATE_NORM = 16.0
GLA_ROW_GROUPS = 2
MLA_HEADS = 8
MLA_NOPE = 128
MLA_ROPE = 64
MLA_V = 128
MLA_Q_LORA = 384
MLA_KV_LORA = 256
ROPE_THETA = 10000.0
LANES = 128
BF16_SUBLANES = 16
F32_SUBLANES = 8
MXU_COLS = 256
LOG2E = 1.4426950408889634
ATTN_COL_BLOCK = 512
ATTN_KEY_TILE = 512
ATTN_HEAD_GROUP = 4
V_ONES_ROWS = 16
ATTN_LAG_LIMIT = 64.0
KCAT = MLA_KV_LORA + LANES

VMEM_LIMIT = 56 * 1024 * 1024


def _rms(xf, g):
    return xf * lax.rsqrt(jnp.mean(xf * xf, axis=-1, keepdims=True) + EPS) * g


def _silu(a):
    return a * (1.0 / (1.0 + jnp.exp(-a)))


def _dot(a, b):
    return jnp.dot(a, b, preferred_element_type=F32)


def _dot_nt(a, b):
    return lax.dot_general(a, b, (((1,), (1,)), ((), ())), preferred_element_type=F32)


def _dot_tn(a, b):
    return lax.dot_general(a, b, (((0,), (0,)), ((), ())), preferred_element_type=F32)


def _resident(shape):
    zeros = (0,) * len(shape)
    return pl.BlockSpec(shape, lambda *_: zeros, pipeline_mode=pl.Buffered(1))


def _ffn_kernel(*refs, f_chunk, final_norm, mixer_proj):
    if mixer_proj:
        x_ref, ot_ref, wo_ref, g_ref, wg_ref, wu_ref, wd_ref, gf_ref, o_ref, hid_ref = refs
        x = x_ref[...] + _dot_tn(ot_ref[0], wo_ref[...])
    else:
        x_ref, g_ref, wg_ref, wu_ref, wd_ref, gf_ref, o_ref, hid_ref = refs
        x = x_ref[...]
    h = _rms(x, g_ref[0]).astype(BF16)
    d_ff = wg_ref.shape[2]
    for c in range(d_ff // f_chunk):
        sl = slice(c * f_chunk, (c + 1) * f_chunk)
        a = _dot(h, wg_ref[0, :, sl].astype(BF16))
        b = _dot(h, wu_ref[0, :, sl].astype(BF16))
        hid_ref[:, sl] = (_silu(a) * b).astype(BF16)
    y = x + 0.5 * _dot(hid_ref[...], wd_ref[0].astype(BF16))
    if final_norm:
        y = _rms(y, gf_ref[...])
    o_ref[...] = y


def _layer_resident(shape, layer):
    index = (layer,) + (0,) * (len(shape) - 1)
    return pl.BlockSpec((1,) + tuple(shape[1:]), lambda *_: index, pipeline_mode=pl.Buffered(1))


def _ffn_call(x, g, wg, wu, wd, gf, layer, *, final_norm, attn_t=None, w_out=None, tm=512, f_chunk=256):
    t, d = x.shape
    d_ff = wg.shape[2]
    assert t % tm == 0 and d_ff % f_chunk == 0
    mixer_proj = attn_t is not None
    x_spec = pl.BlockSpec((tm, d), lambda i: (i, 0))
    weights = (g, wg, wu, wd)
    if mixer_proj:
        tiles_per_seq = attn_t.shape[2] // tm
        assert attn_t.shape[0] * attn_t.shape[2] == t and attn_t.shape[2] % tm == 0
        ot_spec = pl.BlockSpec((1, attn_t.shape[1], tm), lambda i: (i // tiles_per_seq, 0, i % tiles_per_seq))
        operands = (x, attn_t, w_out) + weights + (gf,)
        in_specs = [x_spec, ot_spec, _resident(w_out.shape)]
    else:
        operands = (x,) + weights + (gf,)
        in_specs = [x_spec]
    return pl.pallas_call(
        functools.partial(_ffn_kernel, f_chunk=f_chunk, final_norm=final_norm, mixer_proj=mixer_proj),
        out_shape=jax.ShapeDtypeStruct((t, d), F32),
        grid=(t // tm,),
        in_specs=in_specs + [_layer_resident(w.shape, layer) for w in weights] + [_resident(gf.shape)],
        out_specs=pl.BlockSpec((tm, d), lambda i: (i, 0)),
        scratch_shapes=[pltpu.VMEM((tm, d_ff), BF16)],
        compiler_params=pltpu.CompilerParams(
            dimension_semantics=("parallel",), vmem_limit_bytes=VMEM_LIMIT),
        name="ffn",
    )(*operands)


def _split3(a):
    hi = a.astype(BF16)
    r = a - hi.astype(F32)
    mid = r.astype(BF16)
    lo = (r - mid.astype(F32)).astype(BF16)
    return hi, mid, lo


def _interleave(primary, filler):
    for n, step in enumerate(primary):
        step()
        if n < len(filler):
            filler[n]()
    for extra in filler[len(primary):]:
        extra()


def _gla_kernel(x_ref, s0_ref, gmix_ref, wq_ref, wk_ref, wv_ref, wg_ref, wlow_ref, wgk_ref, bgk_ref,
                gout_ref, wout_ref, y_ref, s_ref, st_s, *, nb, tl, n_sub):
    l = pl.program_id(1)
    heads = GLA_HEADS
    dk = wq_ref.shape[1] // heads
    dv = wv_ref.shape[1] // heads
    rows_sub = nb * tl // n_sub
    n_chunk = rows_sub // CHUNK
    d = x_ref.shape[-1]

    @pl.when(l == 0)
    def _():
        st_s[...] = s0_ref[...]

    ti = lax.broadcasted_iota(jnp.int32, (CHUNK, CHUNK), 0)
    si = lax.broadcasted_iota(jnp.int32, (CHUNK, CHUNK), 1)
    causal = si <= ti
    tri = jnp.where(causal, 1.0, 0.0).astype(BF16)

    def rows_of(ref, sub):
        row0 = sub * rows_sub
        if rows_sub >= tl:
            return ref.at[row0 // tl:(row0 + rows_sub) // tl]
        return ref.at[row0 // tl, row0 % tl:row0 % tl + rows_sub, :]

    def in_projection(sub):
        env = {}

        def start():
            env["x"] = rows_of(x_ref, sub)[...].reshape(rows_sub, d)
            env["h"] = _rms(env["x"], gmix_ref[...]).astype(BF16)

        def piece(name, ref, lo, hi):
            return lambda: env.setdefault(name, []).append(_dot(env["h"], ref[:, lo:hi]))

        thunks = [start]
        for name, ref in (("low", wlow_ref), ("q", wq_ref), ("k", wk_ref), ("v", wv_ref), ("g", wg_ref)):
            width = ref.shape[1]
            thunks += [piece(name, ref, lo, min(lo + MXU_COLS, width)) for lo in range(0, width, MXU_COLS)]
        return thunks, env

    def recurrence(sub, env):
        row0 = sub * rows_sub
        cat = lambda name: jnp.concatenate(env[name], axis=-1)
        loc = {}

        def start():
            loc["q"] = cat("q") * (dk ** -0.5)
            loc["k"] = cat("k")
            loc["v"] = cat("v").astype(BF16)
            z = _dot(cat("low").astype(BF16), wgk_ref[...]) + bgk_ref[...]
            loc["log_a"] = -(jnp.maximum(-z, 0.0) + jnp.log1p(jnp.exp(-jnp.abs(z)))) / GLA_GATE_NORM
            loc["state"] = {}
            loc["o"] = [[None] * n_chunk for _ in range(heads)]

        def cumsum(c):
            rs = slice(c * CHUNK, (c + 1) * CHUNK)
            a_hi, a_mid, a_lo = _split3(loc["log_a"][rs, :])
            loc["cum", c] = _dot(tri, a_hi) + _dot(tri, a_mid) + _dot(tri, a_lo)

        def scores(c):
            rs = slice(c * CHUNK, (c + 1) * CHUNK)
            cum = loc["cum", c]
            cum_last = cum[CHUNK - 1:CHUNK, :]
            q_dec = (loc["q"][rs, :] * jnp.exp(cum)).astype(BF16)
            k_inv = (loc["k"][rs, :] * jnp.exp(-cum)).astype(BF16)
            loc["k_tail", c] = (loc["k"][rs, :] * jnp.exp(cum_last - cum)).astype(BF16)
            loc["q_dec", c] = q_dec
            dec = jnp.broadcast_to(jnp.exp(cum_last), (F32_SUBLANES, heads * dk))
            loc["dec", c] = [dec[:, hd * dk:(hd + 1) * dk].T[:, :1] for hd in range(heads)]
            loc["scores", c] = [
                jnp.where(causal, _dot_nt(q_dec[:, hd * dk:(hd + 1) * dk], k_inv[:, hd * dk:(hd + 1) * dk]), 0.0)
                .astype(BF16) for hd in range(heads)]

        def values(c):
            rs = slice(c * CHUNK, (c + 1) * CHUNK)
            vs = [loc["v"][rs, hd * dv:(hd + 1) * dv] for hd in range(heads)]
            loc["o_intra", c] = [_dot(loc["scores", c][hd], vs[hd]) for hd in range(heads)]
            loc["kv", c] = [_dot_tn(loc["k_tail", c][:, hd * dk:(hd + 1) * dk], vs[hd]) for hd in range(heads)]

        def carry(c):
            b = (row0 + c * CHUNK) // tl
            for hd in range(heads):
                ks = slice(hd * dk, (hd + 1) * dk)
                st = loc["state"].get((b, hd))
                if st is None:
                    st = st_s[b, hd]
                loc["o"][hd][c] = loc["o_intra", c][hd] + _dot(loc["q_dec", c][:, ks], st.astype(BF16))
                loc["state"][(b, hd)] = st * loc["dec", c][hd] + loc["kv", c][hd]

        def finish():
            for (b, hd), st in loc["state"].items():
                st_s[b, hd] = st
            env["o"] = [jnp.concatenate(loc["o"][hd], axis=0) for hd in range(heads)]

        thunks = [start]
        for c in range(n_chunk):
            thunks += [functools.partial(cumsum, c), functools.partial(scores, c), functools.partial(values, c)]
        thunks += [functools.partial(carry, c) for c in range(n_chunk)] + [finish]
        return thunks

    def out_projection(sub, env):
        loc = {}

        def gate():
            g = jnp.concatenate(env["g"], axis=-1)
            loc["gated"] = jnp.concatenate(
                [(_rms(env["o"][hd], gout_ref[...]) * _silu(g[:, hd * dv:(hd + 1) * dv])).astype(BF16)
                 for hd in range(heads)], axis=-1)

        def piece(lo):
            return lambda: loc.setdefault("y", []).append(_dot(loc["gated"], wout_ref[:, lo:lo + MXU_COLS]))

        def store():
            y = env["x"] + jnp.concatenate(loc["y"], axis=-1)
            dst = rows_of(y_ref, sub)
            dst[...] = y.reshape(dst.shape)

        return [gate] + [piece(lo) for lo in range(0, d, MXU_COLS)] + [store]

    proj = [in_projection(sub) for sub in range(n_sub)]
    for step in proj[0][0]:
        step()
    for sub in range(n_sub):
        filler = []
        if sub + 1 < n_sub:
            filler += proj[sub + 1][0]
        if sub >= 1:
            filler += out_projection(sub - 1, proj[sub - 1][1])
        _interleave(recurrence(sub, proj[sub][1]), filler)
    for step in out_projection(n_sub - 1, proj[n_sub - 1][1]):
        step()

    @pl.when(l == pl.num_programs(1) - 1)
    def _():
        s_ref[...] = st_s[...]


def _gla_call(x, s0, gmix, wq, wk, wv, wg, wlow, wgk, bgk, gout, wout, *, nb, tl, n_sub=GLA_ROW_GROUPS):
    bsz, seq, d = x.shape
    heads = GLA_HEADS
    dk = wq.shape[1] // heads
    dv = wv.shape[1] // heads
    rows_sub = nb * tl // n_sub
    assert bsz % nb == 0 and seq % tl == 0 and tl % CHUNK == 0 and (nb * tl) % n_sub == 0
    assert rows_sub % CHUNK == 0 and (rows_sub % tl == 0 or tl % rows_sub == 0)
    return pl.pallas_call(
        functools.partial(_gla_kernel, nb=nb, tl=tl, n_sub=n_sub),
        out_shape=(jax.ShapeDtypeStruct((bsz, seq, d), F32),
                   jax.ShapeDtypeStruct((bsz, heads, dk, dv), F32)),
        grid=(bsz // nb, seq // tl),
        in_specs=[
            pl.BlockSpec((nb, tl, d), lambda b, l: (b, l, 0)),
            pl.BlockSpec((nb, heads, dk, dv), lambda b, l: (b, 0, 0, 0)),
            _resident(gmix.shape), _resident(wq.shape), _resident(wk.shape), _resident(wv.shape),
            _resident(wg.shape), _resident(wlow.shape), _resident(wgk.shape), _resident(bgk.shape),
            _resident(gout.shape), _resident(wout.shape),
        ],
        out_specs=(pl.BlockSpec((nb, tl, d), lambda b, l: (b, l, 0)),
                   pl.BlockSpec((nb, heads, dk, dv), lambda b, l: (b, 0, 0, 0))),
        scratch_shapes=[pltpu.VMEM((nb, heads, dk, dv), F32)],
        compiler_params=pltpu.CompilerParams(
            dimension_semantics=("parallel", "arbitrary"), vmem_limit_bytes=VMEM_LIMIT),
        name="gla",
    )(x, s0, gmix, wq, wk, wv, wg, wlow, wgk, bgk, gout, wout)


def _mla_proj_kernel(x_ref, cos_ref, sin_ref, gmix_ref, wdq_ref, wdkv_ref, wdk_ref, wdkr_ref, qn_ref, kvn_ref,
                     wnope_ref, wr_ref, wrr_ref, wkey_ref, wval_ref,
                     q_ref, k_ref, vt_ref, ckv_ref, kpe_ref, *, absorb, q_scale):
    x = x_ref[0]
    h = _rms(x, gmix_ref[...]).astype(BF16)
    cos = cos_ref[...]
    sin = sin_ref[...]

    ckv = _rms(_dot(h, wdkv_ref[...]), kvn_ref[...])
    kpe = _dot(h, wdk_ref[...]) * cos + _dot(h, wdkr_ref[...]) * sin
    ckv_ref[0] = ckv
    kpe_ref[0] = kpe[:, :MLA_ROPE]
    ckv_b = ckv.astype(BF16)
    kpe_b = kpe.astype(BF16)
    ckv_t = ckv.T.astype(BF16)

    hq = _rms(_dot(h, wdq_ref[...]), qn_ref[...]).astype(BF16)
    q_nope = _dot(hq, wnope_ref[...])
    q_r = _dot(hq, wr_ref[...])
    q_rr = _dot(hq, wrr_ref[...])

    if absorb:
        k_ref[0, :, :MLA_KV_LORA] = ckv_b
        k_ref[0, :, MLA_KV_LORA:] = kpe_b
        vt_ref[0, 0] = ckv_t
    else:
        k_nope = _dot(ckv_b, wkey_ref[...]).astype(BF16)
        v_t = _dot(wval_ref[...], ckv_t).astype(BF16)
        ones = jnp.ones((V_ONES_ROWS, x.shape[0]), BF16)
    for hd in range(MLA_HEADS):
        hs = slice(hd * LANES, (hd + 1) * LANES)
        q_pe = q_r[:, hs] * cos + q_rr[:, hs] * sin
        if absorb:
            q_ref[0, hd, :, :MLA_KV_LORA] = _dot(q_nope[:, hs].astype(BF16), wkey_ref[hd]).astype(BF16)
            q_ref[0, hd, :, MLA_KV_LORA:] = q_pe.astype(BF16)
        else:
            q_ref[0, hd, :, :MLA_NOPE] = (q_nope[:, hs] * q_scale).astype(BF16)
            q_ref[0, hd, :, MLA_NOPE:] = (q_pe * q_scale).astype(BF16)
            k_ref[0, hd, :, :MLA_NOPE] = k_nope[:, hs]
            k_ref[0, hd, :, MLA_NOPE:] = kpe_b
            vt_ref[0, hd, 0, :MLA_V] = v_t[hd * MLA_V:(hd + 1) * MLA_V, :]
            vt_ref[0, hd, 0, MLA_V:] = ones


def _mla_proj_call(x, cos, sin, gmix, wdq, wdkv, wdk, wdkr, qn, kvn, wnope, wr, wrr, wkey, wval, *, tm, absorb,
                   q_scale=1.0):
    bsz, seq, d = x.shape
    assert seq % tm == 0
    weights = (gmix, wdq, wdkv, wdk, wdkr, qn, kvn, wnope, wr, wrr, wkey, wval)
    nt = seq // tm
    if absorb:
        qk_dim = KCAT
        q_k_vt = (jax.ShapeDtypeStruct((bsz, MLA_HEADS, seq, qk_dim), BF16),
                  jax.ShapeDtypeStruct((bsz, seq, qk_dim), BF16),
                  jax.ShapeDtypeStruct((bsz, nt, MLA_KV_LORA, tm), BF16))
        q_k_vt_specs = (pl.BlockSpec((1, MLA_HEADS, tm, qk_dim), lambda b, l: (b, 0, l, 0)),
                        pl.BlockSpec((1, tm, qk_dim), lambda b, l: (b, l, 0)),
                        pl.BlockSpec((1, 1, MLA_KV_LORA, tm), lambda b, l: (b, l, 0, 0)))
    else:
        qk_dim = MLA_NOPE + LANES
        q_k_vt = (jax.ShapeDtypeStruct((bsz, MLA_HEADS, seq, qk_dim), BF16),
                  jax.ShapeDtypeStruct((bsz, MLA_HEADS, seq, qk_dim), BF16),
                  jax.ShapeDtypeStruct((bsz, MLA_HEADS, nt, MLA_V + V_ONES_ROWS, tm), BF16))
        q_k_vt_specs = (pl.BlockSpec((1, MLA_HEADS, tm, qk_dim), lambda b, l: (b, 0, l, 0)),
                        pl.BlockSpec((1, MLA_HEADS, tm, qk_dim), lambda b, l: (b, 0, l, 0)),
                        pl.BlockSpec((1, MLA_HEADS, 1, MLA_V + V_ONES_ROWS, tm), lambda b, l: (b, 0, l, 0, 0)))
    return pl.pallas_call(
        functools.partial(_mla_proj_kernel, absorb=absorb, q_scale=q_scale),
        out_shape=q_k_vt + (jax.ShapeDtypeStruct((bsz, seq, MLA_KV_LORA), F32),
                            jax.ShapeDtypeStruct((bsz, seq, MLA_ROPE), F32)),
        grid=(bsz, nt),
        in_specs=[
            pl.BlockSpec((1, tm, d), lambda b, l: (b, l, 0)),
            pl.BlockSpec((tm, LANES), lambda b, l: (l, 0)),
            pl.BlockSpec((tm, LANES), lambda b, l: (l, 0)),
        ] + [_resident(w.shape) for w in weights],
        out_specs=q_k_vt_specs + (pl.BlockSpec((1, tm, MLA_KV_LORA), lambda b, l: (b, l, 0)),
                                  pl.BlockSpec((1, tm, MLA_ROPE), lambda b, l: (b, l, 0))),
        compiler_params=pltpu.CompilerParams(
            dimension_semantics=("parallel", "parallel"), vmem_limit_bytes=VMEM_LIMIT),
        name="mla_proj",
    )(x, cos, sin, *weights)


def _exp2_scaled(t, c):
    return jnp.exp2(t if c == 1.0 else t * c)


def _tile_update(s, vt, cs, m_s, l_s, acc_s, lag_s, c, one_pass):
    m_prev = m_s[:, cs]
    mx = jnp.max(s, axis=0, keepdims=True)
    m_new = jnp.maximum(m_prev, mx)
    alpha = _exp2_scaled(m_prev - m_new, c)
    if one_pass:
        p = _exp2_scaled(s - m_prev, c)
        if l_s is not None:
            l_s[:, cs] = (l_s[:, cs] + jnp.sum(p, axis=0, keepdims=True)) * alpha
        acc_s[:, cs] = (acc_s[:, cs] + _dot(vt, p.astype(BF16))) * alpha
        lag_s[...] = jnp.maximum(lag_s[...], (mx - m_prev) * c)
    else:
        p = _exp2_scaled(s - m_new, c)
        if l_s is not None:
            l_s[:, cs] = alpha * l_s[:, cs] + jnp.sum(p, axis=0, keepdims=True)
        acc_s[:, cs] = acc_s[:, cs] * alpha + _dot(vt, p.astype(BF16))
    m_s[:, cs] = m_new


def _attend(n_full, n_blk, cb, first_key_scores, diag_scores, diag_values, tile_scores, tile_values,
            m_s, l_s, acc_s, lag_s, c):
    def cols(i):
        return slice(i * cb, (i + 1) * cb)

    def run(one_pass):
        lag_s[...] = jnp.zeros(lag_s.shape, F32)
        for i in range(n_blk):
            m_s[:, cols(i)] = first_key_scores(i)
            acc_s[:, cols(i)] = jnp.zeros((acc_s.shape[0], cb), F32)
            if l_s is not None:
                l_s[:, cols(i)] = jnp.zeros((1, cb), F32)

        def update(units):
            s_next = units[0][0]()
            for u, (_, values, i) in enumerate(units):
                s = s_next
                if u + 1 < len(units):
                    s_next = units[u + 1][0]()
                _tile_update(s, values(), cols(i), m_s, l_s, acc_s, lag_s, c, one_pass)

        def full_tiles(js):
            return [(functools.partial(tile_scores, j, i), functools.partial(tile_values, j, i), i)
                    for j in js for i in range(n_blk)]

        update([(functools.partial(diag_scores, i), functools.partial(diag_values, i), i) for i in range(n_blk)])

        def pair(jj, carry):
            update(full_tiles((2 * jj, 2 * jj + 1)))
            return carry

        lax.fori_loop(0, n_full // 2, pair, 0)
        if isinstance(n_full, int):
            if n_full % 2:
                update(full_tiles((n_full - 1,)))
        else:
            @pl.when(n_full % 2 == 1)
            def _():
                update(full_tiles((n_full - 1,)))

    run(one_pass=True)

    @pl.when(jnp.max(lag_s[...]) > ATTN_LAG_LIMIT)
    def _():
        run(one_pass=False)


def _block_causal(tq, cb, col0):
    ki = lax.broadcasted_iota(jnp.int32, (tq, cb), 0) // CHUNK
    qi = ((lax.broadcasted_iota(jnp.int32, (tq, cb), 1) + col0) % tq) // CHUNK
    return ki <= qi


def _mla_attn_kernel(qt_ref, kfull_ref, vtfull_ref, kdiag_ref, vtdiag_ref, x_ref, wuvt_ref, wout_ref, y_ref,
                     o_s, m_s, l_s, acc_s, lag_s, *, tq, tk, cb, past_tiles, single_block, scale):
    qb = pl.program_id(1)
    heads = o_s.shape[0]
    n_blk = heads * tq // cb

    def qt_blk(i):
        return qt_ref[0, 0, :, i * cb:(i + 1) * cb]

    def diag_scores(i):
        return jnp.where(_block_causal(tq, cb, i * cb), _dot(kdiag_ref[0], qt_blk(i)), -jnp.inf)

    def tile_scores(j, i):
        return _dot(kfull_ref[0, pl.ds(pl.multiple_of(j * tk, tk), tk), :], qt_blk(i))

    def first_key_scores(i):
        return _dot(kdiag_ref[0, :BF16_SUBLANES, :], qt_blk(i))[:1]

    n_full = past_tiles if single_block else past_tiles + qb * (tq // tk)
    _attend(n_full, n_blk, cb, first_key_scores, diag_scores, lambda i: vtdiag_ref[0, 0],
            tile_scores, lambda j, i: vtfull_ref[0, j], m_s, l_s, acc_s, lag_s, scale * LOG2E)

    for hd in range(heads):
        cs = slice(hd * tq, (hd + 1) * tq)
        o_lat_t = (acc_s[:, cs] * (1.0 / l_s[:, cs])).astype(BF16)
        o_s[hd] = _dot(wuvt_ref[hd], o_lat_t).astype(BF16)
    o_t = o_s[...].reshape(o_s.shape[0] * o_s.shape[1], tq)
    y_ref[0] = x_ref[0] + _dot_tn(o_t, wout_ref[...])


def _mla_attn_call(qt, kfull, vtfull, kdiag, vtdiag, x, wuvt, wout, *, tq, tk, scale):
    bsz, seq, d = x.shape
    heads = wuvt.shape[0]
    lfull = kfull.shape[1]
    nq = seq // tq
    assert seq % tq == 0 and lfull % tk == 0 and qt.shape == (bsz, nq, KCAT, heads * tq)
    assert vtfull.shape == (bsz, lfull // tk, MLA_KV_LORA, tk) and vtdiag.shape == (bsz, nq, MLA_KV_LORA, tq)
    if nq == 1:
        past_tiles = lfull // tk
    else:
        assert tq == tk and lfull == seq
        past_tiles = 0
    cb = max(tq, min(heads * tq, ATTN_COL_BLOCK))
    assert cb % tq == 0 and (heads * tq) % cb == 0
    v_dim = wuvt.shape[1]
    return pl.pallas_call(
        functools.partial(_mla_attn_kernel, tq=tq, tk=tk, cb=cb, past_tiles=past_tiles, single_block=nq == 1,
                          scale=scale),
        out_shape=jax.ShapeDtypeStruct((bsz, seq, d), F32),
        grid=(bsz, nq),
        in_specs=[
            pl.BlockSpec((1, 1, KCAT, heads * tq), lambda b, i: (b, i, 0, 0)),
            pl.BlockSpec((1, lfull, KCAT), lambda b, i: (b, 0, 0)),
            pl.BlockSpec((1, lfull // tk, MLA_KV_LORA, tk), lambda b, i: (b, 0, 0, 0)),
            pl.BlockSpec((1, tq, KCAT), lambda b, i: (b, i, 0)),
            pl.BlockSpec((1, 1, MLA_KV_LORA, tq), lambda b, i: (b, i, 0, 0)),
            pl.BlockSpec((1, tq, d), lambda b, i: (b, i, 0)),
            _resident(wuvt.shape),
            _resident(wout.shape),
        ],
        out_specs=pl.BlockSpec((1, tq, d), lambda b, i: (b, i, 0)),
        scratch_shapes=[
            pltpu.VMEM((heads, v_dim, tq), BF16),
            pltpu.VMEM((1, heads * tq), F32),
            pltpu.VMEM((1, heads * tq), F32),
            pltpu.VMEM((MLA_KV_LORA, heads * tq), F32),
            pltpu.VMEM((1, cb), F32),
        ],
        compiler_params=pltpu.CompilerParams(
            dimension_semantics=("parallel", "parallel"), vmem_limit_bytes=VMEM_LIMIT),
        name="mla_attn",
    )(qt, kfull, vtfull, kdiag, vtdiag, x, wuvt, wout)


def _mha_attn_kernel(qt_ref, k_ref, vt_ref, kdiag_ref, vtdiag_ref, ot_ref, m_s, acc_s, lag_s, *, tq):
    qb = pl.program_id(2)
    hp = qt_ref.shape[1]
    visible = _block_causal(tq, tq, 0)

    def diag_scores(i):
        return jnp.where(visible, _dot(kdiag_ref[0, i], qt_ref[0, i, 0]), -jnp.inf)

    def tile_scores(j, i):
        return _dot(k_ref[0, i, pl.ds(pl.multiple_of(j * tq, tq), tq), :], qt_ref[0, i, 0])

    def first_key_scores(i):
        return _dot(kdiag_ref[0, i, :BF16_SUBLANES, :], qt_ref[0, i, 0])[:1]

    _attend(qb, hp, tq, first_key_scores, diag_scores, lambda i: vtdiag_ref[0, i, 0],
            tile_scores, lambda j, i: vt_ref[0, i, j], m_s, None, acc_s, lag_s, 1.0)

    for i in range(hp):
        cs = slice(i * tq, (i + 1) * tq)
        inv_l = 1.0 / acc_s[MLA_V:MLA_V + 1, cs]
        ot_ref[0, i * MLA_V:(i + 1) * MLA_V, :] = (acc_s[:MLA_V, cs] * inv_l).astype(BF16)


def _mha_attn_call(qt, k, vt, *, hp, tq):
    bsz, heads, seq, qk_dim = k.shape
    nq = seq // tq
    v_rows = MLA_V + V_ONES_ROWS
    assert qt.shape == (bsz, heads, nq, qk_dim, tq)
    assert seq % tq == 0 and heads % hp == 0 and vt.shape == (bsz, heads, nq, v_rows, tq)
    return pl.pallas_call(
        functools.partial(_mha_attn_kernel, tq=tq),
        out_shape=jax.ShapeDtypeStruct((bsz, heads * MLA_V, seq), BF16),
        grid=(bsz, heads // hp, nq),
        in_specs=[
            pl.BlockSpec((1, hp, 1, qk_dim, tq), lambda b, g, i: (b, g, i, 0, 0)),
            pl.BlockSpec((1, hp, seq, qk_dim), lambda b, g, i: (b, g, 0, 0)),
            pl.BlockSpec((1, hp, nq, v_rows, tq), lambda b, g, i: (b, g, 0, 0, 0), pipeline_mode=pl.Buffered(1)),
            pl.BlockSpec((1, hp, tq, qk_dim), lambda b, g, i: (b, g, i, 0)),
            pl.BlockSpec((1, hp, 1, v_rows, tq), lambda b, g, i: (b, g, i, 0, 0)),
        ],
        out_specs=pl.BlockSpec((1, hp * MLA_V, tq), lambda b, g, i: (b, g, i)),
        scratch_shapes=[
            pltpu.VMEM((1, hp * tq), F32),
            pltpu.VMEM((v_rows, hp * tq), F32),
            pltpu.VMEM((1, tq), F32),
        ],
        compiler_params=pltpu.CompilerParams(
            dimension_semantics=("parallel", "parallel", "arbitrary"), vmem_limit_bytes=VMEM_LIMIT),
        name="mha_attn",
    )(qt, k, vt, k, vt)


def _rope_tables(pos):
    half = MLA_ROPE // 2
    inv = ROPE_THETA ** (-jnp.arange(half, dtype=F32) / half)
    ang = pos.astype(F32)[:, None] * inv[None, :]
    pad = jnp.zeros((pos.shape[0], LANES - MLA_ROPE), F32)
    cos = jnp.concatenate([jnp.cos(ang), jnp.cos(ang), pad], axis=-1)
    sin = jnp.concatenate([jnp.sin(ang), jnp.sin(ang), pad], axis=-1)
    return cos, sin


def _rotate_half_cols(w):
    half = w.shape[-1] // 2
    return jnp.concatenate([-w[..., half:], w[..., :half]], axis=-1)


def _pad_last(w, n):
    return jnp.concatenate([w, jnp.zeros(w.shape[:-1] + (n - w.shape[-1],), w.dtype)], axis=-1)


def _prep_gla(w_in, w_gk_up, b_gk, g_norm, w_out):
    qk = w_gk_up.shape[1]
    vw = w_out.shape[0]
    wq = w_in[:, :qk]
    wk = w_in[:, qk:2 * qk]
    wv = w_in[:, 2 * qk:2 * qk + vw]
    wg = w_in[:, 2 * qk + vw:2 * qk + 2 * vw]
    wlow = _pad_last(w_in[:, 2 * qk + 2 * vw:], LANES)
    rank = w_gk_up.shape[0]
    wgk = jnp.concatenate([w_gk_up, jnp.zeros((LANES - rank, qk), w_gk_up.dtype)], axis=0)
    return (wq.astype(BF16), wk.astype(BF16), wv.astype(BF16), wg.astype(BF16), wlow.astype(BF16),
            wgk.astype(BF16), b_gk[None, :], g_norm[None, :], w_out.astype(BF16))


def _prep_mla(w_down, q_norm, w_uq, kv_norm, w_uk, w_uv, w_out):
    wdq = w_down[:, :MLA_Q_LORA]
    wdkv = w_down[:, MLA_Q_LORA:MLA_Q_LORA + MLA_KV_LORA]
    wdk = w_down[:, MLA_Q_LORA + MLA_KV_LORA:]
    wdkr = _rotate_half_cols(wdk)
    wuq = w_uq.reshape(MLA_Q_LORA, MLA_HEADS, MLA_NOPE + MLA_ROPE)
    wnope = wuq[..., :MLA_NOPE].reshape(MLA_Q_LORA, MLA_HEADS * MLA_NOPE)
    wrope = wuq[..., MLA_NOPE:]
    wr = _pad_last(wrope, LANES).reshape(MLA_Q_LORA, MLA_HEADS * LANES)
    wrr = _pad_last(_rotate_half_cols(wrope), LANES).reshape(MLA_Q_LORA, MLA_HEADS * LANES)
    wuk = jnp.transpose(w_uk, (1, 2, 0))
    wuvt = jnp.transpose(w_uv, (1, 2, 0))
    wuk_flat = w_uk.reshape(MLA_KV_LORA, MLA_HEADS * MLA_NOPE)
    wuv_flat_t = w_uv.reshape(MLA_KV_LORA, MLA_HEADS * MLA_V).T
    return dict(
        wdq=wdq.astype(BF16), wdkv=wdkv.astype(BF16), wdk=_pad_last(wdk, LANES).astype(BF16),
        wdkr=_pad_last(wdkr, LANES).astype(BF16), qn=q_norm[None, :], kvn=kv_norm[None, :],
        wnope=wnope.astype(BF16), wr=wr.astype(BF16), wrr=wrr.astype(BF16), wuk=wuk.astype(BF16),
        wuvt=wuvt.astype(BF16), wuk_flat=wuk_flat.astype(BF16), wuv_flat_t=wuv_flat_t.astype(BF16),
        wout=w_out.astype(BF16))


def kernel(x_prompt, x_sample, state_gla, cache_ckv, cache_kpe, norm_ffn1, w_ffn1_gate, w_ffn1_up, w_ffn1_down, norm_mix, norm_ffn2, w_ffn2_gate, w_ffn2_up, w_ffn2_down, w_gla_in, w_gla_gk_up, b_gla_gk, gla_out_norm, w_gla_out, w_mla_down, mla_q_norm, w_mla_uq, mla_kv_norm, w_mla_uk, w_mla_uv, w_mla_out, norm_final):
    depth = norm_ffn1.shape[0]
    n_mixers = 2
    d = x_prompt.shape[-1]
    mla_scale = (MLA_NOPE + MLA_ROPE) ** -0.5
    gf = norm_final[None, :]

    ffn1 = (norm_ffn1[:, None, :], w_ffn1_gate, w_ffn1_up, w_ffn1_down)
    ffn2 = (norm_ffn2[:, None, :], w_ffn2_gate, w_ffn2_up, w_ffn2_down)
    gla_w = [_prep_gla(w_gla_in[j], w_gla_gk_up[j], b_gla_gk[j], gla_out_norm[j], w_gla_out[j])
             for j in range(w_gla_in.shape[0])]
    mla_w = [_prep_mla(w_mla_down[j], mla_q_norm[j], w_mla_uq[j], mla_kv_norm[j], w_mla_uk[j], w_mla_uv[j],
                       w_mla_out[j]) for j in range(w_mla_down.shape[0])]

    def run(x, pos, gla_s0, past_ckv, past_kpe, *, gla_nb, gla_tl, attn_tq):
        bsz, seq, _ = x.shape
        cos, sin = _rope_tables(pos)
        new_gla, new_ckv, new_kpe = [], [], []
        pending = {}

        def ffn(x, w, layer, final_norm=False):
            return _ffn_call(x.reshape(bsz * seq, d), w[0], w[1], w[2], w[3], gf, layer,
                             final_norm=final_norm, **pending).reshape(bsz, seq, d)

        for i in range(depth):
            x = ffn(x, ffn1, i)
            j = i // n_mixers
            gm = norm_mix[i][None, :]
            if i % n_mixers == 0:
                x, s_fin = _gla_call(x, gla_s0[j], gm, *gla_w[j], nb=gla_nb, tl=gla_tl)
                new_gla.append(s_fin)
            else:
                w = mla_w[j]
                absorb = past_ckv is not None
                q, k, vt, ckv, kpe = _mla_proj_call(
                    x, cos, sin, gm, w["wdq"], w["wdkv"], w["wdk"], w["wdkr"], w["qn"], w["kvn"],
                    w["wnope"], w["wr"], w["wrr"], w["wuk"] if absorb else w["wuk_flat"], w["wuv_flat_t"],
                    tm=attn_tq, absorb=absorb, q_scale=1.0 if absorb else mla_scale * LOG2E)
                if absorb:
                    pc = past_ckv[j].astype(BF16)
                    pad = jnp.zeros(past_kpe[j].shape[:-1] + (KCAT - MLA_KV_LORA - MLA_ROPE,), BF16)
                    kfull = jnp.concatenate([pc, past_kpe[j].astype(BF16), pad], axis=-1)
                    vtfull = jnp.swapaxes(pc.reshape(bsz, -1, ATTN_KEY_TILE, MLA_KV_LORA), 2, 3)
                    qt = q.reshape(bsz, MLA_HEADS, seq // attn_tq, attn_tq, KCAT).transpose(0, 2, 4, 1, 3)
                    qt = qt.reshape(bsz, seq // attn_tq, KCAT, MLA_HEADS * attn_tq)
                    x = _mla_attn_call(qt, kfull, vtfull, k, vt, x, w["wuvt"], w["wout"],
                                       tq=attn_tq, tk=ATTN_KEY_TILE, scale=mla_scale)
                else:
                    qt = jnp.swapaxes(q.reshape(bsz, MLA_HEADS, seq // attn_tq, attn_tq, q.shape[-1]), 3, 4)
                    pending = dict(attn_t=_mha_attn_call(qt, k, vt, hp=ATTN_HEAD_GROUP, tq=attn_tq), w_out=w["wout"])
                new_ckv.append(ckv)
                new_kpe.append(kpe)
            x = ffn(x, ffn2, i, final_norm=(i == depth - 1))
            pending = {}
        return x, jnp.stack(new_gla), jnp.stack(new_ckv), jnp.stack(new_kpe)

    bp, lp = x_prompt.shape[0], x_prompt.shape[1]
    n_gla = w_gla_in.shape[0]
    s0_p = jnp.zeros((n_gla, bp) + state_gla.shape[2:], x_prompt.dtype)
    y_p, gla_p, ckv_p, kpe_p = run(
        x_prompt, jnp.arange(lp), s0_p, None, None, gla_nb=1, gla_tl=512, attn_tq=ATTN_KEY_TILE)

    past_len = cache_ckv.shape[2]
    ls = x_sample.shape[1]
    y_s, gla_s, ckv_s, kpe_s = run(
        x_sample, past_len + jnp.arange(ls), state_gla, cache_ckv, cache_kpe, gla_nb=8, gla_tl=ls, attn_tq=ls)

    return (y_p, y_s, gla_p, ckv_p, kpe_p, gla_s, ckv_s, kpe_s)
```

```python
import functools

import jax
import jax.numpy as jnp
from jax import lax
from jax.experimental import pallas as pl
from jax.experimental.pallas import tpu as pltpu

F32 = jnp.float32
BF16 = jnp.bfloat16

EPS = 1e-6
CHUNK = 64
GLA_HEADS = 4
GLA_GATE_NORM = 16.0
GLA_ROW_GROUPS = 2
MLA_HEADS = 8
MLA_NOPE = 128
MLA_ROPE = 64
MLA_V = 128
MLA_Q_LORA = 384
MLA_KV_LORA = 256
ROPE_THETA = 10000.0
LANES = 128
BF16_SUBLANES = 16
F32_SUBLANES = 8
MXU_COLS = 256
LOG2E = 1.4426950408889634
ATTN_COL_BLOCK = 512
ATTN_KEY_TILE = 512
ATTN_HEAD_GROUP = 4
V_ONES_ROWS = 16
ATTN_LAG_LIMIT = 64.0
KCAT = MLA_KV_LORA + LANES

VMEM_LIMIT = 56 * 1024 * 1024


def _rms(xf, g):
    return xf * lax.rsqrt(jnp.mean(xf * xf, axis=-1, keepdims=True) + EPS) * g


def _silu(a):
    return a * (1.0 / (1.0 + jnp.exp(-a)))


def _dot(a, b):
    return jnp.dot(a, b, preferred_element_type=F32)


def _dot_nt(a, b):
    return lax.dot_general(a, b, (((1,), (1,)), ((), ())), preferred_element_type=F32)


def _dot_tn(a, b):
    return lax.dot_general(a, b, (((0,), (0,)), ((), ())), preferred_element_type=F32)


def _resident(shape):
    zeros = (0,) * len(shape)
    return pl.BlockSpec(shape, lambda *_: zeros, pipeline_mode=pl.Buffered(1))


def _ffn_kernel(*refs, f_chunk, final_norm, mixer_proj):
    if mixer_proj:
        x_ref, ot_ref, wo_ref, g_ref, wg_ref, wu_ref, wd_ref, gf_ref, o_ref, hid_ref = refs
        x = x_ref[...] + _dot_tn(ot_ref[0], wo_ref[...])
    else:
        x_ref, g_ref, wg_ref, wu_ref, wd_ref, gf_ref, o_ref, hid_ref = refs
        x = x_ref[...]
    h = _rms(x, g_ref[0]).astype(BF16)
    d_ff = wg_ref.shape[2]
    for c in range(d_ff // f_chunk):
        sl = slice(c * f_chunk, (c + 1) * f_chunk)
        a = _dot(h, wg_ref[0, :, sl].astype(BF16))
        b = _dot(h, wu_ref[0, :, sl].astype(BF16))
        hid_ref[:, sl] = (_silu(a) * b).astype(BF16)
    y = x + 0.5 * _dot(hid_ref[...], wd_ref[0].astype(BF16))
    if final_norm:
        y = _rms(y, gf_ref[...])
    o_ref[...] = y


def _layer_resident(shape, layer):
    index = (layer,) + (0,) * (len(shape) - 1)
    return pl.BlockSpec((1,) + tuple(shape[1:]), lambda *_: index, pipeline_mode=pl.Buffered(1))


def _ffn_call(x, g, wg, wu, wd, gf, layer, *, final_norm, attn_t=None, w_out=None, tm=512, f_chunk=256):
    t, d = x.shape
    d_ff = wg.shape[2]
    assert t % tm == 0 and d_ff % f_chunk == 0
    mixer_proj = attn_t is not None
    x_spec = pl.BlockSpec((tm, d), lambda i: (i, 0))
    weights = (g, wg, wu, wd)
    if mixer_proj:
        tiles_per_seq = attn_t.shape[2] // tm
        assert attn_t.shape[0] * attn_t.shape[2] == t and attn_t.shape[2] % tm == 0
        ot_spec = pl.BlockSpec((1, attn_t.shape[1], tm), lambda i: (i // tiles_per_seq, 0, i % tiles_per_seq))
        operands = (x, attn_t, w_out) + weights + (gf,)
        in_specs = [x_spec, ot_spec, _resident(w_out.shape)]
    else:
        operands = (x,) + weights + (gf,)
        in_specs = [x_spec]
    return pl.pallas_call(
        functools.partial(_ffn_kernel, f_chunk=f_chunk, final_norm=final_norm, mixer_proj=mixer_proj),
        out_shape=jax.ShapeDtypeStruct((t, d), F32),
        grid=(t // tm,),
        in_specs=in_specs + [_layer_resident(w.shape, layer) for w in weights] + [_resident(gf.shape)],
        out_specs=pl.BlockSpec((tm, d), lambda i: (i, 0)),
        scratch_shapes=[pltpu.VMEM((tm, d_ff), BF16)],
        compiler_params=pltpu.CompilerParams(
            dimension_semantics=("parallel",), vmem_limit_bytes=VMEM_LIMIT),
        name="ffn",
    )(*operands)


def _split3(a):
    hi = a.astype(BF16)
    r = a - hi.astype(F32)
    mid = r.astype(BF16)
    lo = (r - mid.astype(F32)).astype(BF16)
    return hi, mid, lo


def _interleave(primary, filler):
    for n, step in enumerate(primary):
        step()
        if n < len(filler):
            filler[n]()
    for extra in filler[len(primary):]:
        extra()


def _gla_kernel(x_ref, s0_ref, gmix_ref, wq_ref, wk_ref, wv_ref, wg_ref, wlow_ref, wgk_ref, bgk_ref,
                gout_ref, wout_ref, y_ref, s_ref, st_s, *, nb, tl, n_sub):
    l = pl.program_id(1)
    heads = GLA_HEADS
    dk = wq_ref.shape[1] // heads
    dv = wv_ref.shape[1] // heads
    rows_sub = nb * tl // n_sub
    n_chunk = rows_sub // CHUNK
    d = x_ref.shape[-1]

    @pl.when(l == 0)
    def _():
        st_s[...] = s0_ref[...]

    ti = lax.broadcasted_iota(jnp.int32, (CHUNK, CHUNK), 0)
    si = lax.broadcasted_iota(jnp.int32, (CHUNK, CHUNK), 1)
    causal = si <= ti
    tri = jnp.where(causal, 1.0, 0.0).astype(BF16)

    def rows_of(ref, sub):
        row0 = sub * rows_sub
        if rows_sub >= tl:
            return ref.at[row0 // tl:(row0 + rows_sub) // tl]
        return ref.at[row0 // tl, row0 % tl:row0 % tl + rows_sub, :]

    def in_projection(sub):
        env = {}

        def start():
            env["x"] = rows_of(x_ref, sub)[...].reshape(rows_sub, d)
            env["h"] = _rms(env["x"], gmix_ref[...]).astype(BF16)

        def piece(name, ref, lo, hi):
            return lambda: env.setdefault(name, []).append(_dot(env["h"], ref[:, lo:hi]))

        thunks = [start]
        for name, ref in (("low", wlow_ref), ("q", wq_ref), ("k", wk_ref), ("v", wv_ref), ("g", wg_ref)):
            width = ref.shape[1]
            thunks += [piece(name, ref, lo, min(lo + MXU_COLS, width)) for lo in range(0, width, MXU_COLS)]
        return thunks, env

    def recurrence(sub, env):
        row0 = sub * rows_sub
        cat = lambda name: jnp.concatenate(env[name], axis=-1)
        loc = {}

        def start():
            loc["q"] = cat("q") * (dk ** -0.5)
            loc["k"] = cat("k")
            loc["v"] = cat("v").astype(BF16)
            z = _dot(cat("low").astype(BF16), wgk_ref[...]) + bgk_ref[...]
            loc["log_a"] = -(jnp.maximum(-z, 0.0) + jnp.log1p(jnp.exp(-jnp.abs(z)))) / GLA_GATE_NORM
            loc["state"] = {}
            loc["o"] = [[None] * n_chunk for _ in range(heads)]

        def cumsum(c):
            rs = slice(c * CHUNK, (c + 1) * CHUNK)
            a_hi, a_mid, a_lo = _split3(loc["log_a"][rs, :])
            loc["cum", c] = _dot(tri, a_hi) + _dot(tri, a_mid) + _dot(tri, a_lo)

        def scores(c):
            rs = slice(c * CHUNK, (c + 1) * CHUNK)
            cum = loc["cum", c]
            cum_last = cum[CHUNK - 1:CHUNK, :]
            q_dec = (loc["q"][rs, :] * jnp.exp(cum)).astype(BF16)
            k_inv = (loc["k"][rs, :] * jnp.exp(-cum)).astype(BF16)
            loc["k_tail", c] = (loc["k"][rs, :] * jnp.exp(cum_last - cum)).astype(BF16)
            loc["q_dec", c] = q_dec
            dec = jnp.broadcast_to(jnp.exp(cum_last), (F32_SUBLANES, heads * dk))
            loc["dec", c] = [dec[:, hd * dk:(hd + 1) * dk].T[:, :1] for hd in range(heads)]
            loc["scores", c] = [
                jnp.where(causal, _dot_nt(q_dec[:, hd * dk:(hd + 1) * dk], k_inv[:, hd * dk:(hd + 1) * dk]), 0.0)
                .astype(BF16) for hd in range(heads)]

        def values(c):
            rs = slice(c * CHUNK, (c + 1) * CHUNK)
            vs = [loc["v"][rs, hd * dv:(hd + 1) * dv] for hd in range(heads)]
            loc["o_intra", c] = [_dot(loc["scores", c][hd], vs[hd]) for hd in range(heads)]
            loc["kv", c] = [_dot_tn(loc["k_tail", c][:, hd * dk:(hd + 1) * dk], vs[hd]) for hd in range(heads)]

        def carry(c):
            b = (row0 + c * CHUNK) // tl
            for hd in range(heads):
                ks = slice(hd * dk, (hd + 1) * dk)
                st = loc["state"].get((b, hd))
                if st is None:
                    st = st_s[b, hd]
                loc["o"][hd][c] = loc["o_intra", c][hd] + _dot(loc["q_dec", c][:, ks], st.astype(BF16))
                loc["state"][(b, hd)] = st * loc["dec", c][hd] + loc["kv", c][hd]

        def finish():
            for (b, hd), st in loc["state"].items():
                st_s[b, hd] = st
            env["o"] = [jnp.concatenate(loc["o"][hd], axis=0) for hd in range(heads)]

        thunks = [start]
        for c in range(n_chunk):
            thunks += [functools.partial(cumsum, c), functools.partial(scores, c), functools.partial(values, c)]
        thunks += [functools.partial(carry, c) for c in range(n_chunk)] + [finish]
        return thunks

    def out_projection(sub, env):
        loc = {}

        def gate():
            g = jnp.concatenate(env["g"], axis=-1)
            loc["gated"] = jnp.concatenate(
                [(_rms(env["o"][hd], gout_ref[...]) * _silu(g[:, hd * dv:(hd + 1) * dv])).astype(BF16)
                 for hd in range(heads)], axis=-1)

        def piece(lo):
            return lambda: loc.setdefault("y", []).append(_dot(loc["gated"], wout_ref[:, lo:lo + MXU_COLS]))

        def store():
            y = env["x"] + jnp.concatenate(loc["y"], axis=-1)
            dst = rows_of(y_ref, sub)
            dst[...] = y.reshape(dst.shape)

        return [gate] + [piece(lo) for lo in range(0, d, MXU_COLS)] + [store]

    proj = [in_projection(sub) for sub in range(n_sub)]
    for step in proj[0][0]:
        step()
    for sub in range(n_sub):
        filler = []
        if sub + 1 < n_sub:
            filler += proj[sub + 1][0]
        if sub >= 1:
            filler += out_projection(sub - 1, proj[sub - 1][1])
        _interleave(recurrence(sub, proj[sub][1]), filler)
    for step in out_projection(n_sub - 1, proj[n_sub - 1][1]):
        step()

    @pl.when(l == pl.num_programs(1) - 1)
    def _():
        s_ref[...] = st_s[...]


def _gla_call(x, s0, gmix, wq, wk, wv, wg, wlow, wgk, bgk, gout, wout, *, nb, tl, n_sub=GLA_ROW_GROUPS):
    bsz, seq, d = x.shape
    heads = GLA_HEADS
    dk = wq.shape[1] // heads
    dv = wv.shape[1] // heads
    rows_sub = nb * tl // n_sub
    assert bsz % nb == 0 and seq % tl == 0 and tl % CHUNK == 0 and (nb * tl) % n_sub == 0
    assert rows_sub % CHUNK == 0 and (rows_sub % tl == 0 or tl % rows_sub == 0)
    return pl.pallas_call(
        functools.partial(_gla_kernel, nb=nb, tl=tl, n_sub=n_sub),
        out_shape=(jax.ShapeDtypeStruct((bsz, seq, d), F32),
                   jax.ShapeDtypeStruct((bsz, heads, dk, dv), F32)),
        grid=(bsz // nb, seq // tl),
        in_specs=[
            pl.BlockSpec((nb, tl, d), lambda b, l: (b, l, 0)),
            pl.BlockSpec((nb, heads, dk, dv), lambda b, l: (b, 0, 0, 0)),
            _resident(gmix.shape), _resident(wq.shape), _resident(wk.shape), _resident(wv.shape),
            _resident(wg.shape), _resident(wlow.shape), _resident(wgk.shape), _resident(bgk.shape),
            _resident(gout.shape), _resident(wout.shape),
        ],
        out_specs=(pl.BlockSpec((nb, tl, d), lambda b, l: (b, l, 0)),
                   pl.BlockSpec((nb, heads, dk, dv), lambda b, l: (b, 0, 0, 0))),
        scratch_shapes=[pltpu.VMEM((nb, heads, dk, dv), F32)],
        compiler_params=pltpu.CompilerParams(
            dimension_semantics=("parallel", "arbitrary"), vmem_limit_bytes=VMEM_LIMIT),
        name="gla",
    )(x, s0, gmix, wq, wk, wv, wg, wlow, wgk, bgk, gout, wout)


def _mla_proj_kernel(x_ref, cos_ref, sin_ref, gmix_ref, wdq_ref, wdkv_ref, wdk_ref, wdkr_ref, qn_ref, kvn_ref,
                     wnope_ref, wr_ref, wrr_ref, wkey_ref, wval_ref,
                     q_ref, k_ref, vt_ref, ckv_ref, kpe_ref, *, absorb, q_scale):
    nbp, tm, d = x_ref.shape
    rows = nbp * tm
    x = x_ref[...].reshape(rows, d)
    h = _rms(x, gmix_ref[...]).astype(BF16)
    cos = jnp.concatenate([cos_ref[...]] * nbp, axis=0)
    sin = jnp.concatenate([sin_ref[...]] * nbp, axis=0)

    ckv = _rms(_dot(h, wdkv_ref[...]), kvn_ref[...])
    kpe = _dot(h, wdk_ref[...]) * cos + _dot(h, wdkr_ref[...]) * sin
    ckv_ref[...] = ckv.reshape(nbp, tm, MLA_KV_LORA)
    kpe_ref[...] = kpe[:, :MLA_ROPE].reshape(nbp, tm, MLA_ROPE)
    ckv_b = ckv.astype(BF16)
    kpe_b = kpe.astype(BF16)
    ckv_t = ckv.T.astype(BF16)

    hq = _rms(_dot(h, wdq_ref[...]), qn_ref[...]).astype(BF16)
    q_nope = _dot(hq, wnope_ref[...])
    q_r = _dot(hq, wr_ref[...])
    q_rr = _dot(hq, wrr_ref[...])

    if absorb:
        k_ref[:, :, :MLA_KV_LORA] = ckv_b.reshape(nbp, tm, MLA_KV_LORA)
        k_ref[:, :, MLA_KV_LORA:] = kpe_b.reshape(nbp, tm, LANES)
        for b in range(nbp):
            vt_ref[b, 0] = ckv_t[:, b * tm:(b + 1) * tm]
    else:
        k_nope = _dot(ckv_b, wkey_ref[...]).astype(BF16)
        v_t = _dot(wval_ref[...], ckv_t).astype(BF16)
        ones = jnp.ones((V_ONES_ROWS, tm), BF16)
    for hd in range(MLA_HEADS):
        hs = slice(hd * LANES, (hd + 1) * LANES)
        q_pe = q_r[:, hs] * cos + q_rr[:, hs] * sin
        if absorb:
            q_lat = _dot(q_nope[:, hs].astype(BF16), wkey_ref[hd])
            q_t = jnp.concatenate([q_lat, q_pe], axis=-1).T.astype(BF16)
            for b in range(nbp):
                q_ref[b, 0, :, hd * tm:(hd + 1) * tm] = q_t[:, b * tm:(b + 1) * tm]
        else:
            for b in range(nbp):
                bs = slice(b * tm, (b + 1) * tm)
                q_ref[b, hd, 0, :MLA_NOPE] = (q_nope[bs, hs] * q_scale).T.astype(BF16)
                q_ref[b, hd, 0, MLA_NOPE:] = (q_pe[bs] * q_scale).T.astype(BF16)
                k_ref[b, hd, :, :MLA_NOPE] = k_nope[bs, hs]
                k_ref[b, hd, :, MLA_NOPE:] = kpe_b[bs]
                vt_ref[b, hd, 0, :MLA_V] = v_t[hd * MLA_V:(hd + 1) * MLA_V, bs]
                vt_ref[b, hd, 0, MLA_V:] = ones


def _mla_proj_call(x, cos, sin, gmix, wdq, wdkv, wdk, wdkr, qn, kvn, wnope, wr, wrr, wkey, wval, *, tm, nbp, absorb,
                   q_scale=1.0):
    bsz, seq, d = x.shape
    assert seq % tm == 0 and bsz % nbp == 0
    weights = (gmix, wdq, wdkv, wdk, wdkr, qn, kvn, wnope, wr, wrr, wkey, wval)
    nt = seq // tm
    if absorb:
        q_k_vt = (jax.ShapeDtypeStruct((bsz, nt, KCAT, MLA_HEADS * tm), BF16),
                  jax.ShapeDtypeStruct((bsz, seq, KCAT), BF16),
                  jax.ShapeDtypeStruct((bsz, nt, MLA_KV_LORA, tm), BF16))
        q_k_vt_specs = (pl.BlockSpec((nbp, 1, KCAT, MLA_HEADS * tm), lambda b, l: (b, l, 0, 0)),
                        pl.BlockSpec((nbp, tm, KCAT), lambda b, l: (b, l, 0)),
                        pl.BlockSpec((nbp, 1, MLA_KV_LORA, tm), lambda b, l: (b, l, 0, 0)))
    else:
        qk_dim = MLA_NOPE + LANES
        v_rows = MLA_V + V_ONES_ROWS
        q_k_vt = (jax.ShapeDtypeStruct((bsz, MLA_HEADS, nt, qk_dim, tm), BF16),
                  jax.ShapeDtypeStruct((bsz, MLA_HEADS, seq, qk_dim), BF16),
                  jax.ShapeDtypeStruct((bsz, MLA_HEADS, nt, v_rows, tm), BF16))
        q_k_vt_specs = (pl.BlockSpec((nbp, MLA_HEADS, 1, qk_dim, tm), lambda b, l: (b, 0, l, 0, 0)),
                        pl.BlockSpec((nbp, MLA_HEADS, tm, qk_dim), lambda b, l: (b, 0, l, 0)),
                        pl.BlockSpec((nbp, MLA_HEADS, 1, v_rows, tm), lambda b, l: (b, 0, l, 0, 0)))
    return pl.pallas_call(
        functools.partial(_mla_proj_kernel, absorb=absorb, q_scale=q_scale),
        out_shape=q_k_vt + (jax.ShapeDtypeStruct((bsz, seq, MLA_KV_LORA), F32),
                            jax.ShapeDtypeStruct((bsz, seq, MLA_ROPE), F32)),
        grid=(bsz // nbp, nt),
        in_specs=[
            pl.BlockSpec((nbp, tm, d), lambda b, l: (b, l, 0)),
            pl.BlockSpec((tm, LANES), lambda b, l: (l, 0)),
            pl.BlockSpec((tm, LANES), lambda b, l: (l, 0)),
        ] + [_resident(w.shape) for w in weights],
        out_specs=q_k_vt_specs + (pl.BlockSpec((nbp, tm, MLA_KV_LORA), lambda b, l: (b, l, 0)),
                                  pl.BlockSpec((nbp, tm, MLA_ROPE), lambda b, l: (b, l, 0))),
        compiler_params=pltpu.CompilerParams(
            dimension_semantics=("parallel", "parallel"), vmem_limit_bytes=VMEM_LIMIT),
        name="mla_proj",
    )(x, cos, sin, *weights)


def _exp2_scaled(t, c):
    return jnp.exp2(t if c == 1.0 else t * c)


def _tile_update(s, vt, cs, m_s, l_s, acc_s, lag_s, c, one_pass):
    m_prev = m_s[:, cs]
    mx = jnp.max(s, axis=0, keepdims=True)
    m_new = jnp.maximum(m_prev, mx)
    alpha = _exp2_scaled(m_prev - m_new, c)
    if one_pass:
        p = _exp2_scaled(s - m_prev, c)
        if l_s is not None:
            l_s[:, cs] = (l_s[:, cs] + jnp.sum(p, axis=0, keepdims=True)) * alpha
        acc_s[:, cs] = (acc_s[:, cs] + _dot(vt, p.astype(BF16))) * alpha
        lag_s[...] = jnp.maximum(lag_s[...], (mx - m_prev) * c)
    else:
        p = _exp2_scaled(s - m_new, c)
        if l_s is not None:
            l_s[:, cs] = alpha * l_s[:, cs] + jnp.sum(p, axis=0, keepdims=True)
        acc_s[:, cs] = acc_s[:, cs] * alpha + _dot(vt, p.astype(BF16))
    m_s[:, cs] = m_new


def _attend(n_full, n_blk, cb, first_key_scores, diag_scores, diag_values, tile_scores, tile_values,
            m_s, l_s, acc_s, lag_s, c):
    def cols(i):
        return slice(i * cb, (i + 1) * cb)

    def run(one_pass):
        lag_s[...] = jnp.zeros(lag_s.shape, F32)
        for i in range(n_blk):
            m_s[:, cols(i)] = first_key_scores(i)
            acc_s[:, cols(i)] = jnp.zeros((acc_s.shape[0], cb), F32)
            if l_s is not None:
                l_s[:, cols(i)] = jnp.zeros((1, cb), F32)

        def update(units):
            s_next = units[0][0]()
            for u, (_, values, i) in enumerate(units):
                s = s_next
                if u + 1 < len(units):
                    s_next = units[u + 1][0]()
                _tile_update(s, values(), cols(i), m_s, l_s, acc_s, lag_s, c, one_pass)

        def full_tiles(js):
            return [(functools.partial(tile_scores, j, i), functools.partial(tile_values, j, i), i)
                    for j in js for i in range(n_blk)]

        update([(functools.partial(diag_scores, i), functools.partial(diag_values, i), i) for i in range(n_blk)])

        def pair(jj, carry):
            update(full_tiles((2 * jj, 2 * jj + 1)))
            return carry

        lax.fori_loop(0, n_full // 2, pair, 0)
        if isinstance(n_full, int):
            if n_full % 2:
                update(full_tiles((n_full - 1,)))
        else:
            @pl.when(n_full % 2 == 1)
            def _():
                update(full_tiles((n_full - 1,)))

    run(one_pass=True)

    @pl.when(jnp.max(lag_s[...]) > ATTN_LAG_LIMIT)
    def _():
        run(one_pass=False)


def _block_causal(tq, cb, col0):
    ki = lax.broadcasted_iota(jnp.int32, (tq, cb), 0) // CHUNK
    qi = ((lax.broadcasted_iota(jnp.int32, (tq, cb), 1) + col0) % tq) // CHUNK
    return ki <= qi


def _mla_attn_kernel(qt_ref, kfull_ref, vtfull_ref, kdiag_ref, vtdiag_ref, x_ref, wuvt_ref, wout_ref, y_ref,
                     o_s, m_s, l_s, acc_s, lag_s, *, tq, tk, cb, past_tiles, single_block, scale):
    qb = pl.program_id(1)
    heads = o_s.shape[0]
    n_blk = heads * tq // cb

    def qt_blk(i):
        return qt_ref[0, 0, :, i * cb:(i + 1) * cb]

    def diag_scores(i):
        return jnp.where(_block_causal(tq, cb, i * cb), _dot(kdiag_ref[0], qt_blk(i)), -jnp.inf)

    def tile_scores(j, i):
        return _dot(kfull_ref[0, pl.ds(pl.multiple_of(j * tk, tk), tk), :], qt_blk(i))

    def first_key_scores(i):
        return _dot(kdiag_ref[0, :BF16_SUBLANES, :], qt_blk(i))[:1]

    n_full = past_tiles if single_block else past_tiles + qb * (tq // tk)
    _attend(n_full, n_blk, cb, first_key_scores, diag_scores, lambda i: vtdiag_ref[0, 0],
            tile_scores, lambda j, i: vtfull_ref[0, j], m_s, l_s, acc_s, lag_s, scale * LOG2E)

    for hd in range(heads):
        cs = slice(hd * tq, (hd + 1) * tq)
        o_lat_t = (acc_s[:, cs] * (1.0 / l_s[:, cs])).astype(BF16)
        o_s[hd] = _dot(wuvt_ref[hd], o_lat_t).astype(BF16)
    o_t = o_s[...].reshape(o_s.shape[0] * o_s.shape[1], tq)
    y_ref[0] = x_ref[0] + _dot_tn(o_t, wout_ref[...])


def _mla_attn_call(qt, kfull, vtfull, kdiag, vtdiag, x, wuvt, wout, *, tq, tk, scale):
    bsz, seq, d = x.shape
    heads = wuvt.shape[0]
    lfull = kfull.shape[1]
    nq = seq // tq
    assert seq % tq == 0 and lfull % tk == 0 and qt.shape == (bsz, nq, KCAT, heads * tq)
    assert vtfull.shape == (bsz, lfull // tk, MLA_KV_LORA, tk) and vtdiag.shape == (bsz, nq, MLA_KV_LORA, tq)
    if nq == 1:
        past_tiles = lfull // tk
    else:
        assert tq == tk and lfull == seq
        past_tiles = 0
    cb = max(tq, min(heads * tq, ATTN_COL_BLOCK))
    assert cb % tq == 0 and (heads * tq) % cb == 0
    v_dim = wuvt.shape[1]
    return pl.pallas_call(
        functools.partial(_mla_attn_kernel, tq=tq, tk=tk, cb=cb, past_tiles=past_tiles, single_block=nq == 1,
                          scale=scale),
        out_shape=jax.ShapeDtypeStruct((bsz, seq, d), F32),
        grid=(bsz, nq),
        in_specs=[
            pl.BlockSpec((1, 1, KCAT, heads * tq), lambda b, i: (b, i, 0, 0)),
            pl.BlockSpec((1, lfull, KCAT), lambda b, i: (b, 0, 0)),
            pl.BlockSpec((1, lfull // tk, MLA_KV_LORA, tk), lambda b, i: (b, 0, 0, 0)),
            pl.BlockSpec((1, tq, KCAT), lambda b, i: (b, i, 0)),
            pl.BlockSpec((1, 1, MLA_KV_LORA, tq), lambda b, i: (b, i, 0, 0)),
            pl.BlockSpec((1, tq, d), lambda b, i: (b, i, 0)),
            _resident(wuvt.shape),
            _resident(wout.shape),
        ],
        out_specs=pl.BlockSpec((1, tq, d), lambda b, i: (b, i, 0)),
        scratch_shapes=[
            pltpu.VMEM((heads, v_dim, tq), BF16),
            pltpu.VMEM((1, heads * tq), F32),
            pltpu.VMEM((1, heads * tq), F32),
            pltpu.VMEM((MLA_KV_LORA, heads * tq), F32),
            pltpu.VMEM((1, cb), F32),
        ],
        compiler_params=pltpu.CompilerParams(
            dimension_semantics=("parallel", "parallel"), vmem_limit_bytes=VMEM_LIMIT),
        name="mla_attn",
    )(qt, kfull, vtfull, kdiag, vtdiag, x, wuvt, wout)


def _mha_attn_kernel(qt_ref, k_ref, vt_ref, kdiag_ref, vtdiag_ref, ot_ref, m_s, acc_s, lag_s, *, tq):
    qb = pl.program_id(2)
    hp = qt_ref.shape[1]
    visible = _block_causal(tq, tq, 0)

    def diag_scores(i):
        return jnp.where(visible, _dot(kdiag_ref[0, i], qt_ref[0, i, 0]), -jnp.inf)

    def tile_scores(j, i):
        return _dot(k_ref[0, i, pl.ds(pl.multiple_of(j * tq, tq), tq), :], qt_ref[0, i, 0])

    def first_key_scores(i):
        return _dot(kdiag_ref[0, i, :BF16_SUBLANES, :], qt_ref[0, i, 0])[:1]

    _attend(qb, hp, tq, first_key_scores, diag_scores, lambda i: vtdiag_ref[0, i, 0],
            tile_scores, lambda j, i: vt_ref[0, i, j], m_s, None, acc_s, lag_s, 1.0)

    for i in range(hp):
        cs = slice(i * tq, (i + 1) * tq)
        inv_l = 1.0 / acc_s[MLA_V:MLA_V + 1, cs]
        ot_ref[0, i * MLA_V:(i + 1) * MLA_V, :] = (acc_s[:MLA_V, cs] * inv_l).astype(BF16)


def _mha_attn_call(qt, k, vt, *, hp, tq):
    bsz, heads, seq, qk_dim = k.shape
    nq = seq // tq
    v_rows = MLA_V + V_ONES_ROWS
    assert qt.shape == (bsz, heads, nq, qk_dim, tq)
    assert seq % tq == 0 and heads % hp == 0 and vt.shape == (bsz, heads, nq, v_rows, tq)
    return pl.pallas_call(
        functools.partial(_mha_attn_kernel, tq=tq),
        out_shape=jax.ShapeDtypeStruct((bsz, heads * MLA_V, seq), BF16),
        grid=(bsz, heads // hp, nq),
        in_specs=[
            pl.BlockSpec((1, hp, 1, qk_dim, tq), lambda b, g, i: (b, g, i, 0, 0)),
            pl.BlockSpec((1, hp, seq, qk_dim), lambda b, g, i: (b, g, 0, 0)),
            pl.BlockSpec((1, hp, nq, v_rows, tq), lambda b, g, i: (b, g, 0, 0, 0), pipeline_mode=pl.Buffered(1)),
            pl.BlockSpec((1, hp, tq, qk_dim), lambda b, g, i: (b, g, i, 0)),
            pl.BlockSpec((1, hp, 1, v_rows, tq), lambda b, g, i: (b, g, i, 0, 0)),
        ],
        out_specs=pl.BlockSpec((1, hp * MLA_V, tq), lambda b, g, i: (b, g, i)),
        scratch_shapes=[
            pltpu.VMEM((1, hp * tq), F32),
            pltpu.VMEM((v_rows, hp * tq), F32),
            pltpu.VMEM((1, tq), F32),
        ],
        compiler_params=pltpu.CompilerParams(
            dimension_semantics=("parallel", "parallel", "arbitrary"), vmem_limit_bytes=VMEM_LIMIT),
        name="mha_attn",
    )(qt, k, vt, k, vt)


def _rope_tables(pos):
    half = MLA_ROPE // 2
    inv = ROPE_THETA ** (-jnp.arange(half, dtype=F32) / half)
    ang = pos.astype(F32)[:, None] * inv[None, :]
    pad = jnp.zeros((pos.shape[0], LANES - MLA_ROPE), F32)
    cos = jnp.concatenate([jnp.cos(ang), jnp.cos(ang), pad], axis=-1)
    sin = jnp.concatenate([jnp.sin(ang), jnp.sin(ang), pad], axis=-1)
    return cos, sin


def _rotate_half_cols(w):
    half = w.shape[-1] // 2
    return jnp.concatenate([-w[..., half:], w[..., :half]], axis=-1)


def _pad_last(w, n):
    return jnp.concatenate([w, jnp.zeros(w.shape[:-1] + (n - w.shape[-1],), w.dtype)], axis=-1)


def _prep_gla(w_in, w_gk_up, b_gk, g_norm, w_out):
    qk = w_gk_up.shape[1]
    vw = w_out.shape[0]
    wq = w_in[:, :qk]
    wk = w_in[:, qk:2 * qk]
    wv = w_in[:, 2 * qk:2 * qk + vw]
    wg = w_in[:, 2 * qk + vw:2 * qk + 2 * vw]
    wlow = _pad_last(w_in[:, 2 * qk + 2 * vw:], LANES)
    rank = w_gk_up.shape[0]
    wgk = jnp.concatenate([w_gk_up, jnp.zeros((LANES - rank, qk), w_gk_up.dtype)], axis=0)
    return (wq.astype(BF16), wk.astype(BF16), wv.astype(BF16), wg.astype(BF16), wlow.astype(BF16),
            wgk.astype(BF16), b_gk[None, :], g_norm[None, :], w_out.astype(BF16))


def _prep_mla(w_down, q_norm, w_uq, kv_norm, w_uk, w_uv, w_out):
    wdq = w_down[:, :MLA_Q_LORA]
    wdkv = w_down[:, MLA_Q_LORA:MLA_Q_LORA + MLA_KV_LORA]
    wdk = w_down[:, MLA_Q_LORA + MLA_KV_LORA:]
    wdkr = _rotate_half_cols(wdk)
    wuq = w_uq.reshape(MLA_Q_LORA, MLA_HEADS, MLA_NOPE + MLA_ROPE)
    wnope = wuq[..., :MLA_NOPE].reshape(MLA_Q_LORA, MLA_HEADS * MLA_NOPE)
    wrope = wuq[..., MLA_NOPE:]
    wr = _pad_last(wrope, LANES).reshape(MLA_Q_LORA, MLA_HEADS * LANES)
    wrr = _pad_last(_rotate_half_cols(wrope), LANES).reshape(MLA_Q_LORA, MLA_HEADS * LANES)
    wuk = jnp.transpose(w_uk, (1, 2, 0))
    wuvt = jnp.transpose(w_uv, (1, 2, 0))
    wuk_flat = w_uk.reshape(MLA_KV_LORA, MLA_HEADS * MLA_NOPE)
    wuv_flat_t = w_uv.reshape(MLA_KV_LORA, MLA_HEADS * MLA_V).T
    return dict(
        wdq=wdq.astype(BF16), wdkv=wdkv.astype(BF16), wdk=_pad_last(wdk, LANES).astype(BF16),
        wdkr=_pad_last(wdkr, LANES).astype(BF16), qn=q_norm[None, :], kvn=kv_norm[None, :],
        wnope=wnope.astype(BF16), wr=wr.astype(BF16), wrr=wrr.astype(BF16), wuk=wuk.astype(BF16),
        wuvt=wuvt.astype(BF16), wuk_flat=wuk_flat.astype(BF16), wuv_flat_t=wuv_flat_t.astype(BF16),
        wout=w_out.astype(BF16))


def kernel(x_prompt, x_sample, state_gla, cache_ckv, cache_kpe, norm_ffn1, w_ffn1_gate, w_ffn1_up, w_ffn1_down, norm_mix, norm_ffn2, w_ffn2_gate, w_ffn2_up, w_ffn2_down, w_gla_in, w_gla_gk_up, b_gla_gk, gla_out_norm, w_gla_out, w_mla_down, mla_q_norm, w_mla_uq, mla_kv_norm, w_mla_uk, w_mla_uv, w_mla_out, norm_final):
    depth = norm_ffn1.shape[0]
    n_mixers = 2
    d = x_prompt.shape[-1]
    mla_scale = (MLA_NOPE + MLA_ROPE) ** -0.5
    gf = norm_final[None, :]

    ffn1 = (norm_ffn1[:, None, :], w_ffn1_gate, w_ffn1_up, w_ffn1_down)
    ffn2 = (norm_ffn2[:, None, :], w_ffn2_gate, w_ffn2_up, w_ffn2_down)
    gla_w = [_prep_gla(w_gla_in[j], w_gla_gk_up[j], b_gla_gk[j], gla_out_norm[j], w_gla_out[j])
             for j in range(w_gla_in.shape[0])]
    mla_w = [_prep_mla(w_mla_down[j], mla_q_norm[j], w_mla_uq[j], mla_kv_norm[j], w_mla_uk[j], w_mla_uv[j],
                       w_mla_out[j]) for j in range(w_mla_down.shape[0])]

    def run(x, pos, gla_s0, past_ckv, past_kpe, *, gla_nb, gla_tl, attn_tq, proj_nb):
        bsz, seq, _ = x.shape
        cos, sin = _rope_tables(pos)
        new_gla, new_ckv, new_kpe = [], [], []
        pending = {}

        def ffn(x, w, layer, final_norm=False):
            return _ffn_call(x.reshape(bsz * seq, d), w[0], w[1], w[2], w[3], gf, layer,
                             final_norm=final_norm, **pending).reshape(bsz, seq, d)

        for i in range(depth):
            x = ffn(x, ffn1, i)
            j = i // n_mixers
            gm = norm_mix[i][None, :]
            if i % n_mixers == 0:
                x, s_fin = _gla_call(x, gla_s0[j], gm, *gla_w[j], nb=gla_nb, tl=gla_tl)
                new_gla.append(s_fin)
            else:
                w = mla_w[j]
                absorb = past_ckv is not None
                q, k, vt, ckv, kpe = _mla_proj_call(
                    x, cos, sin, gm, w["wdq"], w["wdkv"], w["wdk"], w["wdkr"], w["qn"], w["kvn"],
                    w["wnope"], w["wr"], w["wrr"], w["wuk"] if absorb else w["wuk_flat"], w["wuv_flat_t"],
                    tm=attn_tq, nbp=proj_nb, absorb=absorb, q_scale=1.0 if absorb else mla_scale * LOG2E)
                if absorb:
                    pc = past_ckv[j].astype(BF16)
                    pad = jnp.zeros(past_kpe[j].shape[:-1] + (KCAT - MLA_KV_LORA - MLA_ROPE,), BF16)
                    kfull = jnp.concatenate([pc, past_kpe[j].astype(BF16), pad], axis=-1)
                    vtfull = jnp.swapaxes(pc.reshape(bsz, -1, ATTN_KEY_TILE, MLA_KV_LORA), 2, 3)
                    x = _mla_attn_call(q, kfull, vtfull, k, vt, x, w["wuvt"], w["wout"],
                                       tq=attn_tq, tk=ATTN_KEY_TILE, scale=mla_scale)
                else:
                    pending = dict(attn_t=_mha_attn_call(q, k, vt, hp=ATTN_HEAD_GROUP, tq=attn_tq), w_out=w["wout"])
                new_ckv.append(ckv)
                new_kpe.append(kpe)
            x = ffn(x, ffn2, i, final_norm=(i == depth - 1))
            pending = {}
        return x, jnp.stack(new_gla), jnp.stack(new_ckv), jnp.stack(new_kpe)

    bp, lp = x_prompt.shape[0], x_prompt.shape[1]
    n_gla = w_gla_in.shape[0]
    s0_p = jnp.zeros((n_gla, bp) + state_gla.shape[2:], x_prompt.dtype)
    y_p, gla_p, ckv_p, kpe_p = run(
        x_prompt, jnp.arange(lp), s0_p, None, None, gla_nb=1, gla_tl=512, attn_tq=ATTN_KEY_TILE, proj_nb=1)

    past_len = cache_ckv.shape[2]
    ls = x_sample.shape[1]
    y_s, gla_s, ckv_s, kpe_s = run(
        x_sample, past_len + jnp.arange(ls), state_gla, cache_ckv, cache_kpe, gla_nb=8, gla_tl=ls, attn_tq=ls,
        proj_nb=8)

    return (y_p, y_s, gla_p, ckv_p, kpe_p, gla_s, ckv_s, kpe_s)
```

```python
import functools

import jax
import jax.numpy as jnp
from jax import lax
from jax.experimental import pallas as pl
from jax.experimental.pallas import tpu as pltpu

F32 = jnp.float32
BF16 = jnp.bfloat16

EPS = 1e-6
CHUNK = 64
GLA_HEADS = 4
GLA_GATE_NORM = 16.0
GLA_ROW_GROUPS = 2
MLA_HEADS = 8
MLA_NOPE = 128
MLA_ROPE = 64
MLA_V = 128
MLA_Q_LORA = 384
MLA_KV_LORA = 256
ROPE_THETA = 10000.0
LANES = 128
BF16_SUBLANES = 16
F32_SUBLANES = 8
MXU_COLS = 256
LOG2E = 1.4426950408889634
ATTN_COL_BLOCK = 512
ATTN_KEY_TILE = 512
ATTN_HEAD_GROUP = 4
ATTN_SEQ_GROUP = 2
V_ONES_ROWS = 16
ATTN_LAG_LIMIT = 64.0
KCAT = MLA_KV_LORA + LANES

VMEM_LIMIT = 56 * 1024 * 1024


def _rms(xf, g):
    return xf * lax.rsqrt(jnp.mean(xf * xf, axis=-1, keepdims=True) + EPS) * g


def _silu(a):
    return a * (1.0 / (1.0 + jnp.exp(-a)))


def _dot(a, b):
    return jnp.dot(a, b, preferred_element_type=F32)


def _dot_nt(a, b):
    return lax.dot_general(a, b, (((1,), (1,)), ((), ())), preferred_element_type=F32)


def _dot_tn(a, b):
    return lax.dot_general(a, b, (((0,), (0,)), ((), ())), preferred_element_type=F32)


def _resident(shape):
    zeros = (0,) * len(shape)
    return pl.BlockSpec(shape, lambda *_: zeros, pipeline_mode=pl.Buffered(1))


def _ffn_kernel(*refs, f_chunk, final_norm, mixer_proj):
    if mixer_proj:
        x_ref, ot_ref, wo_ref, g_ref, wg_ref, wu_ref, wd_ref, gf_ref, o_ref, hid_ref = refs
        x = x_ref[...] + _dot_tn(ot_ref[0], wo_ref[...])
    else:
        x_ref, g_ref, wg_ref, wu_ref, wd_ref, gf_ref, o_ref, hid_ref = refs
        x = x_ref[...]
    h = _rms(x, g_ref[0]).astype(BF16)
    d_ff = wg_ref.shape[2]
    for c in range(d_ff // f_chunk):
        sl = slice(c * f_chunk, (c + 1) * f_chunk)
        a = _dot(h, wg_ref[0, :, sl].astype(BF16))
        b = _dot(h, wu_ref[0, :, sl].astype(BF16))
        hid_ref[:, sl] = (_silu(a) * b).astype(BF16)
    y = x + 0.5 * _dot(hid_ref[...], wd_ref[0].astype(BF16))
    if final_norm:
        y = _rms(y, gf_ref[...])
    o_ref[...] = y


def _layer_resident(shape, layer):
    index = (layer,) + (0,) * (len(shape) - 1)
    return pl.BlockSpec((1,) + tuple(shape[1:]), lambda *_: index, pipeline_mode=pl.Buffered(1))


def _ffn_call(x, g, wg, wu, wd, gf, layer, *, final_norm, attn_t=None, w_out=None, tm=512, f_chunk=256):
    t, d = x.shape
    d_ff = wg.shape[2]
    assert t % tm == 0 and d_ff % f_chunk == 0
    mixer_proj = attn_t is not None
    x_spec = pl.BlockSpec((tm, d), lambda i: (i, 0))
    weights = (g, wg, wu, wd)
    if mixer_proj:
        tiles_per_seq = attn_t.shape[2] // tm
        assert attn_t.shape[0] * attn_t.shape[2] == t and attn_t.shape[2] % tm == 0
        ot_spec = pl.BlockSpec((1, attn_t.shape[1], tm), lambda i: (i // tiles_per_seq, 0, i % tiles_per_seq))
        operands = (x, attn_t, w_out) + weights + (gf,)
        in_specs = [x_spec, ot_spec, _resident(w_out.shape)]
    else:
        operands = (x,) + weights + (gf,)
        in_specs = [x_spec]
    return pl.pallas_call(
        functools.partial(_ffn_kernel, f_chunk=f_chunk, final_norm=final_norm, mixer_proj=mixer_proj),
        out_shape=jax.ShapeDtypeStruct((t, d), F32),
        grid=(t // tm,),
        in_specs=in_specs + [_layer_resident(w.shape, layer) for w in weights] + [_resident(gf.shape)],
        out_specs=pl.BlockSpec((tm, d), lambda i: (i, 0)),
        scratch_shapes=[pltpu.VMEM((tm, d_ff), BF16)],
        compiler_params=pltpu.CompilerParams(
            dimension_semantics=("parallel",), vmem_limit_bytes=VMEM_LIMIT),
        name="ffn",
    )(*operands)


def _split3(a):
    hi = a.astype(BF16)
    r = a - hi.astype(F32)
    mid = r.astype(BF16)
    lo = (r - mid.astype(F32)).astype(BF16)
    return hi, mid, lo


def _interleave(primary, filler):
    for n, step in enumerate(primary):
        step()
        if n < len(filler):
            filler[n]()
    for extra in filler[len(primary):]:
        extra()


def _gla_kernel(x_ref, s0_ref, gmix_ref, wq_ref, wk_ref, wv_ref, wg_ref, wlow_ref, wgk_ref, bgk_ref,
                gout_ref, wout_ref, y_ref, s_ref, st_s, *, nb, tl, n_sub):
    l = pl.program_id(1)
    heads = GLA_HEADS
    dk = wq_ref.shape[1] // heads
    dv = wv_ref.shape[1] // heads
    rows_sub = nb * tl // n_sub
    n_chunk = rows_sub // CHUNK
    d = x_ref.shape[-1]

    @pl.when(l == 0)
    def _():
        st_s[...] = s0_ref[...]

    ti = lax.broadcasted_iota(jnp.int32, (CHUNK, CHUNK), 0)
    si = lax.broadcasted_iota(jnp.int32, (CHUNK, CHUNK), 1)
    causal = si <= ti
    tri = jnp.where(causal, 1.0, 0.0).astype(BF16)

    def rows_of(ref, sub):
        row0 = sub * rows_sub
        if rows_sub >= tl:
            return ref.at[row0 // tl:(row0 + rows_sub) // tl]
        return ref.at[row0 // tl, row0 % tl:row0 % tl + rows_sub, :]

    def in_projection(sub):
        env = {}

        def start():
            env["x"] = rows_of(x_ref, sub)[...].reshape(rows_sub, d)
            env["h"] = _rms(env["x"], gmix_ref[...]).astype(BF16)

        def piece(name, ref, lo, hi):
            return lambda: env.setdefault(name, []).append(_dot(env["h"], ref[:, lo:hi]))

        thunks = [start]
        for name, ref in (("low", wlow_ref), ("q", wq_ref), ("k", wk_ref), ("v", wv_ref), ("g", wg_ref)):
            width = ref.shape[1]
            thunks += [piece(name, ref, lo, min(lo + MXU_COLS, width)) for lo in range(0, width, MXU_COLS)]
        return thunks, env

    def recurrence(sub, env):
        row0 = sub * rows_sub
        cat = lambda name: jnp.concatenate(env[name], axis=-1)
        loc = {}

        def start():
            loc["q"] = cat("q") * (dk ** -0.5)
            loc["k"] = cat("k")
            loc["v"] = cat("v").astype(BF16)
            z = _dot(cat("low").astype(BF16), wgk_ref[...]) + bgk_ref[...]
            loc["log_a"] = -(jnp.maximum(-z, 0.0) + jnp.log1p(jnp.exp(-jnp.abs(z)))) / GLA_GATE_NORM
            loc["state"] = {}
            loc["o"] = [[None] * n_chunk for _ in range(heads)]

        def cumsum(c):
            rs = slice(c * CHUNK, (c + 1) * CHUNK)
            a_hi, a_mid, a_lo = _split3(loc["log_a"][rs, :])
            loc["cum", c] = _dot(tri, a_hi) + _dot(tri, a_mid) + _dot(tri, a_lo)

        def scores(c):
            rs = slice(c * CHUNK, (c + 1) * CHUNK)
            cum = loc["cum", c]
            cum_last = cum[CHUNK - 1:CHUNK, :]
            q_dec = (loc["q"][rs, :] * jnp.exp(cum)).astype(BF16)
            k_inv = (loc["k"][rs, :] * jnp.exp(-cum)).astype(BF16)
            loc["k_tail", c] = (loc["k"][rs, :] * jnp.exp(cum_last - cum)).astype(BF16)
            loc["q_dec", c] = q_dec
            dec = jnp.broadcast_to(jnp.exp(cum_last), (F32_SUBLANES, heads * dk))
            loc["dec", c] = [dec[:, hd * dk:(hd + 1) * dk].T[:, :1] for hd in range(heads)]
            loc["scores", c] = [
                jnp.where(causal, _dot_nt(q_dec[:, hd * dk:(hd + 1) * dk], k_inv[:, hd * dk:(hd + 1) * dk]), 0.0)
                .astype(BF16) for hd in range(heads)]

        def values(c):
            rs = slice(c * CHUNK, (c + 1) * CHUNK)
            vs = [loc["v"][rs, hd * dv:(hd + 1) * dv] for hd in range(heads)]
            loc["o_intra", c] = [_dot(loc["scores", c][hd], vs[hd]) for hd in range(heads)]
            loc["kv", c] = [_dot_tn(loc["k_tail", c][:, hd * dk:(hd + 1) * dk], vs[hd]) for hd in range(heads)]

        def carry(c):
            b = (row0 + c * CHUNK) // tl
            for hd in range(heads):
                ks = slice(hd * dk, (hd + 1) * dk)
                st = loc["state"].get((b, hd))
                if st is None:
                    st = st_s[b, hd]
                loc["o"][hd][c] = loc["o_intra", c][hd] + _dot(loc["q_dec", c][:, ks], st.astype(BF16))
                loc["state"][(b, hd)] = st * loc["dec", c][hd] + loc["kv", c][hd]

        def finish():
            for (b, hd), st in loc["state"].items():
                st_s[b, hd] = st
            env["o"] = [jnp.concatenate(loc["o"][hd], axis=0) for hd in range(heads)]

        stages = (cumsum, scores, values)
        thunks = [start]
        for step in range(n_chunk + len(stages) - 1):
            for st in reversed(range(len(stages))):
                if 0 <= step - st < n_chunk:
                    thunks.append(functools.partial(stages[st], step - st))
        thunks += [functools.partial(carry, c) for c in range(n_chunk)] + [finish]
        return thunks

    def out_projection(sub, env):
        loc = {}

        def gate():
            g = jnp.concatenate(env["g"], axis=-1)
            loc["gated"] = jnp.concatenate(
                [(_rms(env["o"][hd], gout_ref[...]) * _silu(g[:, hd * dv:(hd + 1) * dv])).astype(BF16)
                 for hd in range(heads)], axis=-1)

        def piece(lo):
            return lambda: loc.setdefault("y", []).append(_dot(loc["gated"], wout_ref[:, lo:lo + MXU_COLS]))

        def store():
            y = env["x"] + jnp.concatenate(loc["y"], axis=-1)
            dst = rows_of(y_ref, sub)
            dst[...] = y.reshape(dst.shape)

        return [gate] + [piece(lo) for lo in range(0, d, MXU_COLS)] + [store]

    proj = [in_projection(sub) for sub in range(n_sub)]
    for step in proj[0][0]:
        step()
    for sub in range(n_sub):
        filler = []
        if sub + 1 < n_sub:
            filler += proj[sub + 1][0]
        if sub >= 1:
            filler += out_projection(sub - 1, proj[sub - 1][1])
        _interleave(recurrence(sub, proj[sub][1]), filler)
    for step in out_projection(n_sub - 1, proj[n_sub - 1][1]):
        step()

    @pl.when(l == pl.num_programs(1) - 1)
    def _():
        s_ref[...] = st_s[...]


def _gla_call(x, s0, gmix, wq, wk, wv, wg, wlow, wgk, bgk, gout, wout, *, nb, tl, n_sub=GLA_ROW_GROUPS):
    bsz, seq, d = x.shape
    heads = GLA_HEADS
    dk = wq.shape[1] // heads
    dv = wv.shape[1] // heads
    rows_sub = nb * tl // n_sub
    assert bsz % nb == 0 and seq % tl == 0 and tl % CHUNK == 0 and (nb * tl) % n_sub == 0
    assert rows_sub % CHUNK == 0 and (rows_sub % tl == 0 or tl % rows_sub == 0)
    return pl.pallas_call(
        functools.partial(_gla_kernel, nb=nb, tl=tl, n_sub=n_sub),
        out_shape=(jax.ShapeDtypeStruct((bsz, seq, d), F32),
                   jax.ShapeDtypeStruct((bsz, heads, dk, dv), F32)),
        grid=(bsz // nb, seq // tl),
        in_specs=[
            pl.BlockSpec((nb, tl, d), lambda b, l: (b, l, 0)),
            pl.BlockSpec((nb, heads, dk, dv), lambda b, l: (b, 0, 0, 0)),
            _resident(gmix.shape), _resident(wq.shape), _resident(wk.shape), _resident(wv.shape),
            _resident(wg.shape), _resident(wlow.shape), _resident(wgk.shape), _resident(bgk.shape),
            _resident(gout.shape), _resident(wout.shape),
        ],
        out_specs=(pl.BlockSpec((nb, tl, d), lambda b, l: (b, l, 0)),
                   pl.BlockSpec((nb, heads, dk, dv), lambda b, l: (b, 0, 0, 0))),
        scratch_shapes=[pltpu.VMEM((nb, heads, dk, dv), F32)],
        compiler_params=pltpu.CompilerParams(
            dimension_semantics=("parallel", "arbitrary"), vmem_limit_bytes=VMEM_LIMIT),
        name="gla",
    )(x, s0, gmix, wq, wk, wv, wg, wlow, wgk, bgk, gout, wout)


def _mla_proj_kernel(x_ref, rot_ref, gmix_ref, wdq_ref, wdkv_ref, wdk_ref, qn_ref, kvn_ref,
                     wnope_ref, wr_ref, wkey_ref, wval_ref,
                     q_ref, k_ref, vt_ref, ckv_ref, kpe_ref, *, absorb, q_scale):
    nbp, tm, d = x_ref.shape
    rows = nbp * tm
    x = x_ref[...].reshape(rows, d)
    h = _rms(x, gmix_ref[...]).astype(BF16)
    rot = jnp.concatenate([rot_ref[...]] * nbp, axis=0)

    def rope(pair):
        prod = pair * rot
        return prod + pltpu.roll(prod, MLA_ROPE, axis=1)

    ckv = _rms(_dot(h, wdkv_ref[...]), kvn_ref[...])
    rope_lanes = lax.broadcasted_iota(jnp.int32, (rows, LANES), 1) < MLA_ROPE
    kpe = jnp.where(rope_lanes, rope(_dot(h, wdk_ref[...])), 0.0)
    ckv_ref[...] = ckv.reshape(nbp, tm, MLA_KV_LORA)
    kpe_ref[...] = kpe[:, :MLA_ROPE].reshape(nbp, tm, MLA_ROPE)
    ckv_b = ckv.astype(BF16)
    kpe_b = kpe.astype(BF16)
    ckv_t = ckv.T.astype(BF16)

    hq = _rms(_dot(h, wdq_ref[...]), qn_ref[...]).astype(BF16)
    q_nope = _dot(hq, wnope_ref[...])
    q_r = _dot(hq, wr_ref[...])

    if absorb:
        k_ref[:, :, :MLA_KV_LORA] = ckv_b.reshape(nbp, tm, MLA_KV_LORA)
        k_ref[:, :, MLA_KV_LORA:] = kpe_b.reshape(nbp, tm, LANES)
        for b in range(nbp):
            vt_ref[b, 0] = ckv_t[:, b * tm:(b + 1) * tm]
    else:
        k_nope = _dot(ckv_b, wkey_ref[...]).astype(BF16)
        v_t = _dot(wval_ref[...], ckv_t).astype(BF16)
        ones = jnp.ones((V_ONES_ROWS, tm), BF16)
    for hd in range(MLA_HEADS):
        hs = slice(hd * LANES, (hd + 1) * LANES)
        q_pe = rope(q_r[:, hs])
        if absorb:
            q_lat = _dot(q_nope[:, hs].astype(BF16), wkey_ref[hd])
            q_t = jnp.concatenate([q_lat, q_pe], axis=-1).T.astype(BF16)
            for b in range(nbp):
                q_ref[b, 0, :, hd * tm:(hd + 1) * tm] = q_t[:, b * tm:(b + 1) * tm]
        else:
            for b in range(nbp):
                bs = slice(b * tm, (b + 1) * tm)
                q_ref[b, hd, 0, :MLA_NOPE] = (q_nope[bs, hs] * q_scale).T.astype(BF16)
                q_ref[b, hd, 0, MLA_NOPE:] = (q_pe[bs] * q_scale).T.astype(BF16)
                k_ref[b, hd, :, :MLA_NOPE] = k_nope[bs, hs]
                k_ref[b, hd, :, MLA_NOPE:] = kpe_b[bs]
                vt_ref[b, hd, 0, :MLA_V] = v_t[hd * MLA_V:(hd + 1) * MLA_V, bs]
                vt_ref[b, hd, 0, MLA_V:] = ones


def _mla_proj_call(x, rot, gmix, wdq, wdkv, wdk, qn, kvn, wnope, wr, wkey, wval, *, tm, nbp, absorb, q_scale=1.0):
    bsz, seq, d = x.shape
    assert seq % tm == 0 and bsz % nbp == 0
    weights = (gmix, wdq, wdkv, wdk, qn, kvn, wnope, wr, wkey, wval)
    nt = seq // tm
    if absorb:
        q_k_vt = (jax.ShapeDtypeStruct((bsz, nt, KCAT, MLA_HEADS * tm), BF16),
                  jax.ShapeDtypeStruct((bsz, seq, KCAT), BF16),
                  jax.ShapeDtypeStruct((bsz, nt, MLA_KV_LORA, tm), BF16))
        q_k_vt_specs = (pl.BlockSpec((nbp, 1, KCAT, MLA_HEADS * tm), lambda b, l: (b, l, 0, 0)),
                        pl.BlockSpec((nbp, tm, KCAT), lambda b, l: (b, l, 0)),
                        pl.BlockSpec((nbp, 1, MLA_KV_LORA, tm), lambda b, l: (b, l, 0, 0)))
    else:
        qk_dim = MLA_NOPE + LANES
        v_rows = MLA_V + V_ONES_ROWS
        q_k_vt = (jax.ShapeDtypeStruct((bsz, MLA_HEADS, nt, qk_dim, tm), BF16),
                  jax.ShapeDtypeStruct((bsz, MLA_HEADS, seq, qk_dim), BF16),
                  jax.ShapeDtypeStruct((bsz, MLA_HEADS, nt, v_rows, tm), BF16))
        q_k_vt_specs = (pl.BlockSpec((nbp, MLA_HEADS, 1, qk_dim, tm), lambda b, l: (b, 0, l, 0, 0)),
                        pl.BlockSpec((nbp, MLA_HEADS, tm, qk_dim), lambda b, l: (b, 0, l, 0)),
                        pl.BlockSpec((nbp, MLA_HEADS, 1, v_rows, tm), lambda b, l: (b, 0, l, 0, 0)))
    return pl.pallas_call(
        functools.partial(_mla_proj_kernel, absorb=absorb, q_scale=q_scale),
        out_shape=q_k_vt + (jax.ShapeDtypeStruct((bsz, seq, MLA_KV_LORA), F32),
                            jax.ShapeDtypeStruct((bsz, seq, MLA_ROPE), F32)),
        grid=(bsz // nbp, nt),
        in_specs=[
            pl.BlockSpec((nbp, tm, d), lambda b, l: (b, l, 0)),
            pl.BlockSpec((tm, LANES), lambda b, l: (l, 0)),
        ] + [_resident(w.shape) for w in weights],
        out_specs=q_k_vt_specs + (pl.BlockSpec((nbp, tm, MLA_KV_LORA), lambda b, l: (b, l, 0)),
                                  pl.BlockSpec((nbp, tm, MLA_ROPE), lambda b, l: (b, l, 0))),
        compiler_params=pltpu.CompilerParams(
            dimension_semantics=("parallel", "parallel"), vmem_limit_bytes=VMEM_LIMIT),
        name="mla_proj",
    )(x, rot, *weights)


def _exp2_scaled(t, c):
    return jnp.exp2(t if c == 1.0 else t * c)


def _tile_update(s, vt, cs, m_s, l_s, acc_s, lag_s, c, one_pass):
    m_prev = m_s[:, cs]
    mx = jnp.max(s, axis=0, keepdims=True)
    m_new = jnp.maximum(m_prev, mx)
    alpha = _exp2_scaled(m_prev - m_new, c)
    if one_pass:
        p = _exp2_scaled(s - m_prev, c)
        if l_s is not None:
            l_s[:, cs] = (l_s[:, cs] + jnp.sum(p, axis=0, keepdims=True)) * alpha
        acc_s[:, cs] = (acc_s[:, cs] + _dot(vt, p.astype(BF16))) * alpha
        lag_s[...] = jnp.maximum(lag_s[...], (mx - m_prev) * c)
    else:
        p = _exp2_scaled(s - m_new, c)
        if l_s is not None:
            l_s[:, cs] = alpha * l_s[:, cs] + jnp.sum(p, axis=0, keepdims=True)
        acc_s[:, cs] = acc_s[:, cs] * alpha + _dot(vt, p.astype(BF16))
    m_s[:, cs] = m_new


def _attend(n_full, n_blk, cb, first_key_scores, diag_scores, diag_values, tile_scores, tile_values,
            m_s, l_s, acc_s, lag_s, c):
    def cols(i):
        return slice(i * cb, (i + 1) * cb)

    def run(one_pass):
        lag_s[...] = jnp.zeros(lag_s.shape, F32)
        for i in range(n_blk):
            m_s[:, cols(i)] = first_key_scores(i)
            acc_s[:, cols(i)] = jnp.zeros((acc_s.shape[0], cb), F32)
            if l_s is not None:
                l_s[:, cols(i)] = jnp.zeros((1, cb), F32)

        def update(units):
            s_next = units[0][0]()
            for u, (_, values, i) in enumerate(units):
                s = s_next
                if u + 1 < len(units):
                    s_next = units[u + 1][0]()
                _tile_update(s, values(), cols(i), m_s, l_s, acc_s, lag_s, c, one_pass)

        def full_tiles(js):
            return [(functools.partial(tile_scores, j, i), functools.partial(tile_values, j, i), i)
                    for j in js for i in range(n_blk)]

        update([(functools.partial(diag_scores, i), functools.partial(diag_values, i), i) for i in range(n_blk)])

        def pair(jj, carry):
            update(full_tiles((2 * jj, 2 * jj + 1)))
            return carry

        lax.fori_loop(0, n_full // 2, pair, 0)
        if isinstance(n_full, int):
            if n_full % 2:
                update(full_tiles((n_full - 1,)))
        else:
            @pl.when(n_full % 2 == 1)
            def _():
                update(full_tiles((n_full - 1,)))

    run(one_pass=True)

    @pl.when(jnp.max(lag_s[...]) > ATTN_LAG_LIMIT)
    def _():
        run(one_pass=False)


def _block_causal(tq, cb, col0):
    ki = lax.broadcasted_iota(jnp.int32, (tq, cb), 0) // CHUNK
    qi = ((lax.broadcasted_iota(jnp.int32, (tq, cb), 1) + col0) % tq) // CHUNK
    return ki <= qi


def _mla_attn_kernel(qt_ref, kfull_ref, vtfull_ref, kdiag_ref, vtdiag_ref, x_ref, wuvt_ref, wout_ref, y_ref,
                     o_s, m_s, l_s, acc_s, lag_s, *, tq, tk, cb, past_tiles, single_block, scale):
    qb = pl.program_id(1)
    nbs = qt_ref.shape[0]
    heads = o_s.shape[0]
    blk_per_seq = heads * tq // cb

    def qt_blk(i):
        return qt_ref[i // blk_per_seq, 0, :, (i % blk_per_seq) * cb:(i % blk_per_seq + 1) * cb]

    def diag_scores(i):
        visible = _block_causal(tq, cb, (i % blk_per_seq) * cb)
        return jnp.where(visible, _dot(kdiag_ref[i // blk_per_seq], qt_blk(i)), -jnp.inf)

    def tile_scores(j, i):
        return _dot(kfull_ref[i // blk_per_seq, pl.ds(pl.multiple_of(j * tk, tk), tk), :], qt_blk(i))

    def first_key_scores(i):
        return _dot(kdiag_ref[i // blk_per_seq, :BF16_SUBLANES, :], qt_blk(i))[:1]

    n_full = past_tiles if single_block else past_tiles + qb * (tq // tk)
    _attend(n_full, nbs * blk_per_seq, cb, first_key_scores, diag_scores,
            lambda i: vtdiag_ref[i // blk_per_seq, 0], tile_scores, lambda j, i: vtfull_ref[i // blk_per_seq, j],
            m_s, l_s, acc_s, lag_s, scale * LOG2E)

    for s in range(nbs):
        for hd in range(heads):
            cs = slice((s * heads + hd) * tq, (s * heads + hd + 1) * tq)
            o_lat_t = (acc_s[:, cs] * (1.0 / l_s[:, cs])).astype(BF16)
            o_s[hd] = _dot(wuvt_ref[hd], o_lat_t).astype(BF16)
        o_t = o_s[...].reshape(o_s.shape[0] * o_s.shape[1], tq)
        y_ref[s] = x_ref[s] + _dot_tn(o_t, wout_ref[...])


def _mla_attn_call(qt, kfull, vtfull, kdiag, vtdiag, x, wuvt, wout, *, tq, tk, nbs, scale):
    bsz, seq, d = x.shape
    heads = wuvt.shape[0]
    lfull = kfull.shape[1]
    nq = seq // tq
    assert seq % tq == 0 and lfull % tk == 0 and bsz % nbs == 0 and qt.shape == (bsz, nq, KCAT, heads * tq)
    assert vtfull.shape == (bsz, lfull // tk, MLA_KV_LORA, tk) and vtdiag.shape == (bsz, nq, MLA_KV_LORA, tq)
    if nq == 1:
        past_tiles = lfull // tk
    else:
        assert tq == tk and lfull == seq
        past_tiles = 0
    cb = max(tq, min(heads * tq, ATTN_COL_BLOCK))
    assert cb % tq == 0 and (heads * tq) % cb == 0
    v_dim = wuvt.shape[1]
    return pl.pallas_call(
        functools.partial(_mla_attn_kernel, tq=tq, tk=tk, cb=cb, past_tiles=past_tiles, single_block=nq == 1,
                          scale=scale),
        out_shape=jax.ShapeDtypeStruct((bsz, seq, d), F32),
        grid=(bsz // nbs, nq),
        in_specs=[
            pl.BlockSpec((nbs, 1, KCAT, heads * tq), lambda b, i: (b, i, 0, 0)),
            pl.BlockSpec((nbs, lfull, KCAT), lambda b, i: (b, 0, 0)),
            pl.BlockSpec((nbs, lfull // tk, MLA_KV_LORA, tk), lambda b, i: (b, 0, 0, 0)),
            pl.BlockSpec((nbs, tq, KCAT), lambda b, i: (b, i, 0)),
            pl.BlockSpec((nbs, 1, MLA_KV_LORA, tq), lambda b, i: (b, i, 0, 0)),
            pl.BlockSpec((nbs, tq, d), lambda b, i: (b, i, 0)),
            _resident(wuvt.shape),
            _resident(wout.shape),
        ],
        out_specs=pl.BlockSpec((nbs, tq, d), lambda b, i: (b, i, 0)),
        scratch_shapes=[
            pltpu.VMEM((heads, v_dim, tq), BF16),
            pltpu.VMEM((1, nbs * heads * tq), F32),
            pltpu.VMEM((1, nbs * heads * tq), F32),
            pltpu.VMEM((MLA_KV_LORA, nbs * heads * tq), F32),
            pltpu.VMEM((1, cb), F32),
        ],
        compiler_params=pltpu.CompilerParams(
            dimension_semantics=("parallel", "parallel"), vmem_limit_bytes=VMEM_LIMIT),
        name="mla_attn",
    )(qt, kfull, vtfull, kdiag, vtdiag, x, wuvt, wout)


def _mha_attn_kernel(qt_ref, k_ref, vt_ref, kdiag_ref, vtdiag_ref, ot_ref, m_s, acc_s, lag_s, *, tq):
    qb = pl.program_id(2)
    hp = qt_ref.shape[1]
    visible = _block_causal(tq, tq, 0)

    def diag_scores(i):
        return jnp.where(visible, _dot(kdiag_ref[0, i], qt_ref[0, i, 0]), -jnp.inf)

    def tile_scores(j, i):
        return _dot(k_ref[0, i, pl.ds(pl.multiple_of(j * tq, tq), tq), :], qt_ref[0, i, 0])

    def first_key_scores(i):
        return _dot(kdiag_ref[0, i, :BF16_SUBLANES, :], qt_ref[0, i, 0])[:1]

    _attend(qb, hp, tq, first_key_scores, diag_scores, lambda i: vtdiag_ref[0, i, 0],
            tile_scores, lambda j, i: vt_ref[0, i, j], m_s, None, acc_s, lag_s, 1.0)

    for i in range(hp):
        cs = slice(i * tq, (i + 1) * tq)
        inv_l = 1.0 / acc_s[MLA_V:MLA_V + 1, cs]
        ot_ref[0, i * MLA_V:(i + 1) * MLA_V, :] = (acc_s[:MLA_V, cs] * inv_l).astype(BF16)


def _mha_attn_call(qt, k, vt, *, hp, tq):
    bsz, heads, seq, qk_dim = k.shape
    nq = seq // tq
    v_rows = MLA_V + V_ONES_ROWS
    assert qt.shape == (bsz, heads, nq, qk_dim, tq)
    assert seq % tq == 0 and heads % hp == 0 and vt.shape == (bsz, heads, nq, v_rows, tq)
    return pl.pallas_call(
        functools.partial(_mha_attn_kernel, tq=tq),
        out_shape=jax.ShapeDtypeStruct((bsz, heads * MLA_V, seq), BF16),
        grid=(bsz, heads // hp, nq),
        in_specs=[
            pl.BlockSpec((1, hp, 1, qk_dim, tq), lambda b, g, i: (b, g, i, 0, 0)),
            pl.BlockSpec((1, hp, seq, qk_dim), lambda b, g, i: (b, g, 0, 0)),
            pl.BlockSpec((1, hp, nq, v_rows, tq), lambda b, g, i: (b, g, 0, 0, 0), pipeline_mode=pl.Buffered(1)),
            pl.BlockSpec((1, hp, tq, qk_dim), lambda b, g, i: (b, g, i, 0)),
            pl.BlockSpec((1, hp, 1, v_rows, tq), lambda b, g, i: (b, g, i, 0, 0)),
        ],
        out_specs=pl.BlockSpec((1, hp * MLA_V, tq), lambda b, g, i: (b, g, i)),
        scratch_shapes=[
            pltpu.VMEM((1, hp * tq), F32),
            pltpu.VMEM((v_rows, hp * tq), F32),
            pltpu.VMEM((1, tq), F32),
        ],
        compiler_params=pltpu.CompilerParams(
            dimension_semantics=("parallel", "parallel", "arbitrary"), vmem_limit_bytes=VMEM_LIMIT),
        name="mha_attn",
    )(qt, k, vt, k, vt)


def _rope_table(pos):
    half = MLA_ROPE // 2
    inv = ROPE_THETA ** (-jnp.arange(half, dtype=F32) / half)
    ang = pos.astype(F32)[:, None] * inv[None, :]
    cos, sin = jnp.cos(ang), jnp.sin(ang)
    return jnp.concatenate([cos, cos, sin, sin], axis=-1)


def _rotate_half_cols(w):
    half = w.shape[-1] // 2
    return jnp.concatenate([-w[..., half:], w[..., :half]], axis=-1)


def _pad_last(w, n):
    return jnp.concatenate([w, jnp.zeros(w.shape[:-1] + (n - w.shape[-1],), w.dtype)], axis=-1)


def _prep_gla(w_in, w_gk_up, b_gk, g_norm, w_out):
    qk = w_gk_up.shape[1]
    vw = w_out.shape[0]
    wq = w_in[:, :qk]
    wk = w_in[:, qk:2 * qk]
    wv = w_in[:, 2 * qk:2 * qk + vw]
    wg = w_in[:, 2 * qk + vw:2 * qk + 2 * vw]
    wlow = _pad_last(w_in[:, 2 * qk + 2 * vw:], LANES)
    rank = w_gk_up.shape[0]
    wgk = jnp.concatenate([w_gk_up, jnp.zeros((LANES - rank, qk), w_gk_up.dtype)], axis=0)
    return (wq.astype(BF16), wk.astype(BF16), wv.astype(BF16), wg.astype(BF16), wlow.astype(BF16),
            wgk.astype(BF16), b_gk[None, :], g_norm[None, :], w_out.astype(BF16))


def _prep_mla(w_down, q_norm, w_uq, kv_norm, w_uk, w_uv, w_out):
    wdq = w_down[:, :MLA_Q_LORA]
    wdkv = w_down[:, MLA_Q_LORA:MLA_Q_LORA + MLA_KV_LORA]
    wdk = w_down[:, MLA_Q_LORA + MLA_KV_LORA:]
    wdk = jnp.concatenate([wdk, _rotate_half_cols(wdk)], axis=-1)
    wuq = w_uq.reshape(MLA_Q_LORA, MLA_HEADS, MLA_NOPE + MLA_ROPE)
    wnope = wuq[..., :MLA_NOPE].reshape(MLA_Q_LORA, MLA_HEADS * MLA_NOPE)
    wrope = wuq[..., MLA_NOPE:]
    wr = jnp.concatenate([wrope, _rotate_half_cols(wrope)], axis=-1).reshape(MLA_Q_LORA, MLA_HEADS * LANES)
    wuk = jnp.transpose(w_uk, (1, 2, 0))
    wuvt = jnp.transpose(w_uv, (1, 2, 0))
    wuk_flat = w_uk.reshape(MLA_KV_LORA, MLA_HEADS * MLA_NOPE)
    wuv_flat_t = w_uv.reshape(MLA_KV_LORA, MLA_HEADS * MLA_V).T
    return dict(
        wdq=wdq.astype(BF16), wdkv=wdkv.astype(BF16), wdk=wdk.astype(BF16), qn=q_norm[None, :],
        kvn=kv_norm[None, :], wnope=wnope.astype(BF16), wr=wr.astype(BF16), wuk=wuk.astype(BF16),
        wuvt=wuvt.astype(BF16), wuk_flat=wuk_flat.astype(BF16), wuv_flat_t=wuv_flat_t.astype(BF16),
        wout=w_out.astype(BF16))


def kernel(x_prompt, x_sample, state_gla, cache_ckv, cache_kpe, norm_ffn1, w_ffn1_gate, w_ffn1_up, w_ffn1_down, norm_mix, norm_ffn2, w_ffn2_gate, w_ffn2_up, w_ffn2_down, w_gla_in, w_gla_gk_up, b_gla_gk, gla_out_norm, w_gla_out, w_mla_down, mla_q_norm, w_mla_uq, mla_kv_norm, w_mla_uk, w_mla_uv, w_mla_out, norm_final):
    depth = norm_ffn1.shape[0]
    n_mixers = 2
    d = x_prompt.shape[-1]
    mla_scale = (MLA_NOPE + MLA_ROPE) ** -0.5
    gf = norm_final[None, :]

    ffn1 = (norm_ffn1[:, None, :], w_ffn1_gate, w_ffn1_up, w_ffn1_down)
    ffn2 = (norm_ffn2[:, None, :], w_ffn2_gate, w_ffn2_up, w_ffn2_down)
    gla_w = [_prep_gla(w_gla_in[j], w_gla_gk_up[j], b_gla_gk[j], gla_out_norm[j], w_gla_out[j])
             for j in range(w_gla_in.shape[0])]
    mla_w = [_prep_mla(w_mla_down[j], mla_q_norm[j], w_mla_uq[j], mla_kv_norm[j], w_mla_uk[j], w_mla_uv[j],
                       w_mla_out[j]) for j in range(w_mla_down.shape[0])]

    def run(x, pos, gla_s0, past_ckv, past_kpe, *, gla_nb, gla_tl, attn_tq, proj_nb):
        bsz, seq, _ = x.shape
        rot = _rope_table(pos)
        new_gla, new_ckv, new_kpe = [], [], []
        pending = {}

        def ffn(x, w, layer, final_norm=False):
            return _ffn_call(x.reshape(bsz * seq, d), w[0], w[1], w[2], w[3], gf, layer,
                             final_norm=final_norm, **pending).reshape(bsz, seq, d)

        for i in range(depth):
            x = ffn(x, ffn1, i)
            j = i // n_mixers
            gm = norm_mix[i][None, :]
            if i % n_mixers == 0:
                x, s_fin = _gla_call(x, gla_s0[j], gm, *gla_w[j], nb=gla_nb, tl=gla_tl)
                new_gla.append(s_fin)
            else:
                w = mla_w[j]
                absorb = past_ckv is not None
                q, k, vt, ckv, kpe = _mla_proj_call(
                    x, rot, gm, w["wdq"], w["wdkv"], w["wdk"], w["qn"], w["kvn"],
                    w["wnope"], w["wr"], w["wuk"] if absorb else w["wuk_flat"], w["wuv_flat_t"],
                    tm=attn_tq, nbp=proj_nb, absorb=absorb, q_scale=1.0 if absorb else mla_scale * LOG2E)
                if absorb:
                    pc = past_ckv[j].astype(BF16)
                    pad = jnp.zeros(past_kpe[j].shape[:-1] + (KCAT - MLA_KV_LORA - MLA_ROPE,), BF16)
                    kfull = jnp.concatenate([pc, past_kpe[j].astype(BF16), pad], axis=-1)
                    vtfull = jnp.swapaxes(pc.reshape(bsz, -1, ATTN_KEY_TILE, MLA_KV_LORA), 2, 3)
                    x = _mla_attn_call(q, kfull, vtfull, k, vt, x, w["wuvt"], w["wout"],
                                       tq=attn_tq, tk=ATTN_KEY_TILE, nbs=ATTN_SEQ_GROUP, scale=mla_scale)
                else:
                    pending = dict(attn_t=_mha_attn_call(q, k, vt, hp=ATTN_HEAD_GROUP, tq=attn_tq), w_out=w["wout"])
                new_ckv.append(ckv)
                new_kpe.append(kpe)
            x = ffn(x, ffn2, i, final_norm=(i == depth - 1))
            pending = {}
        return x, jnp.stack(new_gla), jnp.stack(new_ckv), jnp.stack(new_kpe)

    bp, lp = x_prompt.shape[0], x_prompt.shape[1]
    n_gla = w_gla_in.shape[0]
    s0_p = jnp.zeros((n_gla, bp) + state_gla.shape[2:], x_prompt.dtype)
    y_p, gla_p, ckv_p, kpe_p = run(
        x_prompt, jnp.arange(lp), s0_p, None, None, gla_nb=1, gla_tl=512, attn_tq=ATTN_KEY_TILE, proj_nb=1)

    past_len = cache_ckv.shape[2]
    ls = x_sample.shape[1]
    y_s, gla_s, ckv_s, kpe_s = run(
        x_sample, past_len + jnp.arange(ls), state_gla, cache_ckv, cache_kpe, gla_nb=8, gla_tl=ls, attn_tq=ls,
        proj_nb=8)

    return (y_p, y_s, gla_p, ckv_p, kpe_p, gla_s, ckv_s, kpe_s)
```

```python
import functools

import jax
import jax.numpy as jnp
from jax import lax
from jax.experimental import pallas as pl
from jax.experimental.pallas import tpu as pltpu

F32 = jnp.float32
BF16 = jnp.bfloat16

EPS = 1e-6
CHUNK = 64
GLA_HEADS = 4
GLA_GATE_NORM = 16.0
GLA_ROW_GROUPS = 2
MLA_HEADS = 8
MLA_NOPE = 128
MLA_ROPE = 64
MLA_V = 128
MLA_Q_LORA = 384
MLA_KV_LORA = 256
ROPE_THETA = 10000.0
LANES = 128
BF16_SUBLANES = 16
F32_SUBLANES = 8
MXU_COLS = 256
LOG2E = 1.4426950408889634
ATTN_COL_BLOCK = 512
ATTN_KEY_TILE = 512
ATTN_HEAD_GROUP = 4
ATTN_SEQ_GROUP = 2
V_ONES_ROWS = 16
ATTN_LAG_LIMIT = 64.0
KCAT = MLA_KV_LORA + LANES

VMEM_LIMIT = 56 * 1024 * 1024


def _rms(xf, g):
    return xf * lax.rsqrt(jnp.mean(xf * xf, axis=-1, keepdims=True) + EPS) * g


def _silu(a):
    return a * (1.0 / (1.0 + jnp.exp(-a)))


def _dot(a, b):
    return jnp.dot(a, b, preferred_element_type=F32)


def _dot_nt(a, b):
    return lax.dot_general(a, b, (((1,), (1,)), ((), ())), preferred_element_type=F32)


def _dot_tn(a, b):
    return lax.dot_general(a, b, (((0,), (0,)), ((), ())), preferred_element_type=F32)


def _resident(shape):
    zeros = (0,) * len(shape)
    return pl.BlockSpec(shape, lambda *_: zeros, pipeline_mode=pl.Buffered(1))


def _ffn_kernel(*refs, f_chunk, final_norm, mixer_proj):
    if mixer_proj:
        x_ref, ot_ref, wo_ref, g_ref, wg_ref, wu_ref, wd_ref, gf_ref, o_ref, hid_ref = refs
        x = x_ref[...] + _dot_tn(ot_ref[0], wo_ref[...])
    else:
        x_ref, g_ref, wg_ref, wu_ref, wd_ref, gf_ref, o_ref, hid_ref = refs
        x = x_ref[...]
    h = _rms(x, g_ref[0]).astype(BF16)
    d_ff = wg_ref.shape[2]
    for c in range(d_ff // f_chunk):
        sl = slice(c * f_chunk, (c + 1) * f_chunk)
        a = _dot(h, wg_ref[0, :, sl].astype(BF16))
        b = _dot(h, wu_ref[0, :, sl].astype(BF16))
        hid_ref[:, sl] = (_silu(a) * b).astype(BF16)
    y = x + 0.5 * _dot(hid_ref[...], wd_ref[0].astype(BF16))
    if final_norm:
        y = _rms(y, gf_ref[...])
    o_ref[...] = y


def _layer_resident(shape, layer):
    index = (layer,) + (0,) * (len(shape) - 1)
    return pl.BlockSpec((1,) + tuple(shape[1:]), lambda *_: index, pipeline_mode=pl.Buffered(1))


def _ffn_call(x, g, wg, wu, wd, gf, layer, *, final_norm, attn_t=None, w_out=None, tm=512, f_chunk=256):
    t, d = x.shape
    d_ff = wg.shape[2]
    assert t % tm == 0 and d_ff % f_chunk == 0
    mixer_proj = attn_t is not None
    x_spec = pl.BlockSpec((tm, d), lambda i: (i, 0))
    weights = (g, wg, wu, wd)
    if mixer_proj:
        tiles_per_seq = attn_t.shape[2] // tm
        assert attn_t.shape[0] * attn_t.shape[2] == t and attn_t.shape[2] % tm == 0
        ot_spec = pl.BlockSpec((1, attn_t.shape[1], tm), lambda i: (i // tiles_per_seq, 0, i % tiles_per_seq))
        operands = (x, attn_t, w_out) + weights + (gf,)
        in_specs = [x_spec, ot_spec, _resident(w_out.shape)]
    else:
        operands = (x,) + weights + (gf,)
        in_specs = [x_spec]
    return pl.pallas_call(
        functools.partial(_ffn_kernel, f_chunk=f_chunk, final_norm=final_norm, mixer_proj=mixer_proj),
        out_shape=jax.ShapeDtypeStruct((t, d), F32),
        grid=(t // tm,),
        in_specs=in_specs + [_layer_resident(w.shape, layer) for w in weights] + [_resident(gf.shape)],
        out_specs=pl.BlockSpec((tm, d), lambda i: (i, 0)),
        scratch_shapes=[pltpu.VMEM((tm, d_ff), BF16)],
        compiler_params=pltpu.CompilerParams(
            dimension_semantics=("parallel",), vmem_limit_bytes=VMEM_LIMIT),
        name="ffn",
    )(*operands)


def _split3(a):
    hi = a.astype(BF16)
    r = a - hi.astype(F32)
    mid = r.astype(BF16)
    lo = (r - mid.astype(F32)).astype(BF16)
    return hi, mid, lo


def _interleave(primary, filler):
    for n, step in enumerate(primary):
        step()
        if n < len(filler):
            filler[n]()
    for extra in filler[len(primary):]:
        extra()


def _gla_kernel(x_ref, s0_ref, gmix_ref, wq_ref, wk_ref, wv_ref, wg_ref, wlow_ref, wgk_ref, bgk_ref,
                gout_ref, wout_ref, y_ref, s_ref, st_s, *, nb, tl, n_sub):
    l = pl.program_id(1)
    heads = GLA_HEADS
    dk = wq_ref.shape[1] // heads
    dv = wv_ref.shape[1] // heads
    rows_sub = nb * tl // n_sub
    n_chunk = rows_sub // CHUNK
    d = x_ref.shape[-1]

    @pl.when(l == 0)
    def _():
        st_s[...] = s0_ref[...]

    ti = lax.broadcasted_iota(jnp.int32, (CHUNK, CHUNK), 0)
    si = lax.broadcasted_iota(jnp.int32, (CHUNK, CHUNK), 1)
    causal = si <= ti
    tri = jnp.where(causal, 1.0, 0.0).astype(BF16)

    def rows_of(ref, sub):
        row0 = sub * rows_sub
        if rows_sub >= tl:
            return ref.at[row0 // tl:(row0 + rows_sub) // tl]
        return ref.at[row0 // tl, row0 % tl:row0 % tl + rows_sub, :]

    def in_projection(sub):
        env = {}

        def start():
            env["x"] = rows_of(x_ref, sub)[...].reshape(rows_sub, d)
            env["h"] = _rms(env["x"], gmix_ref[...]).astype(BF16)

        def piece(name, ref, lo, hi):
            return lambda: env.setdefault(name, []).append(_dot(env["h"], ref[:, lo:hi]))

        thunks = [start]
        for name, ref in (("low", wlow_ref), ("q", wq_ref), ("k", wk_ref), ("v", wv_ref), ("g", wg_ref)):
            width = ref.shape[1]
            thunks += [piece(name, ref, lo, min(lo + MXU_COLS, width)) for lo in range(0, width, MXU_COLS)]
        return thunks, env

    def recurrence(sub, env):
        row0 = sub * rows_sub
        cat = lambda name: jnp.concatenate(env[name], axis=-1)
        loc = {}

        def start():
            loc["q"] = cat("q") * (dk ** -0.5)
            loc["k"] = cat("k")
            loc["v"] = cat("v").astype(BF16)
            z = _dot(cat("low").astype(BF16), wgk_ref[...]) + bgk_ref[...]
            loc["log_a"] = -(jnp.maximum(-z, 0.0) + jnp.log1p(jnp.exp(-jnp.abs(z)))) / GLA_GATE_NORM
            loc["state"] = {}
            loc["o"] = [[None] * n_chunk for _ in range(heads)]

        def cumsum(c):
            rs = slice(c * CHUNK, (c + 1) * CHUNK)
            a_hi, a_mid, a_lo = _split3(loc["log_a"][rs, :])
            loc["cum", c] = _dot(tri, a_hi) + _dot(tri, a_mid) + _dot(tri, a_lo)

        def scores(c):
            rs = slice(c * CHUNK, (c + 1) * CHUNK)
            cum = loc["cum", c]
            cum_last = cum[CHUNK - 1:CHUNK, :]
            q_dec = (loc["q"][rs, :] * jnp.exp(cum)).astype(BF16)
            k_inv = (loc["k"][rs, :] * jnp.exp(-cum)).astype(BF16)
            loc["k_tail", c] = (loc["k"][rs, :] * jnp.exp(cum_last - cum)).astype(BF16)
            loc["q_dec", c] = q_dec
            dec = jnp.broadcast_to(jnp.exp(cum_last), (F32_SUBLANES, heads * dk))
            loc["dec", c] = [dec[:, hd * dk:(hd + 1) * dk].T[:, :1] for hd in range(heads)]
            loc["scores", c] = [
                jnp.where(causal, _dot_nt(q_dec[:, hd * dk:(hd + 1) * dk], k_inv[:, hd * dk:(hd + 1) * dk]), 0.0)
                .astype(BF16) for hd in range(heads)]

        def values(c):
            rs = slice(c * CHUNK, (c + 1) * CHUNK)
            vs = [loc["v"][rs, hd * dv:(hd + 1) * dv] for hd in range(heads)]
            loc["o_intra", c] = [_dot(loc["scores", c][hd], vs[hd]) for hd in range(heads)]
            loc["kv", c] = [_dot_tn(loc["k_tail", c][:, hd * dk:(hd + 1) * dk], vs[hd]) for hd in range(heads)]

        def carry(c):
            b = (row0 + c * CHUNK) // tl
            for hd in range(heads):
                ks = slice(hd * dk, (hd + 1) * dk)
                st = loc["state"].get((b, hd))
                if st is None:
                    st = st_s[b, hd]
                loc["o"][hd][c] = loc["o_intra", c][hd] + _dot(loc["q_dec", c][:, ks], st.astype(BF16))
                loc["state"][(b, hd)] = st * loc["dec", c][hd] + loc["kv", c][hd]

        def finish():
            for (b, hd), st in loc["state"].items():
                st_s[b, hd] = st
            env["o"] = [jnp.concatenate(loc["o"][hd], axis=0) for hd in range(heads)]

        stages = (cumsum, scores, values, carry)
        thunks = [start]
        for step in range(n_chunk + len(stages) - 1):
            for st in reversed(range(len(stages))):
                if 0 <= step - st < n_chunk:
                    thunks.append(functools.partial(stages[st], step - st))
        return thunks + [finish]

    def out_projection(sub, env):
        loc = {}

        def gate():
            g = jnp.concatenate(env["g"], axis=-1)
            loc["gated"] = jnp.concatenate(
                [(_rms(env["o"][hd], gout_ref[...]) * _silu(g[:, hd * dv:(hd + 1) * dv])).astype(BF16)
                 for hd in range(heads)], axis=-1)

        def piece(lo):
            return lambda: loc.setdefault("y", []).append(_dot(loc["gated"], wout_ref[:, lo:lo + MXU_COLS]))

        def store():
            y = env["x"] + jnp.concatenate(loc["y"], axis=-1)
            dst = rows_of(y_ref, sub)
            dst[...] = y.reshape(dst.shape)

        return [gate] + [piece(lo) for lo in range(0, d, MXU_COLS)] + [store]

    proj = [in_projection(sub) for sub in range(n_sub)]
    for step in proj[0][0]:
        step()
    for sub in range(n_sub):
        filler = []
        if sub + 1 < n_sub:
            filler += proj[sub + 1][0]
        if sub >= 1:
            filler += out_projection(sub - 1, proj[sub - 1][1])
        _interleave(recurrence(sub, proj[sub][1]), filler)
    for step in out_projection(n_sub - 1, proj[n_sub - 1][1]):
        step()

    @pl.when(l == pl.num_programs(1) - 1)
    def _():
        s_ref[...] = st_s[...]


def _gla_call(x, s0, gmix, wq, wk, wv, wg, wlow, wgk, bgk, gout, wout, *, nb, tl, n_sub=GLA_ROW_GROUPS):
    bsz, seq, d = x.shape
    heads = GLA_HEADS
    dk = wq.shape[1] // heads
    dv = wv.shape[1] // heads
    rows_sub = nb * tl // n_sub
    assert bsz % nb == 0 and seq % tl == 0 and tl % CHUNK == 0 and (nb * tl) % n_sub == 0
    assert rows_sub % CHUNK == 0 and (rows_sub % tl == 0 or tl % rows_sub == 0)
    return pl.pallas_call(
        functools.partial(_gla_kernel, nb=nb, tl=tl, n_sub=n_sub),
        out_shape=(jax.ShapeDtypeStruct((bsz, seq, d), F32),
                   jax.ShapeDtypeStruct((bsz, heads, dk, dv), F32)),
        grid=(bsz // nb, seq // tl),
        in_specs=[
            pl.BlockSpec((nb, tl, d), lambda b, l: (b, l, 0)),
            pl.BlockSpec((nb, heads, dk, dv), lambda b, l: (b, 0, 0, 0)),
            _resident(gmix.shape), _resident(wq.shape), _resident(wk.shape), _resident(wv.shape),
            _resident(wg.shape), _resident(wlow.shape), _resident(wgk.shape), _resident(bgk.shape),
            _resident(gout.shape), _resident(wout.shape),
        ],
        out_specs=(pl.BlockSpec((nb, tl, d), lambda b, l: (b, l, 0)),
                   pl.BlockSpec((nb, heads, dk, dv), lambda b, l: (b, 0, 0, 0))),
        scratch_shapes=[pltpu.VMEM((nb, heads, dk, dv), F32)],
        compiler_params=pltpu.CompilerParams(
            dimension_semantics=("parallel", "arbitrary"), vmem_limit_bytes=VMEM_LIMIT),
        name="gla",
    )(x, s0, gmix, wq, wk, wv, wg, wlow, wgk, bgk, gout, wout)


def _mla_proj_kernel(x_ref, rot_ref, gmix_ref, wdq_ref, wdkv_ref, wdk_ref, qn_ref, kvn_ref,
                     wnope_ref, wr_ref, wkey_ref, wval_ref,
                     q_ref, k_ref, vt_ref, ckv_ref, kpe_ref, *, absorb, q_scale):
    nbp, tm, d = x_ref.shape
    rows = nbp * tm
    x = x_ref[...].reshape(rows, d)
    h = _rms(x, gmix_ref[...]).astype(BF16)
    rot = jnp.concatenate([rot_ref[...]] * nbp, axis=0)

    def rope(pair):
        prod = pair * rot
        return prod + pltpu.roll(prod, MLA_ROPE, axis=1)

    ckv = _rms(_dot(h, wdkv_ref[...]), kvn_ref[...])
    rope_lanes = lax.broadcasted_iota(jnp.int32, (rows, LANES), 1) < MLA_ROPE
    kpe = jnp.where(rope_lanes, rope(_dot(h, wdk_ref[...])), 0.0)
    ckv_ref[...] = ckv.reshape(nbp, tm, MLA_KV_LORA)
    kpe_ref[...] = kpe[:, :MLA_ROPE].reshape(nbp, tm, MLA_ROPE)
    ckv_b = ckv.astype(BF16)
    kpe_b = kpe.astype(BF16)
    ckv_t = ckv.T.astype(BF16)

    hq = _rms(_dot(h, wdq_ref[...]), qn_ref[...]).astype(BF16)
    q_nope = _dot(hq, wnope_ref[...])
    q_r = _dot(hq, wr_ref[...])

    if absorb:
        k_ref[:, :, :MLA_KV_LORA] = ckv_b.reshape(nbp, tm, MLA_KV_LORA)
        k_ref[:, :, MLA_KV_LORA:] = kpe_b.reshape(nbp, tm, LANES)
        for b in range(nbp):
            vt_ref[b, 0] = ckv_t[:, b * tm:(b + 1) * tm]
    else:
        k_nope = _dot(ckv_b, wkey_ref[...]).astype(BF16)
        v_t = _dot(wval_ref[...], ckv_t).astype(BF16)
        ones = jnp.ones((V_ONES_ROWS, tm), BF16)
    for hd in range(MLA_HEADS):
        hs = slice(hd * LANES, (hd + 1) * LANES)
        q_pe = rope(q_r[:, hs])
        if absorb:
            q_lat = _dot(q_nope[:, hs].astype(BF16), wkey_ref[hd])
            q_t = jnp.concatenate([q_lat, q_pe], axis=-1).T.astype(BF16)
            for b in range(nbp):
                q_ref[b, 0, :, hd * tm:(hd + 1) * tm] = q_t[:, b * tm:(b + 1) * tm]
        else:
            for b in range(nbp):
                bs = slice(b * tm, (b + 1) * tm)
                q_ref[b, hd, 0, :MLA_NOPE] = (q_nope[bs, hs] * q_scale).T.astype(BF16)
                q_ref[b, hd, 0, MLA_NOPE:] = (q_pe[bs] * q_scale).T.astype(BF16)
                k_ref[b, hd, :, :MLA_NOPE] = k_nope[bs, hs]
                k_ref[b, hd, :, MLA_NOPE:] = kpe_b[bs]
                vt_ref[b, hd, 0, :MLA_V] = v_t[hd * MLA_V:(hd + 1) * MLA_V, bs]
                vt_ref[b, hd, 0, MLA_V:] = ones


def _mla_proj_call(x, rot, gmix, wdq, wdkv, wdk, qn, kvn, wnope, wr, wkey, wval, *, tm, nbp, absorb, q_scale=1.0):
    bsz, seq, d = x.shape
    assert seq % tm == 0 and bsz % nbp == 0
    weights = (gmix, wdq, wdkv, wdk, qn, kvn, wnope, wr, wkey, wval)
    nt = seq // tm
    if absorb:
        q_k_vt = (jax.ShapeDtypeStruct((bsz, nt, KCAT, MLA_HEADS * tm), BF16),
                  jax.ShapeDtypeStruct((bsz, seq, KCAT), BF16),
                  jax.ShapeDtypeStruct((bsz, nt, MLA_KV_LORA, tm), BF16))
        q_k_vt_specs = (pl.BlockSpec((nbp, 1, KCAT, MLA_HEADS * tm), lambda b, l: (b, l, 0, 0)),
                        pl.BlockSpec((nbp, tm, KCAT), lambda b, l: (b, l, 0)),
                        pl.BlockSpec((nbp, 1, MLA_KV_LORA, tm), lambda b, l: (b, l, 0, 0)))
    else:
        qk_dim = MLA_NOPE + LANES
        v_rows = MLA_V + V_ONES_ROWS
        q_k_vt = (jax.ShapeDtypeStruct((bsz, MLA_HEADS, nt, qk_dim, tm), BF16),
                  jax.ShapeDtypeStruct((bsz, MLA_HEADS, seq, qk_dim), BF16),
                  jax.ShapeDtypeStruct((bsz, MLA_HEADS, nt, v_rows, tm), BF16))
        q_k_vt_specs = (pl.BlockSpec((nbp, MLA_HEADS, 1, qk_dim, tm), lambda b, l: (b, 0, l, 0, 0)),
                        pl.BlockSpec((nbp, MLA_HEADS, tm, qk_dim), lambda b, l: (b, 0, l, 0)),
                        pl.BlockSpec((nbp, MLA_HEADS, 1, v_rows, tm), lambda b, l: (b, 0, l, 0, 0)))
    return pl.pallas_call(
        functools.partial(_mla_proj_kernel, absorb=absorb, q_scale=q_scale),
        out_shape=q_k_vt + (jax.ShapeDtypeStruct((bsz, seq, MLA_KV_LORA), F32),
                            jax.ShapeDtypeStruct((bsz, seq, MLA_ROPE), F32)),
        grid=(bsz // nbp, nt),
        in_specs=[
            pl.BlockSpec((nbp, tm, d), lambda b, l: (b, l, 0)),
            pl.BlockSpec((tm, LANES), lambda b, l: (l, 0)),
        ] + [_resident(w.shape) for w in weights],
        out_specs=q_k_vt_specs + (pl.BlockSpec((nbp, tm, MLA_KV_LORA), lambda b, l: (b, l, 0)),
                                  pl.BlockSpec((nbp, tm, MLA_ROPE), lambda b, l: (b, l, 0))),
        compiler_params=pltpu.CompilerParams(
            dimension_semantics=("parallel", "parallel"), vmem_limit_bytes=VMEM_LIMIT),
        name="mla_proj",
    )(x, rot, *weights)


def _exp2_scaled(t, c):
    return jnp.exp2(t if c == 1.0 else t * c)


def _tile_update(s, vt, cs, m_s, l_s, acc_s, lag_s, c, one_pass):
    m_prev = m_s[:, cs]
    mx = jnp.max(s, axis=0, keepdims=True)
    m_new = jnp.maximum(m_prev, mx)
    alpha = _exp2_scaled(m_prev - m_new, c)
    if one_pass:
        p = _exp2_scaled(s - m_prev, c)
        if l_s is not None:
            l_s[:, cs] = (l_s[:, cs] + jnp.sum(p, axis=0, keepdims=True)) * alpha
        acc_s[:, cs] = (acc_s[:, cs] + _dot(vt, p.astype(BF16))) * alpha
        lag_s[...] = jnp.maximum(lag_s[...], (mx - m_prev) * c)
    else:
        p = _exp2_scaled(s - m_new, c)
        if l_s is not None:
            l_s[:, cs] = alpha * l_s[:, cs] + jnp.sum(p, axis=0, keepdims=True)
        acc_s[:, cs] = acc_s[:, cs] * alpha + _dot(vt, p.astype(BF16))
    m_s[:, cs] = m_new


def _attend(n_full, n_blk, cb, first_key_scores, diag_scores, diag_values, tile_scores, tile_values,
            m_s, l_s, acc_s, lag_s, c):
    def cols(i):
        return slice(i * cb, (i + 1) * cb)

    def run(one_pass):
        lag_s[...] = jnp.zeros(lag_s.shape, F32)
        for i in range(n_blk):
            m_s[:, cols(i)] = first_key_scores(i)
            acc_s[:, cols(i)] = jnp.zeros((acc_s.shape[0], cb), F32)
            if l_s is not None:
                l_s[:, cols(i)] = jnp.zeros((1, cb), F32)

        def update(units):
            s_next = units[0][0]()
            for u, (_, values, i) in enumerate(units):
                s = s_next
                if u + 1 < len(units):
                    s_next = units[u + 1][0]()
                _tile_update(s, values(), cols(i), m_s, l_s, acc_s, lag_s, c, one_pass)

        def full_tiles(js):
            return [(functools.partial(tile_scores, j, i), functools.partial(tile_values, j, i), i)
                    for j in js for i in range(n_blk)]

        update([(functools.partial(diag_scores, i), functools.partial(diag_values, i), i) for i in range(n_blk)])

        def pair(jj, carry):
            update(full_tiles((2 * jj, 2 * jj + 1)))
            return carry

        lax.fori_loop(0, n_full // 2, pair, 0)
        if isinstance(n_full, int):
            if n_full % 2:
                update(full_tiles((n_full - 1,)))
        else:
            @pl.when(n_full % 2 == 1)
            def _():
                update(full_tiles((n_full - 1,)))

    run(one_pass=True)

    @pl.when(jnp.max(lag_s[...]) > ATTN_LAG_LIMIT)
    def _():
        run(one_pass=False)


def _block_causal(tq, cb, col0):
    ki = lax.broadcasted_iota(jnp.int32, (tq, cb), 0) // CHUNK
    qi = ((lax.broadcasted_iota(jnp.int32, (tq, cb), 1) + col0) % tq) // CHUNK
    return ki <= qi


def _mla_attn_kernel(qt_ref, pckv_ref, pkpe_ref, kdiag_ref, vtdiag_ref, x_ref, wuvt_ref, wout_ref, y_ref,
                     o_s, m_s, l_s, acc_s, lag_s, *, tq, tk, cb, scale):
    nbs = qt_ref.shape[0]
    heads = o_s.shape[0]
    blk_per_seq = heads * tq // cb
    past_tiles = pckv_ref.shape[1] // tk

    def qt_blk(i):
        return qt_ref[i // blk_per_seq, 0, :, (i % blk_per_seq) * cb:(i % blk_per_seq + 1) * cb]

    def diag_scores(i):
        visible = _block_causal(tq, cb, (i % blk_per_seq) * cb)
        return jnp.where(visible, _dot(kdiag_ref[i // blk_per_seq], qt_blk(i)), -jnp.inf)

    def tile_scores(j, i):
        rows = pl.ds(pl.multiple_of(j * tk, tk), tk)
        latent = pckv_ref[i // blk_per_seq, rows, :].astype(BF16)
        rotary = pkpe_ref[i // blk_per_seq, rows, :].astype(BF16)
        qt = qt_blk(i)
        return _dot(latent, qt[:MLA_KV_LORA]) + _dot(rotary, qt[MLA_KV_LORA:MLA_KV_LORA + MLA_ROPE])

    def tile_values(j, i):
        rows = pl.ds(pl.multiple_of(j * tk, tk), tk)
        return pckv_ref[i // blk_per_seq, rows, :].T.astype(BF16)

    def first_key_scores(i):
        return _dot(kdiag_ref[i // blk_per_seq, :BF16_SUBLANES, :], qt_blk(i))[:1]

    _attend(past_tiles, nbs * blk_per_seq, cb, first_key_scores, diag_scores,
            lambda i: vtdiag_ref[i // blk_per_seq, 0], tile_scores, tile_values,
            m_s, l_s, acc_s, lag_s, scale * LOG2E)

    for s in range(nbs):
        for hd in range(heads):
            cs = slice((s * heads + hd) * tq, (s * heads + hd + 1) * tq)
            o_lat_t = (acc_s[:, cs] * (1.0 / l_s[:, cs])).astype(BF16)
            o_s[hd] = _dot(wuvt_ref[hd], o_lat_t).astype(BF16)
        o_t = o_s[...].reshape(o_s.shape[0] * o_s.shape[1], tq)
        y_ref[s] = x_ref[s] + _dot_tn(o_t, wout_ref[...])


def _mla_attn_call(qt, past_ckv, past_kpe, kdiag, vtdiag, x, wuvt, wout, *, tk, nbs, scale):
    bsz, tq, d = x.shape
    heads = wuvt.shape[0]
    past = past_ckv.shape[1]
    assert past % tk == 0 and bsz % nbs == 0 and qt.shape == (bsz, 1, KCAT, heads * tq)
    assert past_kpe.shape == (bsz, past, MLA_ROPE) and vtdiag.shape == (bsz, 1, MLA_KV_LORA, tq)
    cb = max(tq, min(heads * tq, ATTN_COL_BLOCK))
    assert cb % tq == 0 and (heads * tq) % cb == 0
    v_dim = wuvt.shape[1]
    return pl.pallas_call(
        functools.partial(_mla_attn_kernel, tq=tq, tk=tk, cb=cb, scale=scale),
        out_shape=jax.ShapeDtypeStruct((bsz, tq, d), F32),
        grid=(bsz // nbs,),
        in_specs=[
            pl.BlockSpec((nbs, 1, KCAT, heads * tq), lambda b: (b, 0, 0, 0)),
            pl.BlockSpec((nbs, past, MLA_KV_LORA), lambda b: (b, 0, 0)),
            pl.BlockSpec((nbs, past, MLA_ROPE), lambda b: (b, 0, 0)),
            pl.BlockSpec((nbs, tq, KCAT), lambda b: (b, 0, 0)),
            pl.BlockSpec((nbs, 1, MLA_KV_LORA, tq), lambda b: (b, 0, 0, 0)),
            pl.BlockSpec((nbs, tq, d), lambda b: (b, 0, 0)),
            _resident(wuvt.shape),
            _resident(wout.shape),
        ],
        out_specs=pl.BlockSpec((nbs, tq, d), lambda b: (b, 0, 0)),
        scratch_shapes=[
            pltpu.VMEM((heads, v_dim, tq), BF16),
            pltpu.VMEM((1, nbs * heads * tq), F32),
            pltpu.VMEM((1, nbs * heads * tq), F32),
            pltpu.VMEM((MLA_KV_LORA, nbs * heads * tq), F32),
            pltpu.VMEM((1, cb), F32),
        ],
        compiler_params=pltpu.CompilerParams(
            dimension_semantics=("parallel",), vmem_limit_bytes=VMEM_LIMIT),
        name="mla_attn",
    )(qt, past_ckv, past_kpe, kdiag, vtdiag, x, wuvt, wout)


def _mha_attn_kernel(qt_ref, k_ref, vt_ref, kdiag_ref, vtdiag_ref, ot_ref, m_s, acc_s, lag_s, *, tq):
    qb = pl.program_id(2)
    hp = qt_ref.shape[1]
    visible = _block_causal(tq, tq, 0)

    def diag_scores(i):
        return jnp.where(visible, _dot(kdiag_ref[0, i], qt_ref[0, i, 0]), -jnp.inf)

    def tile_scores(j, i):
        return _dot(k_ref[0, i, pl.ds(pl.multiple_of(j * tq, tq), tq), :], qt_ref[0, i, 0])

    def first_key_scores(i):
        return _dot(kdiag_ref[0, i, :BF16_SUBLANES, :], qt_ref[0, i, 0])[:1]

    _attend(qb, hp, tq, first_key_scores, diag_scores, lambda i: vtdiag_ref[0, i, 0],
            tile_scores, lambda j, i: vt_ref[0, i, j], m_s, None, acc_s, lag_s, 1.0)

    for i in range(hp):
        cs = slice(i * tq, (i + 1) * tq)
        inv_l = 1.0 / acc_s[MLA_V:MLA_V + 1, cs]
        ot_ref[0, i * MLA_V:(i + 1) * MLA_V, :] = (acc_s[:MLA_V, cs] * inv_l).astype(BF16)


def _mha_attn_call(qt, k, vt, *, hp, tq):
    bsz, heads, seq, qk_dim = k.shape
    nq = seq // tq
    v_rows = MLA_V + V_ONES_ROWS
    assert qt.shape == (bsz, heads, nq, qk_dim, tq)
    assert seq % tq == 0 and heads % hp == 0 and vt.shape == (bsz, heads, nq, v_rows, tq)
    return pl.pallas_call(
        functools.partial(_mha_attn_kernel, tq=tq),
        out_shape=jax.ShapeDtypeStruct((bsz, heads * MLA_V, seq), BF16),
        grid=(bsz, heads // hp, nq),
        in_specs=[
            pl.BlockSpec((1, hp, 1, qk_dim, tq), lambda b, g, i: (b, g, i, 0, 0)),
            pl.BlockSpec((1, hp, seq, qk_dim), lambda b, g, i: (b, g, 0, 0)),
            pl.BlockSpec((1, hp, nq, v_rows, tq), lambda b, g, i: (b, g, 0, 0, 0), pipeline_mode=pl.Buffered(1)),
            pl.BlockSpec((1, hp, tq, qk_dim), lambda b, g, i: (b, g, i, 0)),
            pl.BlockSpec((1, hp, 1, v_rows, tq), lambda b, g, i: (b, g, i, 0, 0)),
        ],
        out_specs=pl.BlockSpec((1, hp * MLA_V, tq), lambda b, g, i: (b, g, i)),
        scratch_shapes=[
            pltpu.VMEM((1, hp * tq), F32),
            pltpu.VMEM((v_rows, hp * tq), F32),
            pltpu.VMEM((1, tq), F32),
        ],
        compiler_params=pltpu.CompilerParams(
            dimension_semantics=("parallel", "parallel", "arbitrary"), vmem_limit_bytes=VMEM_LIMIT),
        name="mha_attn",
    )(qt, k, vt, k, vt)


def _rope_table(pos):
    half = MLA_ROPE // 2
    inv = ROPE_THETA ** (-jnp.arange(half, dtype=F32) / half)
    ang = pos.astype(F32)[:, None] * inv[None, :]
    cos, sin = jnp.cos(ang), jnp.sin(ang)
    return jnp.concatenate([cos, cos, sin, sin], axis=-1)


def _rotate_half_cols(w):
    half = w.shape[-1] // 2
    return jnp.concatenate([-w[..., half:], w[..., :half]], axis=-1)


def _pad_last(w, n):
    return jnp.concatenate([w, jnp.zeros(w.shape[:-1] + (n - w.shape[-1],), w.dtype)], axis=-1)


def _prep_gla(w_in, w_gk_up, b_gk, g_norm, w_out):
    qk = w_gk_up.shape[1]
    vw = w_out.shape[0]
    wq = w_in[:, :qk]
    wk = w_in[:, qk:2 * qk]
    wv = w_in[:, 2 * qk:2 * qk + vw]
    wg = w_in[:, 2 * qk + vw:2 * qk + 2 * vw]
    wlow = _pad_last(w_in[:, 2 * qk + 2 * vw:], LANES)
    rank = w_gk_up.shape[0]
    wgk = jnp.concatenate([w_gk_up, jnp.zeros((LANES - rank, qk), w_gk_up.dtype)], axis=0)
    return (wq.astype(BF16), wk.astype(BF16), wv.astype(BF16), wg.astype(BF16), wlow.astype(BF16),
            wgk.astype(BF16), b_gk[None, :], g_norm[None, :], w_out.astype(BF16))


def _prep_mla(w_down, q_norm, w_uq, kv_norm, w_uk, w_uv, w_out):
    wdq = w_down[:, :MLA_Q_LORA]
    wdkv = w_down[:, MLA_Q_LORA:MLA_Q_LORA + MLA_KV_LORA]
    wdk = w_down[:, MLA_Q_LORA + MLA_KV_LORA:]
    wdk = jnp.concatenate([wdk, _rotate_half_cols(wdk)], axis=-1)
    wuq = w_uq.reshape(MLA_Q_LORA, MLA_HEADS, MLA_NOPE + MLA_ROPE)
    wnope = wuq[..., :MLA_NOPE].reshape(MLA_Q_LORA, MLA_HEADS * MLA_NOPE)
    wrope = wuq[..., MLA_NOPE:]
    wr = jnp.concatenate([wrope, _rotate_half_cols(wrope)], axis=-1).reshape(MLA_Q_LORA, MLA_HEADS * LANES)
    wuk = jnp.transpose(w_uk, (1, 2, 0))
    wuvt = jnp.transpose(w_uv, (1, 2, 0))
    wuk_flat = w_uk.reshape(MLA_KV_LORA, MLA_HEADS * MLA_NOPE)
    wuv_flat_t = w_uv.reshape(MLA_KV_LORA, MLA_HEADS * MLA_V).T
    return dict(
        wdq=wdq.astype(BF16), wdkv=wdkv.astype(BF16), wdk=wdk.astype(BF16), qn=q_norm[None, :],
        kvn=kv_norm[None, :], wnope=wnope.astype(BF16), wr=wr.astype(BF16), wuk=wuk.astype(BF16),
        wuvt=wuvt.astype(BF16), wuk_flat=wuk_flat.astype(BF16), wuv_flat_t=wuv_flat_t.astype(BF16),
        wout=w_out.astype(BF16))


def kernel(x_prompt, x_sample, state_gla, cache_ckv, cache_kpe, norm_ffn1, w_ffn1_gate, w_ffn1_up, w_ffn1_down, norm_mix, norm_ffn2, w_ffn2_gate, w_ffn2_up, w_ffn2_down, w_gla_in, w_gla_gk_up, b_gla_gk, gla_out_norm, w_gla_out, w_mla_down, mla_q_norm, w_mla_uq, mla_kv_norm, w_mla_uk, w_mla_uv, w_mla_out, norm_final):
    depth = norm_ffn1.shape[0]
    n_mixers = 2
    d = x_prompt.shape[-1]
    mla_scale = (MLA_NOPE + MLA_ROPE) ** -0.5
    gf = norm_final[None, :]

    ffn1 = (norm_ffn1[:, None, :], w_ffn1_gate, w_ffn1_up, w_ffn1_down)
    ffn2 = (norm_ffn2[:, None, :], w_ffn2_gate, w_ffn2_up, w_ffn2_down)
    gla_w = [_prep_gla(w_gla_in[j], w_gla_gk_up[j], b_gla_gk[j], gla_out_norm[j], w_gla_out[j])
             for j in range(w_gla_in.shape[0])]
    mla_w = [_prep_mla(w_mla_down[j], mla_q_norm[j], w_mla_uq[j], mla_kv_norm[j], w_mla_uk[j], w_mla_uv[j],
                       w_mla_out[j]) for j in range(w_mla_down.shape[0])]

    def run(x, pos, gla_s0, past_ckv, past_kpe, *, gla_nb, gla_tl, attn_tq, proj_nb):
        bsz, seq, _ = x.shape
        rot = _rope_table(pos)
        new_gla, new_ckv, new_kpe = [], [], []
        pending = {}

        def ffn(x, w, layer, final_norm=False):
            return _ffn_call(x.reshape(bsz * seq, d), w[0], w[1], w[2], w[3], gf, layer,
                             final_norm=final_norm, **pending).reshape(bsz, seq, d)

        for i in range(depth):
            x = ffn(x, ffn1, i)
            j = i // n_mixers
            gm = norm_mix[i][None, :]
            if i % n_mixers == 0:
                x, s_fin = _gla_call(x, gla_s0[j], gm, *gla_w[j], nb=gla_nb, tl=gla_tl)
                new_gla.append(s_fin)
            else:
                w = mla_w[j]
                absorb = past_ckv is not None
                q, k, vt, ckv, kpe = _mla_proj_call(
                    x, rot, gm, w["wdq"], w["wdkv"], w["wdk"], w["qn"], w["kvn"],
                    w["wnope"], w["wr"], w["wuk"] if absorb else w["wuk_flat"], w["wuv_flat_t"],
                    tm=attn_tq, nbp=proj_nb, absorb=absorb, q_scale=1.0 if absorb else mla_scale * LOG2E)
                if absorb:
                    x = _mla_attn_call(q, past_ckv[j], past_kpe[j], k, vt, x, w["wuvt"], w["wout"],
                                       tk=ATTN_KEY_TILE, nbs=ATTN_SEQ_GROUP, scale=mla_scale)
                else:
                    pending = dict(attn_t=_mha_attn_call(q, k, vt, hp=ATTN_HEAD_GROUP, tq=attn_tq), w_out=w["wout"])
                new_ckv.append(ckv)
                new_kpe.append(kpe)
            x = ffn(x, ffn2, i, final_norm=(i == depth - 1))
            pending = {}
        return x, jnp.stack(new_gla), jnp.stack(new_ckv), jnp.stack(new_kpe)

    bp, lp = x_prompt.shape[0], x_prompt.shape[1]
    n_gla = w_gla_in.shape[0]
    s0_p = jnp.zeros((n_gla, bp) + state_gla.shape[2:], x_prompt.dtype)
    y_p, gla_p, ckv_p, kpe_p = run(
        x_prompt, jnp.arange(lp), s0_p, None, None, gla_nb=1, gla_tl=512, attn_tq=ATTN_KEY_TILE, proj_nb=1)

    past_len = cache_ckv.shape[2]
    ls = x_sample.shape[1]
    y_s, gla_s, ckv_s, kpe_s = run(
        x_sample, past_len + jnp.arange(ls), state_gla, cache_ckv, cache_kpe, gla_nb=8, gla_tl=ls, attn_tq=ls,
        proj_nb=8)

    return (y_p, y_s, gla_p, ckv_p, kpe_p, gla_s, ckv_s, kpe_s)
```

```python
import functools

import jax
import jax.numpy as jnp
from jax import lax
from jax.experimental import pallas as pl
from jax.experimental.pallas import tpu as pltpu

F32 = jnp.float32
BF16 = jnp.bfloat16

EPS = 1e-6
CHUNK = 64
GLA_HEADS = 4
GLA_GATE_NORM = 16.0
GLA_ROW_GROUPS = 2
MLA_HEADS = 8
MLA_NOPE = 128
MLA_ROPE = 64
MLA_V = 128
MLA_Q_LORA = 384
MLA_KV_LORA = 256
ROPE_THETA = 10000.0
LANES = 128
BF16_SUBLANES = 16
F32_SUBLANES = 8
MXU_COLS = 256
LOG2E = 1.4426950408889634
ATTN_COL_BLOCK = 512
ATTN_KEY_TILE = 512
ATTN_HEAD_GROUP = 4
ATTN_SEQ_GROUP = 2
V_ONES_ROWS = 16
ATTN_LAG_LIMIT = 64.0
KCAT = MLA_KV_LORA + LANES

VMEM_LIMIT = 56 * 1024 * 1024


def _rms(xf, g):
    return xf * lax.rsqrt(jnp.mean(xf * xf, axis=-1, keepdims=True) + EPS) * g


def _silu(a):
    return a * (1.0 / (1.0 + jnp.exp(-a)))


def _dot(a, b):
    return jnp.dot(a, b, preferred_element_type=F32)


def _dot_nt(a, b):
    return lax.dot_general(a, b, (((1,), (1,)), ((), ())), preferred_element_type=F32)


def _dot_tn(a, b):
    return lax.dot_general(a, b, (((0,), (0,)), ((), ())), preferred_element_type=F32)


def _resident(shape):
    zeros = (0,) * len(shape)
    return pl.BlockSpec(shape, lambda *_: zeros, pipeline_mode=pl.Buffered(1))


def _ffn_kernel(*refs, f_chunk, final_norm, mixer_proj):
    if mixer_proj:
        x_ref, ot_ref, wo_ref, g_ref, wg_ref, wu_ref, wd_ref, gf_ref, o_ref, hid_ref = refs
        x = x_ref[...] + _dot_tn(ot_ref[0], wo_ref[...])
    else:
        x_ref, g_ref, wg_ref, wu_ref, wd_ref, gf_ref, o_ref, hid_ref = refs
        x = x_ref[...]
    h = _rms(x, g_ref[0]).astype(BF16)
    d_ff = wg_ref.shape[2]
    for c in range(d_ff // f_chunk):
        sl = slice(c * f_chunk, (c + 1) * f_chunk)
        a = _dot(h, wg_ref[0, :, sl].astype(BF16))
        b = _dot(h, wu_ref[0, :, sl].astype(BF16))
        hid_ref[:, sl] = (_silu(a) * b).astype(BF16)
    y = x + 0.5 * _dot(hid_ref[...], wd_ref[0].astype(BF16))
    if final_norm:
        y = _rms(y, gf_ref[...])
    o_ref[...] = y


def _layer_resident(shape, layer):
    index = (layer,) + (0,) * (len(shape) - 1)
    return pl.BlockSpec((1,) + tuple(shape[1:]), lambda *_: index, pipeline_mode=pl.Buffered(1))


def _ffn_call(x, g, wg, wu, wd, gf, layer, *, final_norm, attn_t=None, w_out=None, tm=512, f_chunk=256):
    t, d = x.shape
    d_ff = wg.shape[2]
    assert t % tm == 0 and d_ff % f_chunk == 0
    mixer_proj = attn_t is not None
    x_spec = pl.BlockSpec((tm, d), lambda i: (i, 0))
    weights = (g, wg, wu, wd)
    if mixer_proj:
        tiles_per_seq = attn_t.shape[2] // tm
        assert attn_t.shape[0] * attn_t.shape[2] == t and attn_t.shape[2] % tm == 0
        ot_spec = pl.BlockSpec((1, attn_t.shape[1], tm), lambda i: (i // tiles_per_seq, 0, i % tiles_per_seq))
        operands = (x, attn_t, w_out) + weights + (gf,)
        in_specs = [x_spec, ot_spec, _resident(w_out.shape)]
    else:
        operands = (x,) + weights + (gf,)
        in_specs = [x_spec]
    return pl.pallas_call(
        functools.partial(_ffn_kernel, f_chunk=f_chunk, final_norm=final_norm, mixer_proj=mixer_proj),
        out_shape=jax.ShapeDtypeStruct((t, d), F32),
        grid=(t // tm,),
        in_specs=in_specs + [_layer_resident(w.shape, layer) for w in weights] + [_resident(gf.shape)],
        out_specs=pl.BlockSpec((tm, d), lambda i: (i, 0)),
        scratch_shapes=[pltpu.VMEM((tm, d_ff), BF16)],
        compiler_params=pltpu.CompilerParams(
            dimension_semantics=("parallel",), vmem_limit_bytes=VMEM_LIMIT),
        name="ffn",
    )(*operands)


def _split3(a):
    hi = a.astype(BF16)
    r = a - hi.astype(F32)
    mid = r.astype(BF16)
    lo = (r - mid.astype(F32)).astype(BF16)
    return hi, mid, lo


def _interleave(primary, filler):
    for n, step in enumerate(primary):
        step()
        if n < len(filler):
            filler[n]()
    for extra in filler[len(primary):]:
        extra()


def _gla_kernel(x_ref, s0_ref, gmix_ref, win_ref, wlow_ref, wgk_ref, bgk_ref,
                gout_ref, wout_ref, y_ref, s_ref, st_s, *, nb, tl, n_sub):
    l = pl.program_id(1)
    heads = GLA_HEADS
    dk = wgk_ref.shape[1] // heads
    dv = wout_ref.shape[1] // heads
    rows_sub = nb * tl // n_sub
    n_chunk = rows_sub // CHUNK
    d = x_ref.shape[-1]

    @pl.when(l == 0)
    def _():
        st_s[...] = s0_ref[...]

    ti = lax.broadcasted_iota(jnp.int32, (CHUNK, CHUNK), 0)
    si = lax.broadcasted_iota(jnp.int32, (CHUNK, CHUNK), 1)
    causal = si <= ti
    tri = jnp.where(causal, 1.0, 0.0).astype(BF16)

    def rows_of(ref, sub):
        row0 = sub * rows_sub
        if rows_sub >= tl:
            return ref.at[row0 // tl:(row0 + rows_sub) // tl]
        return ref.at[row0 // tl, row0 % tl:row0 % tl + rows_sub, :]

    def in_projection(sub):
        env = {}

        def start():
            env["x"] = rows_of(x_ref, sub)[...].reshape(rows_sub, d)
            env["h"] = _rms(env["x"], gmix_ref[...]).astype(BF16)

        def piece(name, lo):
            return lambda: env.setdefault(name, []).append(
                _dot(env["h"], win_ref[0, :, lo:lo + MXU_COLS].astype(BF16)))

        thunks = [start, lambda: env.setdefault("low", []).append(_dot(env["h"], wlow_ref[...]))]
        col = 0
        for name, width in (("q", heads * dk), ("k", heads * dk), ("v", heads * dv), ("g", heads * dv)):
            thunks += [piece(name, col + lo) for lo in range(0, width, MXU_COLS)]
            col += width
        return thunks, env

    def recurrence(sub, env):
        row0 = sub * rows_sub
        cat = lambda name: jnp.concatenate(env[name], axis=-1)
        loc = {}

        def start():
            loc["q"] = cat("q") * (dk ** -0.5)
            loc["k"] = cat("k")
            loc["v"] = cat("v").astype(BF16)
            z = _dot(cat("low").astype(BF16), wgk_ref[...]) + bgk_ref[...]
            loc["log_a"] = -(jnp.maximum(-z, 0.0) + jnp.log1p(jnp.exp(-jnp.abs(z)))) / GLA_GATE_NORM
            loc["state"] = {}
            loc["o"] = [[None] * n_chunk for _ in range(heads)]

        def cumsum(c):
            rs = slice(c * CHUNK, (c + 1) * CHUNK)
            a_hi, a_mid, a_lo = _split3(loc["log_a"][rs, :])
            loc["cum", c] = _dot(tri, a_hi) + _dot(tri, a_mid) + _dot(tri, a_lo)

        def scores(c):
            rs = slice(c * CHUNK, (c + 1) * CHUNK)
            cum = loc["cum", c]
            cum_last = cum[CHUNK - 1:CHUNK, :]
            q_dec = (loc["q"][rs, :] * jnp.exp(cum)).astype(BF16)
            k_inv = (loc["k"][rs, :] * jnp.exp(-cum)).astype(BF16)
            loc["k_tail", c] = (loc["k"][rs, :] * jnp.exp(cum_last - cum)).astype(BF16)
            loc["q_dec", c] = q_dec
            dec = jnp.broadcast_to(jnp.exp(cum_last), (F32_SUBLANES, heads * dk))
            loc["dec", c] = [dec[:, hd * dk:(hd + 1) * dk].T[:, :1] for hd in range(heads)]
            loc["scores", c] = [
                jnp.where(causal, _dot_nt(q_dec[:, hd * dk:(hd + 1) * dk], k_inv[:, hd * dk:(hd + 1) * dk]), 0.0)
                .astype(BF16) for hd in range(heads)]

        def values(c):
            rs = slice(c * CHUNK, (c + 1) * CHUNK)
            vs = [loc["v"][rs, hd * dv:(hd + 1) * dv] for hd in range(heads)]
            loc["o_intra", c] = [_dot(loc["scores", c][hd], vs[hd]) for hd in range(heads)]
            loc["kv", c] = [_dot_tn(loc["k_tail", c][:, hd * dk:(hd + 1) * dk], vs[hd]) for hd in range(heads)]

        def carry(c):
            b = (row0 + c * CHUNK) // tl
            for hd in range(heads):
                ks = slice(hd * dk, (hd + 1) * dk)
                st = loc["state"].get((b, hd))
                if st is None:
                    st = st_s[b, hd]
                loc["o"][hd][c] = loc["o_intra", c][hd] + _dot(loc["q_dec", c][:, ks], st.astype(BF16))
                loc["state"][(b, hd)] = st * loc["dec", c][hd] + loc["kv", c][hd]

        def finish():
            for (b, hd), st in loc["state"].items():
                st_s[b, hd] = st
            env["o"] = [jnp.concatenate(loc["o"][hd], axis=0) for hd in range(heads)]

        stages = (cumsum, scores, values, carry)
        thunks = [start]
        for step in range(n_chunk + len(stages) - 1):
            for st in reversed(range(len(stages))):
                if 0 <= step - st < n_chunk:
                    thunks.append(functools.partial(stages[st], step - st))
        return thunks + [finish]

    def out_projection(sub, env):
        loc = {}

        def gate():
            g = jnp.concatenate(env["g"], axis=-1)
            loc["gated"] = jnp.concatenate(
                [(_rms(env["o"][hd], gout_ref[...]) * _silu(g[:, hd * dv:(hd + 1) * dv])).astype(BF16)
                 for hd in range(heads)], axis=-1)

        def piece(lo):
            return lambda: loc.setdefault("y", []).append(
                _dot(loc["gated"], wout_ref[0, :, lo:lo + MXU_COLS].astype(BF16)))

        def store():
            y = env["x"] + jnp.concatenate(loc["y"], axis=-1)
            dst = rows_of(y_ref, sub)
            dst[...] = y.reshape(dst.shape)

        return [gate] + [piece(lo) for lo in range(0, d, MXU_COLS)] + [store]

    proj = [in_projection(sub) for sub in range(n_sub)]
    for step in proj[0][0]:
        step()
    for sub in range(n_sub):
        filler = []
        if sub + 1 < n_sub:
            filler += proj[sub + 1][0]
        if sub >= 1:
            filler += out_projection(sub - 1, proj[sub - 1][1])
        _interleave(recurrence(sub, proj[sub][1]), filler)
    for step in out_projection(n_sub - 1, proj[n_sub - 1][1]):
        step()

    @pl.when(l == pl.num_programs(1) - 1)
    def _():
        s_ref[...] = st_s[...]


def _gla_call(x, s0, gmix, win, wlow, wgk, bgk, gout, wout, layer, *, nb, tl, n_sub=GLA_ROW_GROUPS):
    bsz, seq, d = x.shape
    heads = GLA_HEADS
    dk = wgk.shape[1] // heads
    dv = wout.shape[1] // heads
    rows_sub = nb * tl // n_sub
    assert bsz % nb == 0 and seq % tl == 0 and tl % CHUNK == 0 and (nb * tl) % n_sub == 0
    assert rows_sub % CHUNK == 0 and (rows_sub % tl == 0 or tl % rows_sub == 0)
    assert (heads * dk) % MXU_COLS == 0 and (heads * dv) % MXU_COLS == 0
    small = (wlow, wgk, bgk, gout)
    return pl.pallas_call(
        functools.partial(_gla_kernel, nb=nb, tl=tl, n_sub=n_sub),
        out_shape=(jax.ShapeDtypeStruct((bsz, seq, d), F32),
                   jax.ShapeDtypeStruct((bsz, heads, dk, dv), F32)),
        grid=(bsz // nb, seq // tl),
        in_specs=[
            pl.BlockSpec((nb, tl, d), lambda b, l: (b, l, 0)),
            pl.BlockSpec((nb, heads, dk, dv), lambda b, l: (b, 0, 0, 0)),
            _resident(gmix.shape), _layer_resident(win.shape, layer),
        ] + [_resident(w.shape) for w in small] + [_layer_resident(wout.shape, layer)],
        out_specs=(pl.BlockSpec((nb, tl, d), lambda b, l: (b, l, 0)),
                   pl.BlockSpec((nb, heads, dk, dv), lambda b, l: (b, 0, 0, 0))),
        scratch_shapes=[pltpu.VMEM((nb, heads, dk, dv), F32)],
        compiler_params=pltpu.CompilerParams(
            dimension_semantics=("parallel", "arbitrary"), vmem_limit_bytes=VMEM_LIMIT),
        name="gla",
    )(x, s0, gmix, win, *small, wout)


def _mla_proj_kernel(x_ref, rot_ref, gmix_ref, wdq_ref, wdkv_ref, wdk_ref, qn_ref, kvn_ref,
                     wnope_ref, wr_ref, wkey_ref, wval_ref,
                     q_ref, k_ref, vt_ref, ckv_ref, kpe_ref, *, absorb, q_scale):
    nbp, tm, d = x_ref.shape
    rows = nbp * tm
    x = x_ref[...].reshape(rows, d)
    h = _rms(x, gmix_ref[...]).astype(BF16)
    rot = jnp.concatenate([rot_ref[...]] * nbp, axis=0)

    def rope(pair):
        prod = pair * rot
        return prod + pltpu.roll(prod, MLA_ROPE, axis=1)

    ckv = _rms(_dot(h, wdkv_ref[...]), kvn_ref[...])
    rope_lanes = lax.broadcasted_iota(jnp.int32, (rows, LANES), 1) < MLA_ROPE
    kpe = jnp.where(rope_lanes, rope(_dot(h, wdk_ref[...])), 0.0)
    ckv_ref[...] = ckv.reshape(nbp, tm, MLA_KV_LORA)
    kpe_ref[...] = kpe[:, :MLA_ROPE].reshape(nbp, tm, MLA_ROPE)
    ckv_b = ckv.astype(BF16)
    kpe_b = kpe.astype(BF16)
    ckv_t = ckv.T.astype(BF16)

    hq = _rms(_dot(h, wdq_ref[...]), qn_ref[...]).astype(BF16)
    q_nope = _dot(hq, wnope_ref[...])
    q_r = _dot(hq, wr_ref[...])

    if absorb:
        k_ref[:, :, :MLA_KV_LORA] = ckv_b.reshape(nbp, tm, MLA_KV_LORA)
        k_ref[:, :, MLA_KV_LORA:] = kpe_b.reshape(nbp, tm, LANES)
        for b in range(nbp):
            vt_ref[b, 0] = ckv_t[:, b * tm:(b + 1) * tm]
    else:
        k_nope = _dot(ckv_b, wkey_ref[...]).astype(BF16)
        v_t = _dot(wval_ref[...], ckv_t).astype(BF16)
        ones = jnp.ones((V_ONES_ROWS, tm), BF16)
    for hd in range(MLA_HEADS):
        hs = slice(hd * LANES, (hd + 1) * LANES)
        q_pe = rope(q_r[:, hs])
        if absorb:
            q_lat = _dot(q_nope[:, hs].astype(BF16), wkey_ref[hd])
            q_t = jnp.concatenate([q_lat, q_pe], axis=-1).T.astype(BF16)
            for b in range(nbp):
                q_ref[b, 0, :, hd * tm:(hd + 1) * tm] = q_t[:, b * tm:(b + 1) * tm]
        else:
            for b in range(nbp):
                bs = slice(b * tm, (b + 1) * tm)
                q_ref[b, hd, 0, :MLA_NOPE] = (q_nope[bs, hs] * q_scale).T.astype(BF16)
                q_ref[b, hd, 0, MLA_NOPE:] = (q_pe[bs] * q_scale).T.astype(BF16)
                k_ref[b, hd, :, :MLA_NOPE] = k_nope[bs, hs]
                k_ref[b, hd, :, MLA_NOPE:] = kpe_b[bs]
                vt_ref[b, hd, 0, :MLA_V] = v_t[hd * MLA_V:(hd + 1) * MLA_V, bs]
                vt_ref[b, hd, 0, MLA_V:] = ones


def _mla_proj_call(x, rot, gmix, wdq, wdkv, wdk, qn, kvn, wnope, wr, wkey, wval, *, tm, nbp, absorb, q_scale=1.0):
    bsz, seq, d = x.shape
    assert seq % tm == 0 and bsz % nbp == 0
    weights = (gmix, wdq, wdkv, wdk, qn, kvn, wnope, wr, wkey, wval)
    nt = seq // tm
    if absorb:
        q_k_vt = (jax.ShapeDtypeStruct((bsz, nt, KCAT, MLA_HEADS * tm), BF16),
                  jax.ShapeDtypeStruct((bsz, seq, KCAT), BF16),
                  jax.ShapeDtypeStruct((bsz, nt, MLA_KV_LORA, tm), BF16))
        q_k_vt_specs = (pl.BlockSpec((nbp, 1, KCAT, MLA_HEADS * tm), lambda b, l: (b, l, 0, 0)),
                        pl.BlockSpec((nbp, tm, KCAT), lambda b, l: (b, l, 0)),
                        pl.BlockSpec((nbp, 1, MLA_KV_LORA, tm), lambda b, l: (b, l, 0, 0)))
    else:
        qk_dim = MLA_NOPE + LANES
        v_rows = MLA_V + V_ONES_ROWS
        q_k_vt = (jax.ShapeDtypeStruct((bsz, MLA_HEADS, nt, qk_dim, tm), BF16),
                  jax.ShapeDtypeStruct((bsz, MLA_HEADS, seq, qk_dim), BF16),
                  jax.ShapeDtypeStruct((bsz, MLA_HEADS, nt, v_rows, tm), BF16))
        q_k_vt_specs = (pl.BlockSpec((nbp, MLA_HEADS, 1, qk_dim, tm), lambda b, l: (b, 0, l, 0, 0)),
                        pl.BlockSpec((nbp, MLA_HEADS, tm, qk_dim), lambda b, l: (b, 0, l, 0)),
                        pl.BlockSpec((nbp, MLA_HEADS, 1, v_rows, tm), lambda b, l: (b, 0, l, 0, 0)))
    return pl.pallas_call(
        functools.partial(_mla_proj_kernel, absorb=absorb, q_scale=q_scale),
        out_shape=q_k_vt + (jax.ShapeDtypeStruct((bsz, seq, MLA_KV_LORA), F32),
                            jax.ShapeDtypeStruct((bsz, seq, MLA_ROPE), F32)),
        grid=(bsz // nbp, nt),
        in_specs=[
            pl.BlockSpec((nbp, tm, d), lambda b, l: (b, l, 0)),
            pl.BlockSpec((tm, LANES), lambda b, l: (l, 0)),
        ] + [_resident(w.shape) for w in weights],
        out_specs=q_k_vt_specs + (pl.BlockSpec((nbp, tm, MLA_KV_LORA), lambda b, l: (b, l, 0)),
                                  pl.BlockSpec((nbp, tm, MLA_ROPE), lambda b, l: (b, l, 0))),
        compiler_params=pltpu.CompilerParams(
            dimension_semantics=("parallel", "parallel"), vmem_limit_bytes=VMEM_LIMIT),
        name="mla_proj",
    )(x, rot, *weights)


def _exp2_scaled(t, c):
    return jnp.exp2(t if c == 1.0 else t * c)


def _tile_update(s, vt, cs, m_s, l_s, acc_s, lag_s, c, one_pass):
    m_prev = m_s[:, cs]
    mx = jnp.max(s, axis=0, keepdims=True)
    m_new = jnp.maximum(m_prev, mx)
    alpha = _exp2_scaled(m_prev - m_new, c)
    if one_pass:
        p = _exp2_scaled(s - m_prev, c)
        if l_s is not None:
            l_s[:, cs] = (l_s[:, cs] + jnp.sum(p, axis=0, keepdims=True)) * alpha
        acc_s[:, cs] = (acc_s[:, cs] + _dot(vt, p.astype(BF16))) * alpha
        lag_s[...] = jnp.maximum(lag_s[...], (mx - m_prev) * c)
    else:
        p = _exp2_scaled(s - m_new, c)
        if l_s is not None:
            l_s[:, cs] = alpha * l_s[:, cs] + jnp.sum(p, axis=0, keepdims=True)
        acc_s[:, cs] = acc_s[:, cs] * alpha + _dot(vt, p.astype(BF16))
    m_s[:, cs] = m_new


def _attend(n_full, n_blk, cb, first_key_scores, diag_scores, diag_values, tile_scores, tile_values,
            m_s, l_s, acc_s, lag_s, c):
    def cols(i):
        return slice(i * cb, (i + 1) * cb)

    def run(one_pass):
        lag_s[...] = jnp.zeros(lag_s.shape, F32)
        for i in range(n_blk):
            m_s[:, cols(i)] = first_key_scores(i)
            acc_s[:, cols(i)] = jnp.zeros((acc_s.shape[0], cb), F32)
            if l_s is not None:
                l_s[:, cols(i)] = jnp.zeros((1, cb), F32)

        def update(units):
            s_next = units[0][0]()
            for u, (_, values, i) in enumerate(units):
                s = s_next
                if u + 1 < len(units):
                    s_next = units[u + 1][0]()
                _tile_update(s, values(), cols(i), m_s, l_s, acc_s, lag_s, c, one_pass)

        def full_tiles(js):
            return [(functools.partial(tile_scores, j, i), functools.partial(tile_values, j, i), i)
                    for j in js for i in range(n_blk)]

        update([(functools.partial(diag_scores, i), functools.partial(diag_values, i), i) for i in range(n_blk)])

        def pair(jj, carry):
            update(full_tiles((2 * jj, 2 * jj + 1)))
            return carry

        if isinstance(n_full, int):
            for jj in range(n_full // 2):
                pair(jj, 0)
            if n_full % 2:
                update(full_tiles((n_full - 1,)))
        else:
            lax.fori_loop(0, n_full // 2, pair, 0)

            @pl.when(n_full % 2 == 1)
            def _():
                update(full_tiles((n_full - 1,)))

    run(one_pass=True)

    @pl.when(jnp.max(lag_s[...]) > ATTN_LAG_LIMIT)
    def _():
        run(one_pass=False)


def _block_causal(tq, cb, col0):
    ki = lax.broadcasted_iota(jnp.int32, (tq, cb), 0) // CHUNK
    qi = ((lax.broadcasted_iota(jnp.int32, (tq, cb), 1) + col0) % tq) // CHUNK
    return ki <= qi


def _mla_attn_kernel(qt_ref, pckv_ref, pkpet_ref, kdiag_ref, vtdiag_ref, x_ref, wuvt_ref, wout_ref, y_ref,
                     o_s, m_s, l_s, acc_s, lag_s, *, tq, tk, cb, scale):
    nbs = qt_ref.shape[0]
    heads = o_s.shape[0]
    blk_per_seq = heads * tq // cb
    past_tiles = pckv_ref.shape[1] // tk

    def qt_blk(i):
        return qt_ref[i // blk_per_seq, 0, :, (i % blk_per_seq) * cb:(i % blk_per_seq + 1) * cb]

    def diag_scores(i):
        visible = _block_causal(tq, cb, (i % blk_per_seq) * cb)
        return jnp.where(visible, _dot(kdiag_ref[i // blk_per_seq], qt_blk(i)), -jnp.inf)

    def tile_scores(j, i):
        rows = slice(j * tk, (j + 1) * tk)
        latent = pckv_ref[i // blk_per_seq, rows, :].astype(BF16)
        rotary_t = pkpet_ref[i // blk_per_seq, :, rows].astype(BF16)
        qt = qt_blk(i)
        return _dot(latent, qt[:MLA_KV_LORA]) + _dot_tn(rotary_t, qt[MLA_KV_LORA:MLA_KV_LORA + MLA_ROPE])

    def tile_values(j, i):
        return pckv_ref[i // blk_per_seq, j * tk:(j + 1) * tk, :].T.astype(BF16)

    def first_key_scores(i):
        return _dot(kdiag_ref[i // blk_per_seq, :BF16_SUBLANES, :], qt_blk(i))[:1]

    _attend(past_tiles, nbs * blk_per_seq, cb, first_key_scores, diag_scores,
            lambda i: vtdiag_ref[i // blk_per_seq, 0], tile_scores, tile_values,
            m_s, l_s, acc_s, lag_s, scale * LOG2E)

    for s in range(nbs):
        for hd in range(heads):
            cs = slice((s * heads + hd) * tq, (s * heads + hd + 1) * tq)
            o_lat_t = (acc_s[:, cs] * (1.0 / l_s[:, cs])).astype(BF16)
            o_s[hd] = _dot(wuvt_ref[hd], o_lat_t).astype(BF16)
        o_t = o_s[...].reshape(o_s.shape[0] * o_s.shape[1], tq)
        y_ref[s] = x_ref[s] + _dot_tn(o_t, wout_ref[...])


def _mla_attn_call(qt, past_ckv, past_kpe_t, kdiag, vtdiag, x, wuvt, wout, *, tk, nbs, scale):
    bsz, tq, d = x.shape
    heads = wuvt.shape[0]
    past = past_ckv.shape[1]
    assert past % tk == 0 and bsz % nbs == 0 and qt.shape == (bsz, 1, KCAT, heads * tq)
    assert past_kpe_t.shape == (bsz, MLA_ROPE, past) and vtdiag.shape == (bsz, 1, MLA_KV_LORA, tq)
    cb = max(tq, min(heads * tq, ATTN_COL_BLOCK))
    assert cb % tq == 0 and (heads * tq) % cb == 0
    v_dim = wuvt.shape[1]
    return pl.pallas_call(
        functools.partial(_mla_attn_kernel, tq=tq, tk=tk, cb=cb, scale=scale),
        out_shape=jax.ShapeDtypeStruct((bsz, tq, d), F32),
        grid=(bsz // nbs,),
        in_specs=[
            pl.BlockSpec((nbs, 1, KCAT, heads * tq), lambda b: (b, 0, 0, 0)),
            pl.BlockSpec((nbs, past, MLA_KV_LORA), lambda b: (b, 0, 0)),
            pl.BlockSpec((nbs, MLA_ROPE, past), lambda b: (b, 0, 0)),
            pl.BlockSpec((nbs, tq, KCAT), lambda b: (b, 0, 0)),
            pl.BlockSpec((nbs, 1, MLA_KV_LORA, tq), lambda b: (b, 0, 0, 0)),
            pl.BlockSpec((nbs, tq, d), lambda b: (b, 0, 0)),
            _resident(wuvt.shape),
            _resident(wout.shape),
        ],
        out_specs=pl.BlockSpec((nbs, tq, d), lambda b: (b, 0, 0)),
        scratch_shapes=[
            pltpu.VMEM((heads, v_dim, tq), BF16),
            pltpu.VMEM((1, nbs * heads * tq), F32),
            pltpu.VMEM((1, nbs * heads * tq), F32),
            pltpu.VMEM((MLA_KV_LORA, nbs * heads * tq), F32),
            pltpu.VMEM((1, cb), F32),
        ],
        compiler_params=pltpu.CompilerParams(
            dimension_semantics=("parallel",), vmem_limit_bytes=VMEM_LIMIT),
        name="mla_attn",
    )(qt, past_ckv, past_kpe_t, kdiag, vtdiag, x, wuvt, wout)


def _mha_attn_kernel(qt_ref, k_ref, vt_ref, kdiag_ref, vtdiag_ref, ot_ref, m_s, acc_s, lag_s, *, tq):
    qb = pl.program_id(2)
    hp = qt_ref.shape[1]
    visible = _block_causal(tq, tq, 0)

    def diag_scores(i):
        return jnp.where(visible, _dot(kdiag_ref[0, i], qt_ref[0, i, 0]), -jnp.inf)

    def tile_scores(j, i):
        return _dot(k_ref[0, i, pl.ds(pl.multiple_of(j * tq, tq), tq), :], qt_ref[0, i, 0])

    def first_key_scores(i):
        return _dot(kdiag_ref[0, i, :BF16_SUBLANES, :], qt_ref[0, i, 0])[:1]

    _attend(qb, hp, tq, first_key_scores, diag_scores, lambda i: vtdiag_ref[0, i, 0],
            tile_scores, lambda j, i: vt_ref[0, i, j], m_s, None, acc_s, lag_s, 1.0)

    for i in range(hp):
        cs = slice(i * tq, (i + 1) * tq)
        inv_l = 1.0 / acc_s[MLA_V:MLA_V + 1, cs]
        ot_ref[0, i * MLA_V:(i + 1) * MLA_V, :] = (acc_s[:MLA_V, cs] * inv_l).astype(BF16)


def _mha_attn_call(qt, k, vt, *, hp, tq):
    bsz, heads, seq, qk_dim = k.shape
    nq = seq // tq
    v_rows = MLA_V + V_ONES_ROWS
    assert qt.shape == (bsz, heads, nq, qk_dim, tq)
    assert seq % tq == 0 and heads % hp == 0 and vt.shape == (bsz, heads, nq, v_rows, tq)
    return pl.pallas_call(
        functools.partial(_mha_attn_kernel, tq=tq),
        out_shape=jax.ShapeDtypeStruct((bsz, heads * MLA_V, seq), BF16),
        grid=(bsz, heads // hp, nq),
        in_specs=[
            pl.BlockSpec((1, hp, 1, qk_dim, tq), lambda b, g, i: (b, g, i, 0, 0)),
            pl.BlockSpec((1, hp, seq, qk_dim), lambda b, g, i: (b, g, 0, 0)),
            pl.BlockSpec((1, hp, nq, v_rows, tq), lambda b, g, i: (b, g, 0, 0, 0), pipeline_mode=pl.Buffered(1)),
            pl.BlockSpec((1, hp, tq, qk_dim), lambda b, g, i: (b, g, i, 0)),
            pl.BlockSpec((1, hp, 1, v_rows, tq), lambda b, g, i: (b, g, i, 0, 0)),
        ],
        out_specs=pl.BlockSpec((1, hp * MLA_V, tq), lambda b, g, i: (b, g, i)),
        scratch_shapes=[
            pltpu.VMEM((1, hp * tq), F32),
            pltpu.VMEM((v_rows, hp * tq), F32),
            pltpu.VMEM((1, tq), F32),
        ],
        compiler_params=pltpu.CompilerParams(
            dimension_semantics=("parallel", "parallel", "arbitrary"), vmem_limit_bytes=VMEM_LIMIT),
        name="mha_attn",
    )(qt, k, vt, k, vt)


def _rope_table(pos0, n):
    half = MLA_ROPE // 2
    inv = ROPE_THETA ** (-jnp.arange(half, dtype=F32) / half)

    def cos_sin(p):
        ang = p.astype(F32)[:, None] * inv[None, :]
        return jnp.cos(ang), jnp.sin(ang)

    blk = ATTN_KEY_TILE
    if n > blk and n % blk == 0:
        ch, sh = cos_sin(pos0 + blk * jnp.arange(n // blk))
        cl, sl = cos_sin(jnp.arange(blk))
        cos = (ch[:, None, :] * cl[None] - sh[:, None, :] * sl[None]).reshape(n, half)
        sin = (sh[:, None, :] * cl[None] + ch[:, None, :] * sl[None]).reshape(n, half)
    else:
        cos, sin = cos_sin(pos0 + jnp.arange(n))
    return jnp.concatenate([cos, cos, sin, sin], axis=-1)


def _rotate_half_cols(w):
    half = w.shape[-1] // 2
    return jnp.concatenate([-w[..., half:], w[..., :half]], axis=-1)


def _pad_last(w, n):
    return jnp.concatenate([w, jnp.zeros(w.shape[:-1] + (n - w.shape[-1],), w.dtype)], axis=-1)


def _prep_gla(w_in, w_gk_up, b_gk, g_norm):
    qk = w_gk_up.shape[1]
    rank = w_gk_up.shape[0]
    wlow = _pad_last(w_in[:, w_in.shape[1] - rank:], LANES)
    wgk = jnp.concatenate([w_gk_up, jnp.zeros((LANES - rank, qk), w_gk_up.dtype)], axis=0)
    return wlow.astype(BF16), wgk.astype(BF16), b_gk[None, :], g_norm[None, :]


def _prep_mla(w_down, q_norm, w_uq, kv_norm, w_uk, w_uv, w_out):
    wdq = w_down[:, :MLA_Q_LORA]
    wdkv = w_down[:, MLA_Q_LORA:MLA_Q_LORA + MLA_KV_LORA]
    wdk = w_down[:, MLA_Q_LORA + MLA_KV_LORA:]
    wdk = jnp.concatenate([wdk, _rotate_half_cols(wdk)], axis=-1)
    wuq = w_uq.reshape(MLA_Q_LORA, MLA_HEADS, MLA_NOPE + MLA_ROPE)
    wnope = wuq[..., :MLA_NOPE].reshape(MLA_Q_LORA, MLA_HEADS * MLA_NOPE)
    wrope = wuq[..., MLA_NOPE:]
    wr = jnp.concatenate([wrope, _rotate_half_cols(wrope)], axis=-1).reshape(MLA_Q_LORA, MLA_HEADS * LANES)
    wuk = jnp.transpose(w_uk, (1, 2, 0))
    wuvt = jnp.transpose(w_uv, (1, 2, 0))
    wuk_flat = w_uk.reshape(MLA_KV_LORA, MLA_HEADS * MLA_NOPE)
    wuv_flat_t = w_uv.reshape(MLA_KV_LORA, MLA_HEADS * MLA_V).T
    return dict(
        wdq=wdq.astype(BF16), wdkv=wdkv.astype(BF16), wdk=wdk.astype(BF16), qn=q_norm[None, :],
        kvn=kv_norm[None, :], wnope=wnope.astype(BF16), wr=wr.astype(BF16), wuk=wuk.astype(BF16),
        wuvt=wuvt.astype(BF16), wuk_flat=wuk_flat.astype(BF16), wuv_flat_t=wuv_flat_t.astype(BF16),
        wout=w_out.astype(BF16))


def kernel(x_prompt, x_sample, state_gla, cache_ckv, cache_kpe, norm_ffn1, w_ffn1_gate, w_ffn1_up, w_ffn1_down, norm_mix, norm_ffn2, w_ffn2_gate, w_ffn2_up, w_ffn2_down, w_gla_in, w_gla_gk_up, b_gla_gk, gla_out_norm, w_gla_out, w_mla_down, mla_q_norm, w_mla_uq, mla_kv_norm, w_mla_uk, w_mla_uv, w_mla_out, norm_final):
    depth = norm_ffn1.shape[0]
    n_mixers = 2
    d = x_prompt.shape[-1]
    mla_scale = (MLA_NOPE + MLA_ROPE) ** -0.5
    gf = norm_final[None, :]

    ffn1 = (norm_ffn1[:, None, :], w_ffn1_gate, w_ffn1_up, w_ffn1_down)
    ffn2 = (norm_ffn2[:, None, :], w_ffn2_gate, w_ffn2_up, w_ffn2_down)
    gla_w = [_prep_gla(w_gla_in[j], w_gla_gk_up[j], b_gla_gk[j], gla_out_norm[j]) for j in range(w_gla_in.shape[0])]
    mla_w = [_prep_mla(w_mla_down[j], mla_q_norm[j], w_mla_uq[j], mla_kv_norm[j], w_mla_uk[j], w_mla_uv[j],
                       w_mla_out[j]) for j in range(w_mla_down.shape[0])]

    def run(x, pos0, gla_s0, past_ckv, past_kpe, *, gla_nb, gla_tl, attn_tq, proj_nb):
        bsz, seq, _ = x.shape
        rot = _rope_table(pos0, seq)
        new_gla, new_ckv, new_kpe = [], [], []
        pending = {}

        def ffn(x, w, layer, final_norm=False):
            return _ffn_call(x.reshape(bsz * seq, d), w[0], w[1], w[2], w[3], gf, layer,
                             final_norm=final_norm, **pending).reshape(bsz, seq, d)

        for i in range(depth):
            x = ffn(x, ffn1, i)
            j = i // n_mixers
            gm = norm_mix[i][None, :]
            if i % n_mixers == 0:
                x, s_fin = _gla_call(x, gla_s0[j], gm, w_gla_in, *gla_w[j], w_gla_out, j, nb=gla_nb, tl=gla_tl)
                new_gla.append(s_fin)
            else:
                w = mla_w[j]
                absorb = past_ckv is not None
                q, k, vt, ckv, kpe = _mla_proj_call(
                    x, rot, gm, w["wdq"], w["wdkv"], w["wdk"], w["qn"], w["kvn"],
                    w["wnope"], w["wr"], w["wuk"] if absorb else w["wuk_flat"], w["wuv_flat_t"],
                    tm=attn_tq, nbp=proj_nb, absorb=absorb, q_scale=1.0 if absorb else mla_scale * LOG2E)
                if absorb:
                    x = _mla_attn_call(q, past_ckv[j], jnp.swapaxes(past_kpe[j], 1, 2), k, vt, x, w["wuvt"], w["wout"],
                                       tk=ATTN_KEY_TILE, nbs=ATTN_SEQ_GROUP, scale=mla_scale)
                else:
                    pending = dict(attn_t=_mha_attn_call(q, k, vt, hp=ATTN_HEAD_GROUP, tq=attn_tq), w_out=w["wout"])
                new_ckv.append(ckv)
                new_kpe.append(kpe)
            x = ffn(x, ffn2, i, final_norm=(i == depth - 1))
            pending = {}
        return x, jnp.stack(new_gla), jnp.stack(new_ckv), jnp.stack(new_kpe)

    bp, lp = x_prompt.shape[0], x_prompt.shape[1]
    n_gla = w_gla_in.shape[0]
    s0_p = jnp.zeros((n_gla, bp) + state_gla.shape[2:], x_prompt.dtype)
    y_p, gla_p, ckv_p, kpe_p = run(
        x_prompt, 0, s0_p, None, None, gla_nb=1, gla_tl=512, attn_tq=ATTN_KEY_TILE, proj_nb=1)

    past_len = cache_ckv.shape[2]
    ls = x_sample.shape[1]
    y_s, gla_s, ckv_s, kpe_s = run(
        x_sample, past_len, state_gla, cache_ckv, cache_kpe, gla_nb=8, gla_tl=ls, attn_tq=ls,
        proj_nb=8)

    return (y_p, y_s, gla_p, ckv_p, kpe_p, gla_s, ckv_s, kpe_s)
```

```python
import functools

import jax
import jax.numpy as jnp
from jax import lax
from jax.experimental import pallas as pl
from jax.experimental.pallas import tpu as pltpu

F32 = jnp.float32
BF16 = jnp.bfloat16

EPS = 1e-6
CHUNK = 64
GLA_HEADS = 4
GLA_GATE_NORM = 16.0
GLA_ROW_GROUPS = 2
MLA_HEADS = 8
MLA_NOPE = 128
MLA_ROPE = 64
MLA_V = 128
MLA_Q_LORA = 384
MLA_KV_LORA = 256
ROPE_THETA = 10000.0
LANES = 128
BF16_SUBLANES = 16
F32_SUBLANES = 8
MXU_COLS = 256
LOG2E = 1.4426950408889634
ATTN_COL_BLOCK = 512
ATTN_KEY_TILE = 512
ATTN_HEAD_GROUP = 4
ATTN_SEQ_GROUP = 2
V_ONES_ROWS = 16
ATTN_LAG_LIMIT = 64.0
KCAT = MLA_KV_LORA + LANES

VMEM_LIMIT = 56 * 1024 * 1024


def _rms(xf, g):
    return xf * lax.rsqrt(jnp.mean(xf * xf, axis=-1, keepdims=True) + EPS) * g


def _silu(a):
    return a * (1.0 / (1.0 + jnp.exp(-a)))


def _dot(a, b):
    return jnp.dot(a, b, preferred_element_type=F32)


def _dot_nt(a, b):
    return lax.dot_general(a, b, (((1,), (1,)), ((), ())), preferred_element_type=F32)


def _dot_tn(a, b):
    return lax.dot_general(a, b, (((0,), (0,)), ((), ())), preferred_element_type=F32)


def _resident(shape):
    zeros = (0,) * len(shape)
    return pl.BlockSpec(shape, lambda *_: zeros, pipeline_mode=pl.Buffered(1))


def _ffn_kernel(*refs, f_chunk, final_norm, mixer_proj):
    if mixer_proj:
        x_ref, ot_ref, wo_ref, g_ref, wg_ref, wu_ref, wd_ref, gf_ref, o_ref, hid_ref = refs
        x = x_ref[...] + _dot_tn(ot_ref[0], wo_ref[...])
    else:
        x_ref, g_ref, wg_ref, wu_ref, wd_ref, gf_ref, o_ref, hid_ref = refs
        x = x_ref[...]
    h = _rms(x, g_ref[0]).astype(BF16)
    d_ff = wg_ref.shape[2]
    for c in range(d_ff // f_chunk):
        sl = slice(c * f_chunk, (c + 1) * f_chunk)
        a = _dot(h, wg_ref[0, :, sl].astype(BF16))
        b = _dot(h, wu_ref[0, :, sl].astype(BF16))
        hid_ref[:, sl] = (_silu(a) * b).astype(BF16)
    y = x + 0.5 * _dot(hid_ref[...], wd_ref[0].astype(BF16))
    if final_norm:
        y = _rms(y, gf_ref[...])
    o_ref[...] = y


def _layer_resident(shape, layer):
    index = (layer,) + (0,) * (len(shape) - 1)
    return pl.BlockSpec((1,) + tuple(shape[1:]), lambda *_: index, pipeline_mode=pl.Buffered(1))


def _ffn_call(x, g, wg, wu, wd, gf, layer, *, final_norm, attn_t=None, w_out=None, tm=512, f_chunk=256):
    t, d = x.shape
    d_ff = wg.shape[2]
    assert t % tm == 0 and d_ff % f_chunk == 0
    mixer_proj = attn_t is not None
    x_spec = pl.BlockSpec((tm, d), lambda i: (i, 0))
    weights = (g, wg, wu, wd)
    if mixer_proj:
        tiles_per_seq = attn_t.shape[2] // tm
        assert attn_t.shape[0] * attn_t.shape[2] == t and attn_t.shape[2] % tm == 0
        ot_spec = pl.BlockSpec((1, attn_t.shape[1], tm), lambda i: (i // tiles_per_seq, 0, i % tiles_per_seq))
        operands = (x, attn_t, w_out) + weights + (gf,)
        in_specs = [x_spec, ot_spec, _resident(w_out.shape)]
    else:
        operands = (x,) + weights + (gf,)
        in_specs = [x_spec]
    return pl.pallas_call(
        functools.partial(_ffn_kernel, f_chunk=f_chunk, final_norm=final_norm, mixer_proj=mixer_proj),
        out_shape=jax.ShapeDtypeStruct((t, d), F32),
        grid=(t // tm,),
        in_specs=in_specs + [_layer_resident(w.shape, layer) for w in weights] + [_resident(gf.shape)],
        out_specs=pl.BlockSpec((tm, d), lambda i: (i, 0)),
        scratch_shapes=[pltpu.VMEM((tm, d_ff), BF16)],
        compiler_params=pltpu.CompilerParams(
            dimension_semantics=("parallel",), vmem_limit_bytes=VMEM_LIMIT),
        name="ffn",
    )(*operands)


def _split3(a):
    hi = a.astype(BF16)
    r = a - hi.astype(F32)
    mid = r.astype(BF16)
    lo = (r - mid.astype(F32)).astype(BF16)
    return hi, mid, lo


def _interleave(primary, filler):
    for n, step in enumerate(primary):
        step()
        if n < len(filler):
            filler[n]()
    for extra in filler[len(primary):]:
        extra()


def _gla_kernel(x_ref, s0_ref, gmix_ref, wq_ref, wk_ref, wv_ref, wg_ref, wlow_ref, wgk_ref, bgk_ref,
                gout_ref, wout_ref, y_ref, s_ref, st_s, *, nb, tl, n_sub):
    l = pl.program_id(1)
    heads = GLA_HEADS
    dk = wq_ref.shape[1] // heads
    dv = wv_ref.shape[1] // heads
    rows_sub = nb * tl // n_sub
    n_chunk = rows_sub // CHUNK
    d = x_ref.shape[-1]

    @pl.when(l == 0)
    def _():
        st_s[...] = s0_ref[...]

    ti = lax.broadcasted_iota(jnp.int32, (CHUNK, CHUNK), 0)
    si = lax.broadcasted_iota(jnp.int32, (CHUNK, CHUNK), 1)
    causal = si <= ti
    tri = jnp.where(causal, 1.0, 0.0).astype(BF16)

    def rows_of(ref, sub):
        row0 = sub * rows_sub
        if rows_sub >= tl:
            return ref.at[row0 // tl:(row0 + rows_sub) // tl]
        return ref.at[row0 // tl, row0 % tl:row0 % tl + rows_sub, :]

    def in_projection(sub):
        env = {}

        def start():
            env["x"] = rows_of(x_ref, sub)[...].reshape(rows_sub, d)
            env["h"] = _rms(env["x"], gmix_ref[...]).astype(BF16)

        def piece(name, ref, lo, hi):
            return lambda: env.setdefault(name, []).append(_dot(env["h"], ref[:, lo:hi]))

        thunks = [start]
        for name, ref in (("low", wlow_ref), ("q", wq_ref), ("k", wk_ref), ("v", wv_ref), ("g", wg_ref)):
            width = ref.shape[1]
            thunks += [piece(name, ref, lo, min(lo + MXU_COLS, width)) for lo in range(0, width, MXU_COLS)]
        return thunks, env

    def recurrence(sub, env):
        row0 = sub * rows_sub
        cat = lambda name: jnp.concatenate(env[name], axis=-1)
        loc = {}

        def start():
            loc["q"] = cat("q") * (dk ** -0.5)
            loc["k"] = cat("k")
            loc["v"] = cat("v").astype(BF16)
            z = _dot(cat("low").astype(BF16), wgk_ref[...]) + bgk_ref[...]
            loc["log_a"] = -(jnp.maximum(-z, 0.0) + jnp.log1p(jnp.exp(-jnp.abs(z)))) / GLA_GATE_NORM
            loc["state"] = {}
            loc["o"] = [[None] * n_chunk for _ in range(heads)]

        def cumsum(c):
            rs = slice(c * CHUNK, (c + 1) * CHUNK)
            a_hi, a_mid, a_lo = _split3(loc["log_a"][rs, :])
            loc["cum", c] = _dot(tri, a_hi) + _dot(tri, a_mid) + _dot(tri, a_lo)

        def scores(c):
            rs = slice(c * CHUNK, (c + 1) * CHUNK)
            cum = loc["cum", c]
            cum_last = cum[CHUNK - 1:CHUNK, :]
            q_dec = (loc["q"][rs, :] * jnp.exp(cum)).astype(BF16)
            k_inv = (loc["k"][rs, :] * jnp.exp(-cum)).astype(BF16)
            loc["k_tail", c] = (loc["k"][rs, :] * jnp.exp(cum_last - cum)).astype(BF16)
            loc["q_dec", c] = q_dec
            dec = jnp.broadcast_to(jnp.exp(cum_last), (F32_SUBLANES, heads * dk))
            loc["dec", c] = [dec[:, hd * dk:(hd + 1) * dk].T[:, :1] for hd in range(heads)]
            loc["scores", c] = [
                jnp.where(causal, _dot_nt(q_dec[:, hd * dk:(hd + 1) * dk], k_inv[:, hd * dk:(hd + 1) * dk]), 0.0)
                .astype(BF16) for hd in range(heads)]

        def values(c):
            rs = slice(c * CHUNK, (c + 1) * CHUNK)
            vs = [loc["v"][rs, hd * dv:(hd + 1) * dv] for hd in range(heads)]
            loc["o_intra", c] = [_dot(loc["scores", c][hd], vs[hd]) for hd in range(heads)]
            loc["kv", c] = [_dot_tn(loc["k_tail", c][:, hd * dk:(hd + 1) * dk], vs[hd]) for hd in range(heads)]

        def carry(c):
            b = (row0 + c * CHUNK) // tl
            for hd in range(heads):
                ks = slice(hd * dk, (hd + 1) * dk)
                st = loc["state"].get((b, hd))
                if st is None:
                    st = st_s[b, hd]
                loc["o"][hd][c] = loc["o_intra", c][hd] + _dot(loc["q_dec", c][:, ks], st.astype(BF16))
                loc["state"][(b, hd)] = st * loc["dec", c][hd] + loc["kv", c][hd]

        def finish():
            for (b, hd), st in loc["state"].items():
                st_s[b, hd] = st
            env["o"] = [jnp.concatenate(loc["o"][hd], axis=0) for hd in range(heads)]

        stages = (cumsum, scores, values, carry)
        thunks = [start]
        for step in range(n_chunk + len(stages) - 1):
            for st in reversed(range(len(stages))):
                if 0 <= step - st < n_chunk:
                    thunks.append(functools.partial(stages[st], step - st))
        return thunks + [finish]

    def out_projection(sub, env):
        loc = {}

        def gate():
            g = jnp.concatenate(env["g"], axis=-1)
            loc["gated"] = jnp.concatenate(
                [(_rms(env["o"][hd], gout_ref[...]) * _silu(g[:, hd * dv:(hd + 1) * dv])).astype(BF16)
                 for hd in range(heads)], axis=-1)

        def piece(lo):
            return lambda: loc.setdefault("y", []).append(_dot(loc["gated"], wout_ref[:, lo:lo + MXU_COLS]))

        def store():
            y = env["x"] + jnp.concatenate(loc["y"], axis=-1)
            dst = rows_of(y_ref, sub)
            dst[...] = y.reshape(dst.shape)

        return [gate] + [piece(lo) for lo in range(0, d, MXU_COLS)] + [store]

    proj = [in_projection(sub) for sub in range(n_sub)]
    for step in proj[0][0]:
        step()
    for sub in range(n_sub):
        filler = []
        if sub + 1 < n_sub:
            filler += proj[sub + 1][0]
        if sub >= 1:
            filler += out_projection(sub - 1, proj[sub - 1][1])
        _interleave(recurrence(sub, proj[sub][1]), filler)
    for step in out_projection(n_sub - 1, proj[n_sub - 1][1]):
        step()

    @pl.when(l == pl.num_programs(1) - 1)
    def _():
        s_ref[...] = st_s[...]


def _gla_call(x, s0, gmix, wq, wk, wv, wg, wlow, wgk, bgk, gout, wout, *, nb, tl, n_sub=GLA_ROW_GROUPS):
    bsz, seq, d = x.shape
    heads = GLA_HEADS
    dk = wq.shape[1] // heads
    dv = wv.shape[1] // heads
    rows_sub = nb * tl // n_sub
    assert bsz % nb == 0 and seq % tl == 0 and tl % CHUNK == 0 and (nb * tl) % n_sub == 0
    assert rows_sub % CHUNK == 0 and (rows_sub % tl == 0 or tl % rows_sub == 0)
    return pl.pallas_call(
        functools.partial(_gla_kernel, nb=nb, tl=tl, n_sub=n_sub),
        out_shape=(jax.ShapeDtypeStruct((bsz, seq, d), F32),
                   jax.ShapeDtypeStruct((bsz, heads, dk, dv), F32)),
        grid=(bsz // nb, seq // tl),
        in_specs=[
            pl.BlockSpec((nb, tl, d), lambda b, l: (b, l, 0)),
            pl.BlockSpec((nb, heads, dk, dv), lambda b, l: (b, 0, 0, 0)),
            _resident(gmix.shape), _resident(wq.shape), _resident(wk.shape), _resident(wv.shape),
            _resident(wg.shape), _resident(wlow.shape), _resident(wgk.shape), _resident(bgk.shape),
            _resident(gout.shape), _resident(wout.shape),
        ],
        out_specs=(pl.BlockSpec((nb, tl, d), lambda b, l: (b, l, 0)),
                   pl.BlockSpec((nb, heads, dk, dv), lambda b, l: (b, 0, 0, 0))),
        scratch_shapes=[pltpu.VMEM((nb, heads, dk, dv), F32)],
        compiler_params=pltpu.CompilerParams(
            dimension_semantics=("parallel", "arbitrary"), vmem_limit_bytes=VMEM_LIMIT),
        name="gla",
    )(x, s0, gmix, wq, wk, wv, wg, wlow, wgk, bgk, gout, wout)


def _mla_proj_kernel(x_ref, rot_ref, gmix_ref, wdq_ref, wdkv_ref, wdk_ref, qn_ref, kvn_ref,
                     wnope_ref, wr_ref, wkey_ref, wval_ref,
                     q_ref, k_ref, vt_ref, ckv_ref, kpe_ref, *, absorb, q_scale):
    nbp, tm, d = x_ref.shape
    rows = nbp * tm
    x = x_ref[...].reshape(rows, d)
    h = _rms(x, gmix_ref[...]).astype(BF16)
    rot = jnp.concatenate([rot_ref[...]] * nbp, axis=0)

    def rope(pair):
        prod = pair * rot
        return prod + pltpu.roll(prod, MLA_ROPE, axis=1)

    ckv = _rms(_dot(h, wdkv_ref[...]), kvn_ref[...])
    rope_lanes = lax.broadcasted_iota(jnp.int32, (rows, LANES), 1) < MLA_ROPE
    kpe = jnp.where(rope_lanes, rope(_dot(h, wdk_ref[...])), 0.0)
    ckv_ref[...] = ckv.reshape(nbp, tm, MLA_KV_LORA)
    kpe_ref[...] = kpe[:, :MLA_ROPE].reshape(nbp, tm, MLA_ROPE)
    ckv_b = ckv.astype(BF16)
    kpe_b = kpe.astype(BF16)
    ckv_t = ckv.T.astype(BF16)

    hq = _rms(_dot(h, wdq_ref[...]), qn_ref[...]).astype(BF16)
    q_nope = _dot(hq, wnope_ref[...])
    q_r = _dot(hq, wr_ref[...])

    if absorb:
        k_ref[:, :, :MLA_KV_LORA] = ckv_b.reshape(nbp, tm, MLA_KV_LORA)
        k_ref[:, :, MLA_KV_LORA:] = kpe_b.reshape(nbp, tm, LANES)
        for b in range(nbp):
            vt_ref[b, 0] = ckv_t[:, b * tm:(b + 1) * tm]
    else:
        k_nope = _dot(ckv_b, wkey_ref[...]).astype(BF16)
        v_t = _dot(wval_ref[...], ckv_t).astype(BF16)
        ones = jnp.ones((V_ONES_ROWS, tm), BF16)
    for hd in range(MLA_HEADS):
        hs = slice(hd * LANES, (hd + 1) * LANES)
        q_pe = rope(q_r[:, hs])
        if absorb:
            q_lat = _dot(q_nope[:, hs].astype(BF16), wkey_ref[hd])
            q_t = jnp.concatenate([q_lat, q_pe], axis=-1).T.astype(BF16)
            for b in range(nbp):
                q_ref[b, 0, :, hd * tm:(hd + 1) * tm] = q_t[:, b * tm:(b + 1) * tm]
        else:
            for b in range(nbp):
                bs = slice(b * tm, (b + 1) * tm)
                q_ref[b, hd, 0, :MLA_NOPE] = (q_nope[bs, hs] * q_scale).T.astype(BF16)
                q_ref[b, hd, 0, MLA_NOPE:] = (q_pe[bs] * q_scale).T.astype(BF16)
                k_ref[b, hd, :, :MLA_NOPE] = k_nope[bs, hs]
                k_ref[b, hd, :, MLA_NOPE:] = kpe_b[bs]
                vt_ref[b, hd, 0, :MLA_V] = v_t[hd * MLA_V:(hd + 1) * MLA_V, bs]
                vt_ref[b, hd, 0, MLA_V:] = ones


def _mla_proj_call(x, rot, gmix, wdq, wdkv, wdk, qn, kvn, wnope, wr, wkey, wval, *, tm, nbp, absorb, q_scale=1.0):
    bsz, seq, d = x.shape
    assert seq % tm == 0 and bsz % nbp == 0
    weights = (gmix, wdq, wdkv, wdk, qn, kvn, wnope, wr, wkey, wval)
    nt = seq // tm
    if absorb:
        q_k_vt = (jax.ShapeDtypeStruct((bsz, nt, KCAT, MLA_HEADS * tm), BF16),
                  jax.ShapeDtypeStruct((bsz, seq, KCAT), BF16),
                  jax.ShapeDtypeStruct((bsz, nt, MLA_KV_LORA, tm), BF16))
        q_k_vt_specs = (pl.BlockSpec((nbp, 1, KCAT, MLA_HEADS * tm), lambda b, l: (b, l, 0, 0)),
                        pl.BlockSpec((nbp, tm, KCAT), lambda b, l: (b, l, 0)),
                        pl.BlockSpec((nbp, 1, MLA_KV_LORA, tm), lambda b, l: (b, l, 0, 0)))
    else:
        qk_dim = MLA_NOPE + LANES
        v_rows = MLA_V + V_ONES_ROWS
        q_k_vt = (jax.ShapeDtypeStruct((bsz, MLA_HEADS, nt, qk_dim, tm), BF16),
                  jax.ShapeDtypeStruct((bsz, MLA_HEADS, seq, qk_dim), BF16),
                  jax.ShapeDtypeStruct((bsz, MLA_HEADS, nt, v_rows, tm), BF16))
        q_k_vt_specs = (pl.BlockSpec((nbp, MLA_HEADS, 1, qk_dim, tm), lambda b, l: (b, 0, l, 0, 0)),
                        pl.BlockSpec((nbp, MLA_HEADS, tm, qk_dim), lambda b, l: (b, 0, l, 0)),
                        pl.BlockSpec((nbp, MLA_HEADS, 1, v_rows, tm), lambda b, l: (b, 0, l, 0, 0)))
    return pl.pallas_call(
        functools.partial(_mla_proj_kernel, absorb=absorb, q_scale=q_scale),
        out_shape=q_k_vt + (jax.ShapeDtypeStruct((bsz, seq, MLA_KV_LORA), F32),
                            jax.ShapeDtypeStruct((bsz, seq, MLA_ROPE), F32)),
        grid=(bsz // nbp, nt),
        in_specs=[
            pl.BlockSpec((nbp, tm, d), lambda b, l: (b, l, 0)),
            pl.BlockSpec((tm, LANES), lambda b, l: (l, 0)),
        ] + [_resident(w.shape) for w in weights],
        out_specs=q_k_vt_specs + (pl.BlockSpec((nbp, tm, MLA_KV_LORA), lambda b, l: (b, l, 0)),
                                  pl.BlockSpec((nbp, tm, MLA_ROPE), lambda b, l: (b, l, 0))),
        compiler_params=pltpu.CompilerParams(
            dimension_semantics=("parallel", "parallel"), vmem_limit_bytes=VMEM_LIMIT),
        name="mla_proj",
    )(x, rot, *weights)


def _exp2_scaled(t, c):
    return jnp.exp2(t if c == 1.0 else t * c)


def _tile_update(s, vt, cs, m_s, l_s, acc_s, lag_s, c, one_pass):
    m_prev = m_s[:, cs]
    mx = jnp.max(s, axis=0, keepdims=True)
    m_new = jnp.maximum(m_prev, mx)
    alpha = _exp2_scaled(m_prev - m_new, c)
    if one_pass:
        p = _exp2_scaled(s - m_prev, c)
        if l_s is not None:
            l_s[:, cs] = (l_s[:, cs] + jnp.sum(p, axis=0, keepdims=True)) * alpha
        acc_s[:, cs] = (acc_s[:, cs] + _dot(vt, p.astype(BF16))) * alpha
        lag_s[...] = jnp.maximum(lag_s[...], (mx - m_prev) * c)
    else:
        p = _exp2_scaled(s - m_new, c)
        if l_s is not None:
            l_s[:, cs] = alpha * l_s[:, cs] + jnp.sum(p, axis=0, keepdims=True)
        acc_s[:, cs] = acc_s[:, cs] * alpha + _dot(vt, p.astype(BF16))
    m_s[:, cs] = m_new


def _attend(n_full, n_blk, cb, first_key_scores, diag_scores, diag_values, tile_scores, tile_values,
            m_s, l_s, acc_s, lag_s, c):
    def cols(i):
        return slice(i * cb, (i + 1) * cb)

    def run(one_pass):
        lag_s[...] = jnp.zeros(lag_s.shape, F32)
        for i in range(n_blk):
            m_s[:, cols(i)] = first_key_scores(i)
            acc_s[:, cols(i)] = jnp.zeros((acc_s.shape[0], cb), F32)
            if l_s is not None:
                l_s[:, cols(i)] = jnp.zeros((1, cb), F32)

        def update(units):
            s_next = units[0][0]()
            for u, (_, values, i) in enumerate(units):
                s = s_next
                if u + 1 < len(units):
                    s_next = units[u + 1][0]()
                _tile_update(s, values(), cols(i), m_s, l_s, acc_s, lag_s, c, one_pass)

        def full_tiles(js):
            return [(functools.partial(tile_scores, j, i), functools.partial(tile_values, j, i), i)
                    for j in js for i in range(n_blk)]

        update([(functools.partial(diag_scores, i), functools.partial(diag_values, i), i) for i in range(n_blk)])

        def pair(jj, carry):
            update(full_tiles((2 * jj, 2 * jj + 1)))
            return carry

        if isinstance(n_full, int):
            for jj in range(n_full // 2):
                pair(jj, 0)
            if n_full % 2:
                update(full_tiles((n_full - 1,)))
        else:
            lax.fori_loop(0, n_full // 2, pair, 0)

            @pl.when(n_full % 2 == 1)
            def _():
                update(full_tiles((n_full - 1,)))

    run(one_pass=True)

    @pl.when(jnp.max(lag_s[...]) > ATTN_LAG_LIMIT)
    def _():
        run(one_pass=False)


def _block_causal(tq, cb, col0):
    ki = lax.broadcasted_iota(jnp.int32, (tq, cb), 0) // CHUNK
    qi = ((lax.broadcasted_iota(jnp.int32, (tq, cb), 1) + col0) % tq) // CHUNK
    return ki <= qi


def _mla_attn_kernel(qt_ref, pckv_ref, pkpet_ref, kdiag_ref, vtdiag_ref, x_ref, wuvt_ref, wout_ref, y_ref,
                     o_s, m_s, l_s, acc_s, lag_s, *, tq, tk, cb, scale):
    nbs = qt_ref.shape[0]
    heads = o_s.shape[0]
    blk_per_seq = heads * tq // cb
    past_tiles = pckv_ref.shape[1] // tk

    def qt_blk(i):
        return qt_ref[i // blk_per_seq, 0, :, (i % blk_per_seq) * cb:(i % blk_per_seq + 1) * cb]

    def diag_scores(i):
        visible = _block_causal(tq, cb, (i % blk_per_seq) * cb)
        return jnp.where(visible, _dot(kdiag_ref[i // blk_per_seq], qt_blk(i)), -jnp.inf)

    def tile_scores(j, i):
        rows = slice(j * tk, (j + 1) * tk)
        latent = pckv_ref[i // blk_per_seq, rows, :].astype(BF16)
        rotary_t = pkpet_ref[i // blk_per_seq, :, rows].astype(BF16)
        qt = qt_blk(i)
        return _dot(latent, qt[:MLA_KV_LORA]) + _dot_tn(rotary_t, qt[MLA_KV_LORA:MLA_KV_LORA + MLA_ROPE])

    def tile_values(j, i):
        return pckv_ref[i // blk_per_seq, j * tk:(j + 1) * tk, :].T.astype(BF16)

    def first_key_scores(i):
        return _dot(kdiag_ref[i // blk_per_seq, :BF16_SUBLANES, :], qt_blk(i))[:1]

    _attend(past_tiles, nbs * blk_per_seq, cb, first_key_scores, diag_scores,
            lambda i: vtdiag_ref[i // blk_per_seq, 0], tile_scores, tile_values,
            m_s, l_s, acc_s, lag_s, scale * LOG2E)

    for s in range(nbs):
        for hd in range(heads):
            cs = slice((s * heads + hd) * tq, (s * heads + hd + 1) * tq)
            o_lat_t = (acc_s[:, cs] * (1.0 / l_s[:, cs])).astype(BF16)
            o_s[hd] = _dot(wuvt_ref[hd], o_lat_t).astype(BF16)
        o_t = o_s[...].reshape(o_s.shape[0] * o_s.shape[1], tq)
        y_ref[s] = x_ref[s] + _dot_tn(o_t, wout_ref[...])


def _mla_attn_call(qt, past_ckv, past_kpe_t, kdiag, vtdiag, x, wuvt, wout, *, tk, nbs, scale):
    bsz, tq, d = x.shape
    heads = wuvt.shape[0]
    past = past_ckv.shape[1]
    assert past % tk == 0 and bsz % nbs == 0 and qt.shape == (bsz, 1, KCAT, heads * tq)
    assert past_kpe_t.shape == (bsz, MLA_ROPE, past) and vtdiag.shape == (bsz, 1, MLA_KV_LORA, tq)
    cb = max(tq, min(heads * tq, ATTN_COL_BLOCK))
    assert cb % tq == 0 and (heads * tq) % cb == 0
    v_dim = wuvt.shape[1]
    return pl.pallas_call(
        functools.partial(_mla_attn_kernel, tq=tq, tk=tk, cb=cb, scale=scale),
        out_shape=jax.ShapeDtypeStruct((bsz, tq, d), F32),
        grid=(bsz // nbs,),
        in_specs=[
            pl.BlockSpec((nbs, 1, KCAT, heads * tq), lambda b: (b, 0, 0, 0)),
            pl.BlockSpec((nbs, past, MLA_KV_LORA), lambda b: (b, 0, 0)),
            pl.BlockSpec((nbs, MLA_ROPE, past), lambda b: (b, 0, 0)),
            pl.BlockSpec((nbs, tq, KCAT), lambda b: (b, 0, 0)),
            pl.BlockSpec((nbs, 1, MLA_KV_LORA, tq), lambda b: (b, 0, 0, 0)),
            pl.BlockSpec((nbs, tq, d), lambda b: (b, 0, 0)),
            _resident(wuvt.shape),
            _resident(wout.shape),
        ],
        out_specs=pl.BlockSpec((nbs, tq, d), lambda b: (b, 0, 0)),
        scratch_shapes=[
            pltpu.VMEM((heads, v_dim, tq), BF16),
            pltpu.VMEM((1, nbs * heads * tq), F32),
            pltpu.VMEM((1, nbs * heads * tq), F32),
            pltpu.VMEM((MLA_KV_LORA, nbs * heads * tq), F32),
            pltpu.VMEM((1, cb), F32),
        ],
        compiler_params=pltpu.CompilerParams(
            dimension_semantics=("parallel",), vmem_limit_bytes=VMEM_LIMIT),
        name="mla_attn",
    )(qt, past_ckv, past_kpe_t, kdiag, vtdiag, x, wuvt, wout)


def _mha_attn_kernel(qt_ref, k_ref, vt_ref, kdiag_ref, vtdiag_ref, ot_ref, m_s, acc_s, lag_s, *, tq):
    qb = pl.program_id(2)
    hp = qt_ref.shape[1]
    visible = _block_causal(tq, tq, 0)

    def diag_scores(i):
        return jnp.where(visible, _dot(kdiag_ref[0, i], qt_ref[0, i, 0]), -jnp.inf)

    def tile_scores(j, i):
        return _dot(k_ref[0, i, pl.ds(pl.multiple_of(j * tq, tq), tq), :], qt_ref[0, i, 0])

    def first_key_scores(i):
        return _dot(kdiag_ref[0, i, :BF16_SUBLANES, :], qt_ref[0, i, 0])[:1]

    _attend(qb, hp, tq, first_key_scores, diag_scores, lambda i: vtdiag_ref[0, i, 0],
            tile_scores, lambda j, i: vt_ref[0, i, j], m_s, None, acc_s, lag_s, 1.0)

    for i in range(hp):
        cs = slice(i * tq, (i + 1) * tq)
        inv_l = 1.0 / acc_s[MLA_V:MLA_V + 1, cs]
        ot_ref[0, i * MLA_V:(i + 1) * MLA_V, :] = (acc_s[:MLA_V, cs] * inv_l).astype(BF16)


def _mha_attn_call(qt, k, vt, *, hp, tq):
    bsz, heads, seq, qk_dim = k.shape
    nq = seq // tq
    v_rows = MLA_V + V_ONES_ROWS
    assert qt.shape == (bsz, heads, nq, qk_dim, tq)
    assert seq % tq == 0 and heads % hp == 0 and vt.shape == (bsz, heads, nq, v_rows, tq)
    return pl.pallas_call(
        functools.partial(_mha_attn_kernel, tq=tq),
        out_shape=jax.ShapeDtypeStruct((bsz, heads * MLA_V, seq), BF16),
        grid=(bsz, heads // hp, nq),
        in_specs=[
            pl.BlockSpec((1, hp, 1, qk_dim, tq), lambda b, g, i: (b, g, i, 0, 0)),
            pl.BlockSpec((1, hp, seq, qk_dim), lambda b, g, i: (b, g, 0, 0)),
            pl.BlockSpec((1, hp, nq, v_rows, tq), lambda b, g, i: (b, g, 0, 0, 0), pipeline_mode=pl.Buffered(1)),
            pl.BlockSpec((1, hp, tq, qk_dim), lambda b, g, i: (b, g, i, 0)),
            pl.BlockSpec((1, hp, 1, v_rows, tq), lambda b, g, i: (b, g, i, 0, 0)),
        ],
        out_specs=pl.BlockSpec((1, hp * MLA_V, tq), lambda b, g, i: (b, g, i)),
        scratch_shapes=[
            pltpu.VMEM((1, hp * tq), F32),
            pltpu.VMEM((v_rows, hp * tq), F32),
            pltpu.VMEM((1, tq), F32),
        ],
        compiler_params=pltpu.CompilerParams(
            dimension_semantics=("parallel", "parallel", "arbitrary"), vmem_limit_bytes=VMEM_LIMIT),
        name="mha_attn",
    )(qt, k, vt, k, vt)


def _rope_table(pos0, n):
    half = MLA_ROPE // 2
    inv = ROPE_THETA ** (-jnp.arange(half, dtype=F32) / half)

    def cos_sin(p):
        ang = p.astype(F32)[:, None] * inv[None, :]
        return jnp.cos(ang), jnp.sin(ang)

    blk = ATTN_KEY_TILE
    if n > blk and n % blk == 0:
        ch, sh = cos_sin(pos0 + blk * jnp.arange(n // blk))
        cl, sl = cos_sin(jnp.arange(blk))
        cos = (ch[:, None, :] * cl[None] - sh[:, None, :] * sl[None]).reshape(n, half)
        sin = (sh[:, None, :] * cl[None] + ch[:, None, :] * sl[None]).reshape(n, half)
    else:
        cos, sin = cos_sin(pos0 + jnp.arange(n))
    return jnp.concatenate([cos, cos, sin, sin], axis=-1)


def _rotate_half_cols(w):
    half = w.shape[-1] // 2
    return jnp.concatenate([-w[..., half:], w[..., :half]], axis=-1)


def _pad_last(w, n):
    return jnp.concatenate([w, jnp.zeros(w.shape[:-1] + (n - w.shape[-1],), w.dtype)], axis=-1)


def _prep_gla(w_in, w_gk_up, b_gk, g_norm, w_out):
    qk = w_gk_up.shape[1]
    vw = w_out.shape[0]
    wq = w_in[:, :qk]
    wk = w_in[:, qk:2 * qk]
    wv = w_in[:, 2 * qk:2 * qk + vw]
    wg = w_in[:, 2 * qk + vw:2 * qk + 2 * vw]
    wlow = _pad_last(w_in[:, 2 * qk + 2 * vw:], LANES)
    rank = w_gk_up.shape[0]
    wgk = jnp.concatenate([w_gk_up, jnp.zeros((LANES - rank, qk), w_gk_up.dtype)], axis=0)
    return (wq.astype(BF16), wk.astype(BF16), wv.astype(BF16), wg.astype(BF16), wlow.astype(BF16),
            wgk.astype(BF16), b_gk[None, :], g_norm[None, :], w_out.astype(BF16))


def _prep_mla(w_down, q_norm, w_uq, kv_norm, w_uk, w_uv, w_out):
    wdq = w_down[:, :MLA_Q_LORA]
    wdkv = w_down[:, MLA_Q_LORA:MLA_Q_LORA + MLA_KV_LORA]
    wdk = w_down[:, MLA_Q_LORA + MLA_KV_LORA:]
    wdk = jnp.concatenate([wdk, _rotate_half_cols(wdk)], axis=-1)
    wuq = w_uq.reshape(MLA_Q_LORA, MLA_HEADS, MLA_NOPE + MLA_ROPE)
    wnope = wuq[..., :MLA_NOPE].reshape(MLA_Q_LORA, MLA_HEADS * MLA_NOPE)
    wrope = wuq[..., MLA_NOPE:]
    wr = jnp.concatenate([wrope, _rotate_half_cols(wrope)], axis=-1).reshape(MLA_Q_LORA, MLA_HEADS * LANES)
    wuk = jnp.transpose(w_uk, (1, 2, 0))
    wuvt = jnp.transpose(w_uv, (1, 2, 0))
    wuk_flat = w_uk.reshape(MLA_KV_LORA, MLA_HEADS * MLA_NOPE)
    wuv_flat_t = w_uv.reshape(MLA_KV_LORA, MLA_HEADS * MLA_V).T
    return dict(
        wdq=wdq.astype(BF16), wdkv=wdkv.astype(BF16), wdk=wdk.astype(BF16), qn=q_norm[None, :],
        kvn=kv_norm[None, :], wnope=wnope.astype(BF16), wr=wr.astype(BF16), wuk=wuk.astype(BF16),
        wuvt=wuvt.astype(BF16), wuk_flat=wuk_flat.astype(BF16), wuv_flat_t=wuv_flat_t.astype(BF16),
        wout=w_out.astype(BF16))


def kernel(x_prompt, x_sample, state_gla, cache_ckv, cache_kpe, norm_ffn1, w_ffn1_gate, w_ffn1_up, w_ffn1_down, norm_mix, norm_ffn2, w_ffn2_gate, w_ffn2_up, w_ffn2_down, w_gla_in, w_gla_gk_up, b_gla_gk, gla_out_norm, w_gla_out, w_mla_down, mla_q_norm, w_mla_uq, mla_kv_norm, w_mla_uk, w_mla_uv, w_mla_out, norm_final):
    depth = norm_ffn1.shape[0]
    n_mixers = 2
    d = x_prompt.shape[-1]
    mla_scale = (MLA_NOPE + MLA_ROPE) ** -0.5
    gf = norm_final[None, :]

    ffn1 = (norm_ffn1[:, None, :], w_ffn1_gate, w_ffn1_up, w_ffn1_down)
    ffn2 = (norm_ffn2[:, None, :], w_ffn2_gate, w_ffn2_up, w_ffn2_down)
    gla_w = [_prep_gla(w_gla_in[j], w_gla_gk_up[j], b_gla_gk[j], gla_out_norm[j], w_gla_out[j])
             for j in range(w_gla_in.shape[0])]
    mla_w = [_prep_mla(w_mla_down[j], mla_q_norm[j], w_mla_uq[j], mla_kv_norm[j], w_mla_uk[j], w_mla_uv[j],
                       w_mla_out[j]) for j in range(w_mla_down.shape[0])]

    def run(x, pos0, gla_s0, past_ckv, past_kpe, *, gla_nb, gla_tl, attn_tq, proj_nb):
        bsz, seq, _ = x.shape
        rot = _rope_table(pos0, seq)
        new_gla, new_ckv, new_kpe = [], [], []
        pending = {}

        def ffn(x, w, layer, final_norm=False):
            return _ffn_call(x.reshape(bsz * seq, d), w[0], w[1], w[2], w[3], gf, layer,
                             final_norm=final_norm, **pending).reshape(bsz, seq, d)

        for i in range(depth):
            x = ffn(x, ffn1, i)
            j = i // n_mixers
            gm = norm_mix[i][None, :]
            if i % n_mixers == 0:
                x, s_fin = _gla_call(x, gla_s0[j], gm, *gla_w[j], nb=gla_nb, tl=gla_tl)
                new_gla.append(s_fin)
            else:
                w = mla_w[j]
                absorb = past_ckv is not None
                q, k, vt, ckv, kpe = _mla_proj_call(
                    x, rot, gm, w["wdq"], w["wdkv"], w["wdk"], w["qn"], w["kvn"],
                    w["wnope"], w["wr"], w["wuk"] if absorb else w["wuk_flat"], w["wuv_flat_t"],
                    tm=attn_tq, nbp=proj_nb, absorb=absorb, q_scale=1.0 if absorb else mla_scale * LOG2E)
                if absorb:
                    x = _mla_attn_call(q, past_ckv[j], jnp.swapaxes(past_kpe[j], 1, 2), k, vt, x, w["wuvt"], w["wout"],
                                       tk=ATTN_KEY_TILE, nbs=ATTN_SEQ_GROUP, scale=mla_scale)
                else:
                    pending = dict(attn_t=_mha_attn_call(q, k, vt, hp=ATTN_HEAD_GROUP, tq=attn_tq), w_out=w["wout"])
                new_ckv.append(ckv)
                new_kpe.append(kpe)
            x = ffn(x, ffn2, i, final_norm=(i == depth - 1))
            pending = {}
        return x, jnp.stack(new_gla), jnp.stack(new_ckv), jnp.stack(new_kpe)

    bp, lp = x_prompt.shape[0], x_prompt.shape[1]
    n_gla = w_gla_in.shape[0]
    s0_p = jnp.zeros((n_gla, bp) + state_gla.shape[2:], x_prompt.dtype)
    y_p, gla_p, ckv_p, kpe_p = run(
        x_prompt, 0, s0_p, None, None, gla_nb=1, gla_tl=512, attn_tq=ATTN_KEY_TILE, proj_nb=1)

    past_len = cache_ckv.shape[2]
    ls = x_sample.shape[1]
    y_s, gla_s, ckv_s, kpe_s = run(
        x_sample, past_len, state_gla, cache_ckv, cache_kpe, gla_nb=8, gla_tl=ls, attn_tq=ls,
        proj_nb=8)

    return (y_p, y_s, gla_p, ckv_p, kpe_p, gla_s, ckv_s, kpe_s)
```

```python
import functools

import jax
import jax.numpy as jnp
from jax import lax
from jax.experimental import pallas as pl
from jax.experimental.pallas import tpu as pltpu

F32 = jnp.float32
BF16 = jnp.bfloat16

EPS = 1e-6
CHUNK = 64
GLA_HEADS = 4
GLA_GATE_NORM = 16.0
GLA_ROW_GROUPS = 2
MLA_HEADS = 8
MLA_NOPE = 128
MLA_ROPE = 64
MLA_V = 128
MLA_Q_LORA = 384
MLA_KV_LORA = 256
ROPE_THETA = 10000.0
LANES = 128
BF16_SUBLANES = 16
F32_SUBLANES = 8
MXU_COLS = 256
LOG2E = 1.4426950408889634
ATTN_COL_BLOCK = 512
ATTN_KEY_TILE = 512
ATTN_HEAD_GROUP = 4
ATTN_SEQ_GROUP = 2
V_ONES_ROWS = 16
ATTN_LAG_LIMIT = 64.0
KCAT = MLA_KV_LORA + LANES

VMEM_LIMIT = 56 * 1024 * 1024


def _rms(xf, g):
    return xf * lax.rsqrt(jnp.mean(xf * xf, axis=-1, keepdims=True) + EPS) * g


def _silu(a):
    return a * (1.0 / (1.0 + jnp.exp(-a)))


def _dot(a, b):
    return jnp.dot(a, b, preferred_element_type=F32)


def _dot_nt(a, b):
    return lax.dot_general(a, b, (((1,), (1,)), ((), ())), preferred_element_type=F32)


def _dot_tn(a, b):
    return lax.dot_general(a, b, (((0,), (0,)), ((), ())), preferred_element_type=F32)


def _resident(shape):
    zeros = (0,) * len(shape)
    return pl.BlockSpec(shape, lambda *_: zeros, pipeline_mode=pl.Buffered(1))


def _ffn_kernel(*refs, f_chunk, final_norm, mixer_proj, layer):
    if mixer_proj:
        x_ref, ot_ref, wo_ref, g_ref, wg_ref, wu_ref, wd_hbm, gf_ref, o_ref, hid_ref, wd_s, wd_sem = refs
        x = x_ref[...] + _dot_tn(ot_ref[0], wo_ref[...])
    else:
        x_ref, g_ref, wg_ref, wu_ref, wd_hbm, gf_ref, o_ref, hid_ref, wd_s, wd_sem = refs
        x = x_ref[...]
    first = pl.program_id(0) == 0
    wd_copy = pltpu.make_async_copy(wd_hbm.at[layer], wd_s, wd_sem)

    @pl.when(first)
    def _():
        wd_copy.start()

    h = _rms(x, g_ref[0]).astype(BF16)
    d_ff = wg_ref.shape[2]
    for c in range(d_ff // f_chunk):
        sl = slice(c * f_chunk, (c + 1) * f_chunk)
        a = _dot(h, wg_ref[0, :, sl].astype(BF16))
        b = _dot(h, wu_ref[0, :, sl].astype(BF16))
        hid_ref[:, sl] = (_silu(a) * b).astype(BF16)
    @pl.when(first)
    def _():
        wd_copy.wait()

    y = x + 0.5 * _dot(hid_ref[...], wd_s[...].astype(BF16))
    if final_norm:
        y = _rms(y, gf_ref[...])
    o_ref[...] = y


def _layer_resident(shape, layer):
    index = (layer,) + (0,) * (len(shape) - 1)
    return pl.BlockSpec((1,) + tuple(shape[1:]), lambda *_: index, pipeline_mode=pl.Buffered(1))


def _ffn_call(x, g, wg, wu, wd, gf, layer, *, final_norm, attn_t=None, w_out=None, tm=512, f_chunk=256):
    t, d = x.shape
    d_ff = wg.shape[2]
    assert t % tm == 0 and d_ff % f_chunk == 0
    mixer_proj = attn_t is not None
    x_spec = pl.BlockSpec((tm, d), lambda i: (i, 0))
    weights = (g, wg, wu, wd)
    if mixer_proj:
        tiles_per_seq = attn_t.shape[2] // tm
        assert attn_t.shape[0] * attn_t.shape[2] == t and attn_t.shape[2] % tm == 0
        ot_spec = pl.BlockSpec((1, attn_t.shape[1], tm), lambda i: (i // tiles_per_seq, 0, i % tiles_per_seq))
        operands = (x, attn_t, w_out) + weights + (gf,)
        in_specs = [x_spec, ot_spec, _resident(w_out.shape)]
    else:
        operands = (x,) + weights + (gf,)
        in_specs = [x_spec]
    return pl.pallas_call(
        functools.partial(_ffn_kernel, f_chunk=f_chunk, final_norm=final_norm, mixer_proj=mixer_proj, layer=layer),
        out_shape=jax.ShapeDtypeStruct((t, d), F32),
        grid=(t // tm,),
        in_specs=(in_specs + [_layer_resident(w.shape, layer) for w in (g, wg, wu)]
                  + [pl.BlockSpec(memory_space=pl.ANY), _resident(gf.shape)]),
        out_specs=pl.BlockSpec((tm, d), lambda i: (i, 0)),
        scratch_shapes=[pltpu.VMEM((tm, d_ff), BF16), pltpu.VMEM(wd.shape[1:], F32), pltpu.SemaphoreType.DMA(())],
        compiler_params=pltpu.CompilerParams(
            dimension_semantics=("arbitrary",), vmem_limit_bytes=VMEM_LIMIT),
        name="ffn",
    )(*operands)


def _split3(a):
    hi = a.astype(BF16)
    r = a - hi.astype(F32)
    mid = r.astype(BF16)
    lo = (r - mid.astype(F32)).astype(BF16)
    return hi, mid, lo


def _interleave(primary, filler):
    for n, step in enumerate(primary):
        step()
        if n < len(filler):
            filler[n]()
    for extra in filler[len(primary):]:
        extra()


def _gla_kernel(x_ref, s0_ref, gmix_ref, wq_ref, wk_ref, wv_ref, wg_ref, wlow_ref, wgk_ref, bgk_ref,
                gout_ref, wout_ref, y_ref, s_ref, st_s, *, nb, tl, n_sub):
    l = pl.program_id(1)
    heads = GLA_HEADS
    dk = wq_ref.shape[1] // heads
    dv = wv_ref.shape[1] // heads
    rows_sub = nb * tl // n_sub
    n_chunk = rows_sub // CHUNK
    d = x_ref.shape[-1]

    @pl.when(l == 0)
    def _():
        st_s[...] = s0_ref[...]

    ti = lax.broadcasted_iota(jnp.int32, (CHUNK, CHUNK), 0)
    si = lax.broadcasted_iota(jnp.int32, (CHUNK, CHUNK), 1)
    causal = si <= ti
    tri = jnp.where(causal, 1.0, 0.0).astype(BF16)

    def rows_of(ref, sub):
        row0 = sub * rows_sub
        if rows_sub >= tl:
            return ref.at[row0 // tl:(row0 + rows_sub) // tl]
        return ref.at[row0 // tl, row0 % tl:row0 % tl + rows_sub, :]

    def in_projection(sub):
        env = {}

        def start():
            env["x"] = rows_of(x_ref, sub)[...].reshape(rows_sub, d)
            env["h"] = _rms(env["x"], gmix_ref[...]).astype(BF16)

        def piece(name, ref, lo, hi):
            return lambda: env.setdefault(name, []).append(_dot(env["h"], ref[:, lo:hi]))

        thunks = [start]
        for name, ref in (("low", wlow_ref), ("q", wq_ref), ("k", wk_ref), ("v", wv_ref), ("g", wg_ref)):
            width = ref.shape[1]
            thunks += [piece(name, ref, lo, min(lo + MXU_COLS, width)) for lo in range(0, width, MXU_COLS)]
        return thunks, env

    def recurrence(sub, env):
        row0 = sub * rows_sub
        cat = lambda name: jnp.concatenate(env[name], axis=-1)
        loc = {}

        def start():
            loc["q"] = cat("q") * (dk ** -0.5)
            loc["k"] = cat("k")
            loc["v"] = cat("v").astype(BF16)
            z = _dot(cat("low").astype(BF16), wgk_ref[...]) + bgk_ref[...]
            loc["log_a"] = -(jnp.maximum(-z, 0.0) + jnp.log1p(jnp.exp(-jnp.abs(z)))) / GLA_GATE_NORM
            loc["state"] = {}
            loc["o"] = [[None] * n_chunk for _ in range(heads)]

        def cumsum(c):
            rs = slice(c * CHUNK, (c + 1) * CHUNK)
            a_hi, a_mid, a_lo = _split3(loc["log_a"][rs, :])
            loc["cum", c] = _dot(tri, a_hi) + _dot(tri, a_mid) + _dot(tri, a_lo)

        def scores(c):
            rs = slice(c * CHUNK, (c + 1) * CHUNK)
            cum = loc["cum", c]
            cum_last = cum[CHUNK - 1:CHUNK, :]
            q_dec = (loc["q"][rs, :] * jnp.exp(cum)).astype(BF16)
            k_inv = (loc["k"][rs, :] * jnp.exp(-cum)).astype(BF16)
            loc["k_tail", c] = (loc["k"][rs, :] * jnp.exp(cum_last - cum)).astype(BF16)
            loc["q_dec", c] = q_dec
            dec = jnp.broadcast_to(jnp.exp(cum_last), (F32_SUBLANES, heads * dk))
            loc["dec", c] = [dec[:, hd * dk:(hd + 1) * dk].T[:, :1] for hd in range(heads)]
            loc["scores", c] = [
                jnp.where(causal, _dot_nt(q_dec[:, hd * dk:(hd + 1) * dk], k_inv[:, hd * dk:(hd + 1) * dk]), 0.0)
                .astype(BF16) for hd in range(heads)]

        def values(c):
            rs = slice(c * CHUNK, (c + 1) * CHUNK)
            vs = [loc["v"][rs, hd * dv:(hd + 1) * dv] for hd in range(heads)]
            loc["o_intra", c] = [_dot(loc["scores", c][hd], vs[hd]) for hd in range(heads)]
            loc["kv", c] = [_dot_tn(loc["k_tail", c][:, hd * dk:(hd + 1) * dk], vs[hd]) for hd in range(heads)]

        def carry(c):
            b = (row0 + c * CHUNK) // tl
            for hd in range(heads):
                ks = slice(hd * dk, (hd + 1) * dk)
                st = loc["state"].get((b, hd))
                if st is None:
                    st = st_s[b, hd]
                loc["o"][hd][c] = loc["o_intra", c][hd] + _dot(loc["q_dec", c][:, ks], st.astype(BF16))
                loc["state"][(b, hd)] = st * loc["dec", c][hd] + loc["kv", c][hd]

        def finish():
            for (b, hd), st in loc["state"].items():
                st_s[b, hd] = st
            env["o"] = [jnp.concatenate(loc["o"][hd], axis=0) for hd in range(heads)]

        stages = (cumsum, scores, values, carry)
        thunks = [start]
        for step in range(n_chunk + len(stages) - 1):
            for st in reversed(range(len(stages))):
                if 0 <= step - st < n_chunk:
                    thunks.append(functools.partial(stages[st], step - st))
        return thunks + [finish]

    def out_projection(sub, env):
        loc = {}

        def gate():
            g = jnp.concatenate(env["g"], axis=-1)
            loc["gated"] = jnp.concatenate(
                [(_rms(env["o"][hd], gout_ref[...]) * _silu(g[:, hd * dv:(hd + 1) * dv])).astype(BF16)
                 for hd in range(heads)], axis=-1)

        def piece(lo):
            return lambda: loc.setdefault("y", []).append(_dot(loc["gated"], wout_ref[:, lo:lo + MXU_COLS]))

        def store():
            y = env["x"] + jnp.concatenate(loc["y"], axis=-1)
            dst = rows_of(y_ref, sub)
            dst[...] = y.reshape(dst.shape)

        return [gate] + [piece(lo) for lo in range(0, d, MXU_COLS)] + [store]

    proj = [in_projection(sub) for sub in range(n_sub)]
    for step in proj[0][0]:
        step()
    for sub in range(n_sub):
        filler = []
        if sub + 1 < n_sub:
            filler += proj[sub + 1][0]
        if sub >= 1:
            filler += out_projection(sub - 1, proj[sub - 1][1])
        _interleave(recurrence(sub, proj[sub][1]), filler)
    for step in out_projection(n_sub - 1, proj[n_sub - 1][1]):
        step()

    @pl.when(l == pl.num_programs(1) - 1)
    def _():
        s_ref[...] = st_s[...]


def _gla_call(x, s0, gmix, wq, wk, wv, wg, wlow, wgk, bgk, gout, wout, *, nb, tl, n_sub=GLA_ROW_GROUPS):
    bsz, seq, d = x.shape
    heads = GLA_HEADS
    dk = wq.shape[1] // heads
    dv = wv.shape[1] // heads
    rows_sub = nb * tl // n_sub
    assert bsz % nb == 0 and seq % tl == 0 and tl % CHUNK == 0 and (nb * tl) % n_sub == 0
    assert rows_sub % CHUNK == 0 and (rows_sub % tl == 0 or tl % rows_sub == 0)
    return pl.pallas_call(
        functools.partial(_gla_kernel, nb=nb, tl=tl, n_sub=n_sub),
        out_shape=(jax.ShapeDtypeStruct((bsz, seq, d), F32),
                   jax.ShapeDtypeStruct((bsz, heads, dk, dv), F32)),
        grid=(bsz // nb, seq // tl),
        in_specs=[
            pl.BlockSpec((nb, tl, d), lambda b, l: (b, l, 0)),
            pl.BlockSpec((nb, heads, dk, dv), lambda b, l: (b, 0, 0, 0)),
            _resident(gmix.shape), _resident(wq.shape), _resident(wk.shape), _resident(wv.shape),
            _resident(wg.shape), _resident(wlow.shape), _resident(wgk.shape), _resident(bgk.shape),
            _resident(gout.shape), _resident(wout.shape),
        ],
        out_specs=(pl.BlockSpec((nb, tl, d), lambda b, l: (b, l, 0)),
                   pl.BlockSpec((nb, heads, dk, dv), lambda b, l: (b, 0, 0, 0))),
        scratch_shapes=[pltpu.VMEM((nb, heads, dk, dv), F32)],
        compiler_params=pltpu.CompilerParams(
            dimension_semantics=("parallel", "arbitrary"), vmem_limit_bytes=VMEM_LIMIT),
        name="gla",
    )(x, s0, gmix, wq, wk, wv, wg, wlow, wgk, bgk, gout, wout)


def _mla_proj_kernel(x_ref, rot_ref, gmix_ref, wdq_ref, wdkv_ref, wdk_ref, qn_ref, kvn_ref,
                     wnope_ref, wr_ref, wkey_ref, wval_ref,
                     q_ref, k_ref, vt_ref, ckv_ref, kpe_ref, *, absorb, q_scale):
    nbp, tm, d = x_ref.shape
    rows = nbp * tm
    x = x_ref[...].reshape(rows, d)
    h = _rms(x, gmix_ref[...]).astype(BF16)
    rot = jnp.concatenate([rot_ref[...]] * nbp, axis=0)

    def rope(pair):
        prod = pair * rot
        return prod + pltpu.roll(prod, MLA_ROPE, axis=1)

    ckv = _rms(_dot(h, wdkv_ref[...]), kvn_ref[...])
    rope_lanes = lax.broadcasted_iota(jnp.int32, (rows, LANES), 1) < MLA_ROPE
    kpe = jnp.where(rope_lanes, rope(_dot(h, wdk_ref[...])), 0.0)
    ckv_ref[...] = ckv.reshape(nbp, tm, MLA_KV_LORA)
    kpe_ref[...] = kpe[:, :MLA_ROPE].reshape(nbp, tm, MLA_ROPE)
    ckv_b = ckv.astype(BF16)
    kpe_b = kpe.astype(BF16)
    ckv_t = ckv.T.astype(BF16)

    hq = _rms(_dot(h, wdq_ref[...]), qn_ref[...]).astype(BF16)
    q_nope = _dot(hq, wnope_ref[...])
    q_r = _dot(hq, wr_ref[...])

    if absorb:
        k_ref[:, :, :MLA_KV_LORA] = ckv_b.reshape(nbp, tm, MLA_KV_LORA)
        k_ref[:, :, MLA_KV_LORA:] = kpe_b.reshape(nbp, tm, LANES)
        for b in range(nbp):
            vt_ref[b, 0] = ckv_t[:, b * tm:(b + 1) * tm]
    else:
        k_nope = _dot(ckv_b, wkey_ref[...]).astype(BF16)
        v_t = _dot(wval_ref[...], ckv_t).astype(BF16)
        ones = jnp.ones((V_ONES_ROWS, tm), BF16)
    for hd in range(MLA_HEADS):
        hs = slice(hd * LANES, (hd + 1) * LANES)
        q_pe = rope(q_r[:, hs])
        if absorb:
            q_lat = _dot(q_nope[:, hs].astype(BF16), wkey_ref[hd])
            q_t = jnp.concatenate([q_lat, q_pe], axis=-1).T.astype(BF16)
            for b in range(nbp):
                q_ref[b, 0, :, hd * tm:(hd + 1) * tm] = q_t[:, b * tm:(b + 1) * tm]
        else:
            for b in range(nbp):
                bs = slice(b * tm, (b + 1) * tm)
                q_ref[b, hd, 0, :MLA_NOPE] = (q_nope[bs, hs] * q_scale).T.astype(BF16)
                q_ref[b, hd, 0, MLA_NOPE:] = (q_pe[bs] * q_scale).T.astype(BF16)
                k_ref[b, hd, :, :MLA_NOPE] = k_nope[bs, hs]
                k_ref[b, hd, :, MLA_NOPE:] = kpe_b[bs]
                vt_ref[b, hd, 0, :MLA_V] = v_t[hd * MLA_V:(hd + 1) * MLA_V, bs]
                vt_ref[b, hd, 0, MLA_V:] = ones


def _mla_proj_call(x, rot, gmix, wdq, wdkv, wdk, qn, kvn, wnope, wr, wkey, wval, *, tm, nbp, absorb, q_scale=1.0):
    bsz, seq, d = x.shape
    assert seq % tm == 0 and bsz % nbp == 0
    weights = (gmix, wdq, wdkv, wdk, qn, kvn, wnope, wr, wkey, wval)
    nt = seq // tm
    if absorb:
        q_k_vt = (jax.ShapeDtypeStruct((bsz, nt, KCAT, MLA_HEADS * tm), BF16),
                  jax.ShapeDtypeStruct((bsz, seq, KCAT), BF16),
                  jax.ShapeDtypeStruct((bsz, nt, MLA_KV_LORA, tm), BF16))
        q_k_vt_specs = (pl.BlockSpec((nbp, 1, KCAT, MLA_HEADS * tm), lambda b, l: (b, l, 0, 0)),
                        pl.BlockSpec((nbp, tm, KCAT), lambda b, l: (b, l, 0)),
                        pl.BlockSpec((nbp, 1, MLA_KV_LORA, tm), lambda b, l: (b, l, 0, 0)))
    else:
        qk_dim = MLA_NOPE + LANES
        v_rows = MLA_V + V_ONES_ROWS
        q_k_vt = (jax.ShapeDtypeStruct((bsz, MLA_HEADS, nt, qk_dim, tm), BF16),
                  jax.ShapeDtypeStruct((bsz, MLA_HEADS, seq, qk_dim), BF16),
                  jax.ShapeDtypeStruct((bsz, MLA_HEADS, nt, v_rows, tm), BF16))
        q_k_vt_specs = (pl.BlockSpec((nbp, MLA_HEADS, 1, qk_dim, tm), lambda b, l: (b, 0, l, 0, 0)),
                        pl.BlockSpec((nbp, MLA_HEADS, tm, qk_dim), lambda b, l: (b, 0, l, 0)),
                        pl.BlockSpec((nbp, MLA_HEADS, 1, v_rows, tm), lambda b, l: (b, 0, l, 0, 0)))
    return pl.pallas_call(
        functools.partial(_mla_proj_kernel, absorb=absorb, q_scale=q_scale),
        out_shape=q_k_vt + (jax.ShapeDtypeStruct((bsz, seq, MLA_KV_LORA), F32),
                            jax.ShapeDtypeStruct((bsz, seq, MLA_ROPE), F32)),
        grid=(bsz // nbp, nt),
        in_specs=[
            pl.BlockSpec((nbp, tm, d), lambda b, l: (b, l, 0)),
            pl.BlockSpec((tm, LANES), lambda b, l: (l, 0)),
        ] + [_resident(w.shape) for w in weights],
        out_specs=q_k_vt_specs + (pl.BlockSpec((nbp, tm, MLA_KV_LORA), lambda b, l: (b, l, 0)),
                                  pl.BlockSpec((nbp, tm, MLA_ROPE), lambda b, l: (b, l, 0))),
        compiler_params=pltpu.CompilerParams(
            dimension_semantics=("parallel", "parallel"), vmem_limit_bytes=VMEM_LIMIT),
        name="mla_proj",
    )(x, rot, *weights)


def _exp2_scaled(t, c):
    return jnp.exp2(t if c == 1.0 else t * c)


def _tile_update(s, vt, cs, m_s, l_s, acc_s, lag_s, c, one_pass):
    m_prev = m_s[:, cs]
    mx = jnp.max(s, axis=0, keepdims=True)
    m_new = jnp.maximum(m_prev, mx)
    alpha = _exp2_scaled(m_prev - m_new, c)
    if one_pass:
        p = _exp2_scaled(s - m_prev, c)
        if l_s is not None:
            l_s[:, cs] = (l_s[:, cs] + jnp.sum(p, axis=0, keepdims=True)) * alpha
        acc_s[:, cs] = (acc_s[:, cs] + _dot(vt, p.astype(BF16))) * alpha
        lag_s[...] = jnp.maximum(lag_s[...], (mx - m_prev) * c)
    else:
        p = _exp2_scaled(s - m_new, c)
        if l_s is not None:
            l_s[:, cs] = alpha * l_s[:, cs] + jnp.sum(p, axis=0, keepdims=True)
        acc_s[:, cs] = acc_s[:, cs] * alpha + _dot(vt, p.astype(BF16))
    m_s[:, cs] = m_new


def _attend(n_full, n_blk, cb, first_key_scores, diag_scores, diag_values, tile_scores, tile_values,
            m_s, l_s, acc_s, lag_s, c):
    def cols(i):
        return slice(i * cb, (i + 1) * cb)

    def run(one_pass):
        lag_s[...] = jnp.zeros(lag_s.shape, F32)
        for i in range(n_blk):
            m_s[:, cols(i)] = first_key_scores(i)
            acc_s[:, cols(i)] = jnp.zeros((acc_s.shape[0], cb), F32)
            if l_s is not None:
                l_s[:, cols(i)] = jnp.zeros((1, cb), F32)

        def update(units):
            s_next = units[0][0]()
            for u, (_, values, i) in enumerate(units):
                s = s_next
                if u + 1 < len(units):
                    s_next = units[u + 1][0]()
                _tile_update(s, values(), cols(i), m_s, l_s, acc_s, lag_s, c, one_pass)

        def full_tiles(js):
            return [(functools.partial(tile_scores, j, i), functools.partial(tile_values, j, i), i)
                    for j in js for i in range(n_blk)]

        update([(functools.partial(diag_scores, i), functools.partial(diag_values, i), i) for i in range(n_blk)])

        def pair(jj, carry):
            update(full_tiles((2 * jj, 2 * jj + 1)))
            return carry

        if isinstance(n_full, int):
            for jj in range(n_full // 2):
                pair(jj, 0)
            if n_full % 2:
                update(full_tiles((n_full - 1,)))
        else:
            lax.fori_loop(0, n_full // 2, pair, 0)

            @pl.when(n_full % 2 == 1)
            def _():
                update(full_tiles((n_full - 1,)))

    run(one_pass=True)

    @pl.when(jnp.max(lag_s[...]) > ATTN_LAG_LIMIT)
    def _():
        run(one_pass=False)


def _block_causal(tq, cb, col0):
    ki = lax.broadcasted_iota(jnp.int32, (tq, cb), 0) // CHUNK
    qi = ((lax.broadcasted_iota(jnp.int32, (tq, cb), 1) + col0) % tq) // CHUNK
    return ki <= qi


def _mla_attn_kernel(qt_ref, pckv_ref, pkpet_ref, kdiag_ref, vtdiag_ref, x_ref, wuvt_ref, wout_ref, y_ref,
                     o_s, m_s, l_s, acc_s, lag_s, *, tq, tk, cb, scale):
    nbs = qt_ref.shape[0]
    heads = o_s.shape[0]
    blk_per_seq = heads * tq // cb
    past_tiles = pckv_ref.shape[1] // tk

    def qt_blk(i):
        return qt_ref[i // blk_per_seq, 0, :, (i % blk_per_seq) * cb:(i % blk_per_seq + 1) * cb]

    def diag_scores(i):
        visible = _block_causal(tq, cb, (i % blk_per_seq) * cb)
        return jnp.where(visible, _dot(kdiag_ref[i // blk_per_seq], qt_blk(i)), -jnp.inf)

    def tile_scores(j, i):
        rows = slice(j * tk, (j + 1) * tk)
        latent = pckv_ref[i // blk_per_seq, rows, :].astype(BF16)
        rotary_t = pkpet_ref[i // blk_per_seq, :, rows].astype(BF16)
        qt = qt_blk(i)
        return _dot(latent, qt[:MLA_KV_LORA]) + _dot_tn(rotary_t, qt[MLA_KV_LORA:MLA_KV_LORA + MLA_ROPE])

    def tile_values(j, i):
        return pckv_ref[i // blk_per_seq, j * tk:(j + 1) * tk, :].T.astype(BF16)

    def first_key_scores(i):
        return _dot(kdiag_ref[i // blk_per_seq, :BF16_SUBLANES, :], qt_blk(i))[:1]

    _attend(past_tiles, nbs * blk_per_seq, cb, first_key_scores, diag_scores,
            lambda i: vtdiag_ref[i // blk_per_seq, 0], tile_scores, tile_values,
            m_s, l_s, acc_s, lag_s, scale * LOG2E)

    for s in range(nbs):
        for hd in range(heads):
            cs = slice((s * heads + hd) * tq, (s * heads + hd + 1) * tq)
            o_lat_t = (acc_s[:, cs] * (1.0 / l_s[:, cs])).astype(BF16)
            o_s[hd] = _dot(wuvt_ref[hd], o_lat_t).astype(BF16)
        o_t = o_s[...].reshape(o_s.shape[0] * o_s.shape[1], tq)
        y_ref[s] = x_ref[s] + _dot_tn(o_t, wout_ref[...])


def _mla_attn_call(qt, past_ckv, past_kpe_t, kdiag, vtdiag, x, wuvt, wout, *, tk, nbs, scale):
    bsz, tq, d = x.shape
    heads = wuvt.shape[0]
    past = past_ckv.shape[1]
    assert past % tk == 0 and bsz % nbs == 0 and qt.shape == (bsz, 1, KCAT, heads * tq)
    assert past_kpe_t.shape == (bsz, MLA_ROPE, past) and vtdiag.shape == (bsz, 1, MLA_KV_LORA, tq)
    cb = max(tq, min(heads * tq, ATTN_COL_BLOCK))
    assert cb % tq == 0 and (heads * tq) % cb == 0
    v_dim = wuvt.shape[1]
    return pl.pallas_call(
        functools.partial(_mla_attn_kernel, tq=tq, tk=tk, cb=cb, scale=scale),
        out_shape=jax.ShapeDtypeStruct((bsz, tq, d), F32),
        grid=(bsz // nbs,),
        in_specs=[
            pl.BlockSpec((nbs, 1, KCAT, heads * tq), lambda b: (b, 0, 0, 0)),
            pl.BlockSpec((nbs, past, MLA_KV_LORA), lambda b: (b, 0, 0)),
            pl.BlockSpec((nbs, MLA_ROPE, past), lambda b: (b, 0, 0)),
            pl.BlockSpec((nbs, tq, KCAT), lambda b: (b, 0, 0)),
            pl.BlockSpec((nbs, 1, MLA_KV_LORA, tq), lambda b: (b, 0, 0, 0)),
            pl.BlockSpec((nbs, tq, d), lambda b: (b, 0, 0)),
            _resident(wuvt.shape),
            _resident(wout.shape),
        ],
        out_specs=pl.BlockSpec((nbs, tq, d), lambda b: (b, 0, 0)),
        scratch_shapes=[
            pltpu.VMEM((heads, v_dim, tq), BF16),
            pltpu.VMEM((1, nbs * heads * tq), F32),
            pltpu.VMEM((1, nbs * heads * tq), F32),
            pltpu.VMEM((MLA_KV_LORA, nbs * heads * tq), F32),
            pltpu.VMEM((1, cb), F32),
        ],
        compiler_params=pltpu.CompilerParams(
            dimension_semantics=("parallel",), vmem_limit_bytes=VMEM_LIMIT),
        name="mla_attn",
    )(qt, past_ckv, past_kpe_t, kdiag, vtdiag, x, wuvt, wout)


def _mha_attn_kernel(qt_ref, k_ref, vt_ref, kdiag_ref, vtdiag_ref, ot_ref, m_s, acc_s, lag_s, *, tq):
    qb = pl.program_id(2)
    hp = qt_ref.shape[1]
    visible = _block_causal(tq, tq, 0)

    def diag_scores(i):
        return jnp.where(visible, _dot(kdiag_ref[0, i], qt_ref[0, i, 0]), -jnp.inf)

    def tile_scores(j, i):
        return _dot(k_ref[0, i, pl.ds(pl.multiple_of(j * tq, tq), tq), :], qt_ref[0, i, 0])

    def first_key_scores(i):
        return _dot(kdiag_ref[0, i, :BF16_SUBLANES, :], qt_ref[0, i, 0])[:1]

    _attend(qb, hp, tq, first_key_scores, diag_scores, lambda i: vtdiag_ref[0, i, 0],
            tile_scores, lambda j, i: vt_ref[0, i, j], m_s, None, acc_s, lag_s, 1.0)

    for i in range(hp):
        cs = slice(i * tq, (i + 1) * tq)
        inv_l = 1.0 / acc_s[MLA_V:MLA_V + 1, cs]
        ot_ref[0, i * MLA_V:(i + 1) * MLA_V, :] = (acc_s[:MLA_V, cs] * inv_l).astype(BF16)


def _mha_attn_call(qt, k, vt, *, hp, tq):
    bsz, heads, seq, qk_dim = k.shape
    nq = seq // tq
    v_rows = MLA_V + V_ONES_ROWS
    assert qt.shape == (bsz, heads, nq, qk_dim, tq)
    assert seq % tq == 0 and heads % hp == 0 and vt.shape == (bsz, heads, nq, v_rows, tq)
    return pl.pallas_call(
        functools.partial(_mha_attn_kernel, tq=tq),
        out_shape=jax.ShapeDtypeStruct((bsz, heads * MLA_V, seq), BF16),
        grid=(bsz, heads // hp, nq),
        in_specs=[
            pl.BlockSpec((1, hp, 1, qk_dim, tq), lambda b, g, i: (b, g, i, 0, 0)),
            pl.BlockSpec((1, hp, seq, qk_dim), lambda b, g, i: (b, g, 0, 0)),
            pl.BlockSpec((1, hp, nq, v_rows, tq), lambda b, g, i: (b, g, 0, 0, 0), pipeline_mode=pl.Buffered(1)),
            pl.BlockSpec((1, hp, tq, qk_dim), lambda b, g, i: (b, g, i, 0)),
            pl.BlockSpec((1, hp, 1, v_rows, tq), lambda b, g, i: (b, g, i, 0, 0)),
        ],
        out_specs=pl.BlockSpec((1, hp * MLA_V, tq), lambda b, g, i: (b, g, i)),
        scratch_shapes=[
            pltpu.VMEM((1, hp * tq), F32),
            pltpu.VMEM((v_rows, hp * tq), F32),
            pltpu.VMEM((1, tq), F32),
        ],
        compiler_params=pltpu.CompilerParams(
            dimension_semantics=("parallel", "parallel", "arbitrary"), vmem_limit_bytes=VMEM_LIMIT),
        name="mha_attn",
    )(qt, k, vt, k, vt)


def _rope_table(pos0, n):
    half = MLA_ROPE // 2
    inv = ROPE_THETA ** (-jnp.arange(half, dtype=F32) / half)

    def cos_sin(p):
        ang = p.astype(F32)[:, None] * inv[None, :]
        return jnp.cos(ang), jnp.sin(ang)

    blk = ATTN_KEY_TILE
    if n > blk and n % blk == 0:
        ch, sh = cos_sin(pos0 + blk * jnp.arange(n // blk))
        cl, sl = cos_sin(jnp.arange(blk))
        cos = (ch[:, None, :] * cl[None] - sh[:, None, :] * sl[None]).reshape(n, half)
        sin = (sh[:, None, :] * cl[None] + ch[:, None, :] * sl[None]).reshape(n, half)
    else:
        cos, sin = cos_sin(pos0 + jnp.arange(n))
    return jnp.concatenate([cos, cos, sin, sin], axis=-1)


def _rotate_half_cols(w):
    half = w.shape[-1] // 2
    return jnp.concatenate([-w[..., half:], w[..., :half]], axis=-1)


def _pad_last(w, n):
    return jnp.concatenate([w, jnp.zeros(w.shape[:-1] + (n - w.shape[-1],), w.dtype)], axis=-1)


def _prep_gla(w_in, w_gk_up, b_gk, g_norm, w_out):
    qk = w_gk_up.shape[1]
    vw = w_out.shape[0]
    wq = w_in[:, :qk]
    wk = w_in[:, qk:2 * qk]
    wv = w_in[:, 2 * qk:2 * qk + vw]
    wg = w_in[:, 2 * qk + vw:2 * qk + 2 * vw]
    wlow = _pad_last(w_in[:, 2 * qk + 2 * vw:], LANES)
    rank = w_gk_up.shape[0]
    wgk = jnp.concatenate([w_gk_up, jnp.zeros((LANES - rank, qk), w_gk_up.dtype)], axis=0)
    return (wq.astype(BF16), wk.astype(BF16), wv.astype(BF16), wg.astype(BF16), wlow.astype(BF16),
            wgk.astype(BF16), b_gk[None, :], g_norm[None, :], w_out.astype(BF16))


def _prep_mla(w_down, q_norm, w_uq, kv_norm, w_uk, w_uv, w_out):
    wdq = w_down[:, :MLA_Q_LORA]
    wdkv = w_down[:, MLA_Q_LORA:MLA_Q_LORA + MLA_KV_LORA]
    wdk = w_down[:, MLA_Q_LORA + MLA_KV_LORA:]
    wdk = jnp.concatenate([wdk, _rotate_half_cols(wdk)], axis=-1)
    wuq = w_uq.reshape(MLA_Q_LORA, MLA_HEADS, MLA_NOPE + MLA_ROPE)
    wnope = wuq[..., :MLA_NOPE].reshape(MLA_Q_LORA, MLA_HEADS * MLA_NOPE)
    wrope = wuq[..., MLA_NOPE:]
    wr = jnp.concatenate([wrope, _rotate_half_cols(wrope)], axis=-1).reshape(MLA_Q_LORA, MLA_HEADS * LANES)
    wuk = jnp.transpose(w_uk, (1, 2, 0))
    wuvt = jnp.transpose(w_uv, (1, 2, 0))
    wuk_flat = w_uk.reshape(MLA_KV_LORA, MLA_HEADS * MLA_NOPE)
    wuv_flat_t = w_uv.reshape(MLA_KV_LORA, MLA_HEADS * MLA_V).T
    return dict(
        wdq=wdq.astype(BF16), wdkv=wdkv.astype(BF16), wdk=wdk.astype(BF16), qn=q_norm[None, :],
        kvn=kv_norm[None, :], wnope=wnope.astype(BF16), wr=wr.astype(BF16), wuk=wuk.astype(BF16),
        wuvt=wuvt.astype(BF16), wuk_flat=wuk_flat.astype(BF16), wuv_flat_t=wuv_flat_t.astype(BF16),
        wout=w_out.astype(BF16))


def kernel(x_prompt, x_sample, state_gla, cache_ckv, cache_kpe, norm_ffn1, w_ffn1_gate, w_ffn1_up, w_ffn1_down, norm_mix, norm_ffn2, w_ffn2_gate, w_ffn2_up, w_ffn2_down, w_gla_in, w_gla_gk_up, b_gla_gk, gla_out_norm, w_gla_out, w_mla_down, mla_q_norm, w_mla_uq, mla_kv_norm, w_mla_uk, w_mla_uv, w_mla_out, norm_final):
    depth = norm_ffn1.shape[0]
    n_mixers = 2
    d = x_prompt.shape[-1]
    mla_scale = (MLA_NOPE + MLA_ROPE) ** -0.5
    gf = norm_final[None, :]

    ffn1 = (norm_ffn1[:, None, :], w_ffn1_gate, w_ffn1_up, w_ffn1_down)
    ffn2 = (norm_ffn2[:, None, :], w_ffn2_gate, w_ffn2_up, w_ffn2_down)
    gla_w = [_prep_gla(w_gla_in[j], w_gla_gk_up[j], b_gla_gk[j], gla_out_norm[j], w_gla_out[j])
             for j in range(w_gla_in.shape[0])]
    mla_w = [_prep_mla(w_mla_down[j], mla_q_norm[j], w_mla_uq[j], mla_kv_norm[j], w_mla_uk[j], w_mla_uv[j],
                       w_mla_out[j]) for j in range(w_mla_down.shape[0])]

    def run(x, pos0, gla_s0, past_ckv, past_kpe, *, gla_nb, gla_tl, attn_tq, proj_nb):
        bsz, seq, _ = x.shape
        rot = _rope_table(pos0, seq)
        new_gla, new_ckv, new_kpe = [], [], []
        pending = {}

        def ffn(x, w, layer, final_norm=False):
            return _ffn_call(x.reshape(bsz * seq, d), w[0], w[1], w[2], w[3], gf, layer,
                             final_norm=final_norm, **pending).reshape(bsz, seq, d)

        for i in range(depth):
            x = ffn(x, ffn1, i)
            j = i // n_mixers
            gm = norm_mix[i][None, :]
            if i % n_mixers == 0:
                x, s_fin = _gla_call(x, gla_s0[j], gm, *gla_w[j], nb=gla_nb, tl=gla_tl)
                new_gla.append(s_fin)
            else:
                w = mla_w[j]
                absorb = past_ckv is not None
                q, k, vt, ckv, kpe = _mla_proj_call(
                    x, rot, gm, w["wdq"], w["wdkv"], w["wdk"], w["qn"], w["kvn"],
                    w["wnope"], w["wr"], w["wuk"] if absorb else w["wuk_flat"], w["wuv_flat_t"],
                    tm=attn_tq, nbp=proj_nb, absorb=absorb, q_scale=1.0 if absorb else mla_scale * LOG2E)
                if absorb:
                    x = _mla_attn_call(q, past_ckv[j], jnp.swapaxes(past_kpe[j], 1, 2), k, vt, x, w["wuvt"], w["wout"],
                                       tk=ATTN_KEY_TILE, nbs=ATTN_SEQ_GROUP, scale=mla_scale)
                else:
                    pending = dict(attn_t=_mha_attn_call(q, k, vt, hp=ATTN_HEAD_GROUP, tq=attn_tq), w_out=w["wout"])
                new_ckv.append(ckv)
                new_kpe.append(kpe)
            x = ffn(x, ffn2, i, final_norm=(i == depth - 1))
            pending = {}
        return x, jnp.stack(new_gla), jnp.stack(new_ckv), jnp.stack(new_kpe)

    bp, lp = x_prompt.shape[0], x_prompt.shape[1]
    n_gla = w_gla_in.shape[0]
    s0_p = jnp.zeros((n_gla, bp) + state_gla.shape[2:], x_prompt.dtype)
    y_p, gla_p, ckv_p, kpe_p = run(
        x_prompt, 0, s0_p, None, None, gla_nb=1, gla_tl=512, attn_tq=ATTN_KEY_TILE, proj_nb=1)

    past_len = cache_ckv.shape[2]
    ls = x_sample.shape[1]
    y_s, gla_s, ckv_s, kpe_s = run(
        x_sample, past_len, state_gla, cache_ckv, cache_kpe, gla_nb=8, gla_tl=ls, attn_tq=ls,
        proj_nb=8)

    return (y_p, y_s, gla_p, ckv_p, kpe_p, gla_s, ckv_s, kpe_s)
```
